```python
import functools
import jax, jax.numpy as jnp
from jax import lax
import numpy as np

D_MODEL = 1024
BATCH = 8
SEQ = 2048
DEPTH = 1
DEC_BATCH = 32
DEC_SEQ = 8
PAST_LEN = 16384
PAGE_SIZE = 128

CONV_CH = D_MODEL // 2
CONV_WIDTH = 31
N_HEADS = 8
HEAD_DIM = (D_MODEL // 2) // N_HEADS
ATT_W = N_HEADS * HEAD_DIM
N_IDX_HEADS = 8
IDX_DIM = 64
IDX_TOPK_MAX = 256
Q_BLOCK = 128
N_GROUPS = 4
EXPERTS_PER_GROUP = 8
N_EXPERTS = N_GROUPS * EXPERTS_PER_GROUP
TOP_K_IN_GROUP = 2
D_EXPERT = D_MODEL // 4
PLE_DIM = 256
LN_EPS = 1e-5
DEEPNORM_ALPHA = (2.0 * DEPTH) ** 0.25
DEEPNORM_BETA = (8.0 * DEPTH) ** -0.25
IDX_SCALE = (N_IDX_HEADS * IDX_DIM) ** -0.5
IN_SIZES = (CONV_CH, CONV_CH, ATT_W, ATT_W, ATT_W, N_IDX_HEADS * IDX_DIM, IDX_DIM, N_IDX_HEADS, D_MODEL, D_MODEL)
IN_COLS = sum(IN_SIZES)
IN_SPLITS = tuple(int(c) for c in np.cumsum(IN_SIZES)[:-1])

kernel_name = 'hybrid_conv_dsa_hmoe_deepnorm_step'


def layer_norm(x, g, b):
    xf = x.astype(jnp.float32)
    mu = jnp.mean(xf, axis=-1, keepdims=True)
    var = jnp.mean(jnp.square(xf - mu), axis=-1, keepdims=True)
    y = (xf - mu) * lax.rsqrt(var + LN_EPS) * g.astype(jnp.float32) + b.astype(jnp.float32)
    return y.astype(x.dtype)


def causal_depthwise_conv(buf, w, b):
    y = lax.conv_general_dilated(buf, w[:, None, :].astype(buf.dtype), window_strides=(1,), padding='VALID',
                                 dimension_numbers=('NWC', 'WIO', 'NWC'), feature_group_count=buf.shape[-1])
    return y + b


def indexer_scores(qi, wi, ki):
    s = jax.nn.relu(jnp.einsum('bqhd,bsd->bqhs', qi, ki).astype(jnp.float32))
    return jnp.einsum('bqhs,bqh->bqs', s, wi.astype(jnp.float32)) * IDX_SCALE


def select_keys(scores, q_pos, topk):
    key_pos = jnp.arange(scores.shape[-1])
    admissible = key_pos[None, None, :] <= q_pos[None, :, None]
    masked = jnp.where(admissible, scores, -jnp.inf)
    _, idx = lax.top_k(masked, topk)
    valid = idx <= q_pos[None, :, None]
    return idx, valid


def gather_rows(a, idx):
    return jax.vmap(lambda ab, ib: ab[ib])(a, idx)


def attend_selected(q, kg, vg, valid):
    logits = jnp.einsum('bqhd,bqkhd->bqhk', q, kg).astype(jnp.float32) * (HEAD_DIM ** -0.5)
    logits = jnp.where(valid[:, :, None, :], logits, -jnp.inf)
    p = jax.nn.softmax(logits, axis=-1).astype(vg.dtype)
    return jnp.einsum('bqhk,bqkhd->bqhd', p, vg)


def sparse_attention_prompt(q, k, v, qi, ki, wi):
    B, T = q.shape[0], q.shape[1]
    qblk = min(Q_BLOCK, T)
    nb = T // qblk
    topk = min(IDX_TOPK_MAX, T // 4)

    def blockify(a):
        return jnp.moveaxis(a.reshape((B, nb, qblk) + a.shape[2:]), 1, 0)

    def one_block(args):
        q_b, qi_b, wi_b, start = args
        q_pos = start + jnp.arange(qblk)
        idx, valid = select_keys(indexer_scores(qi_b, wi_b, ki), q_pos, topk)
        return attend_selected(q_b, gather_rows(k, idx), gather_rows(v, idx), valid)

    starts = jnp.arange(nb) * qblk
    out = lax.map(one_block, (blockify(q), blockify(qi), blockify(wi), starts))
    return jnp.moveaxis(out, 0, 1).reshape(q.shape)


def sparse_attention_sample(q, k_new, v_new, qi, ki_new, wi, cache_k, cache_v, cache_kidx, page_table, layer):
    B, T = q.shape[0], q.shape[1]
    page = cache_k.shape[2]
    past = page_table.shape[1] * page
    ki_past = cache_kidx[layer, page_table].reshape(B, past, IDX_DIM)
    ki_all = jnp.concatenate([ki_past, ki_new.astype(ki_past.dtype)], axis=1)
    q_pos = past + jnp.arange(T)
    topk = min(IDX_TOPK_MAX, (past + T) // 4)
    idx, valid = select_keys(indexer_scores(qi, wi, ki_all), q_pos, topk)
    is_past = idx < past
    pidx = jnp.minimum(idx, past - 1)
    phys = jnp.take_along_axis(page_table, (pidx // page).reshape(B, -1), axis=1).reshape(pidx.shape)
    off = pidx % page
    nidx = jnp.clip(idx - past, 0, T - 1)
    sel = is_past[..., None, None]
    kg = jnp.where(sel, cache_k[layer, phys, off], gather_rows(k_new, nidx))
    vg = jnp.where(sel, cache_v[layer, phys, off], gather_rows(v_new, nidx))
    return attend_selected(q, kg, vg, valid)


def token_mixing(h, conv_prefix, attend_fn, w_in, b_gate, conv_w, conv_b, lnc_g, lnc_b, w_conv_out, w_attn_out, w_o):
    B, T, _ = h.shape
    a_glu, b_glu, q, k, v, qi, ki, wi, g_c, g_a = jnp.split(h @ w_in, IN_SPLITS, axis=-1)
    glu = a_glu * jax.nn.sigmoid(b_glu)
    buf = jnp.concatenate([conv_prefix.astype(glu.dtype), glu], axis=1)
    c = jax.nn.silu(layer_norm(causal_depthwise_conv(buf, conv_w, conv_b), lnc_g, lnc_b))
    conv_branch = c @ w_conv_out
    q = q.reshape(B, T, N_HEADS, HEAD_DIM)
    k = k.reshape(B, T, N_HEADS, HEAD_DIM)
    v = v.reshape(B, T, N_HEADS, HEAD_DIM)
    qi = qi.reshape(B, T, N_IDX_HEADS, IDX_DIM)
    o = attend_fn(q, k, v, qi, ki, wi)
    attn_branch = o.reshape(B, T, ATT_W) @ w_attn_out
    gc = jax.nn.sigmoid(g_c + b_gate[:D_MODEL])
    ga = jax.nn.sigmoid(g_a + b_gate[D_MODEL:])
    mix = (gc * conv_branch + ga * attn_branch) @ w_o
    return mix, k, v, ki, buf[:, -(CONV_WIDTH - 1):]


def hier_moe(x, w_rg, b_rg, w_re, b_re, w_eg, w_eu, w_ed):
    N = x.shape[0]
    gl = (x @ w_rg).astype(jnp.float32) + b_rg.astype(jnp.float32)
    pg = jax.nn.softmax(gl, axis=-1)
    grp = jnp.argmax(gl, axis=-1)
    el = ((x @ w_re).astype(jnp.float32) + b_re.astype(jnp.float32)).reshape(N, N_GROUPS, EXPERTS_PER_GROUP)
    el_g = jnp.take_along_axis(el, grp[:, None, None], axis=1)[:, 0]
    top_p, top_i = lax.top_k(jax.nn.softmax(el_g, axis=-1), TOP_K_IN_GROUP)
    top_p = top_p / jnp.sum(top_p, axis=-1, keepdims=True)
    gate = jnp.take_along_axis(pg, grp[:, None], axis=1) * top_p
    eid = grp[:, None] * EXPERTS_PER_GROUP + top_i
    combine = jnp.sum(jax.nn.one_hot(eid, N_EXPERTS, dtype=jnp.float32) * gate[..., None], axis=1)
    hg = jnp.einsum('nd,edf->nef', x, w_eg)
    hu = jnp.einsum('nd,edf->nef', x, w_eu)
    hidden = jax.nn.silu(hg) * hu * combine[..., None].astype(x.dtype)
    return jnp.einsum('nef,efd->nd', hidden, w_ed)


def channel_stage(x1, p, w_rg, b_rg, w_re, b_re, w_eg, w_eu, w_ed, w_pg, w_pp, ln2_g, ln2_b):
    B, T, D = x1.shape
    ffn = hier_moe(x1.reshape(B * T, D), w_rg, b_rg, w_re, b_re, w_eg, w_eu, w_ed).reshape(B, T, D)
    ple = jax.nn.sigmoid(x1 @ w_pg) * (p @ w_pp)
    return layer_norm(DEEPNORM_ALPHA * x1 + ffn + ple, ln2_g, ln2_b)


def setup_inputs(seed: int = 0) -> dict:
    key = jax.random.key(seed)
    ks = jax.random.split(key, 40)
    f32 = jnp.float32

    def nrm(i, shape, scale):
        return jax.random.normal(ks[i], shape, f32) * scale

    n_pages = PAST_LEN // PAGE_SIZE
    n_used = DEC_BATCH * n_pages
    n_pool = n_used + max(1, n_used // 4)
    page_table = jax.random.permutation(ks[0], n_pool)[:n_used].reshape(DEC_BATCH, n_pages).astype(jnp.int32)
    L = DEPTH
    return {
        'x_prompt': nrm(1, (BATCH, SEQ, D_MODEL), 1.0),
        'x_sample': nrm(2, (DEC_BATCH, DEC_SEQ, D_MODEL), 1.0),
        'cache_k': nrm(3, (L, n_pool, PAGE_SIZE, N_HEADS, HEAD_DIM), 1.0),
        'cache_v': nrm(4, (L, n_pool, PAGE_SIZE, N_HEADS, HEAD_DIM), 1.0),
        'cache_kidx': nrm(5, (L, n_pool, PAGE_SIZE, IDX_DIM), 1.0),
        'state_conv': nrm(6, (L, DEC_BATCH, CONV_WIDTH - 1, CONV_CH), 0.5),
        'page_table': page_table,
        'p_prompt': nrm(7, (L, BATCH, SEQ, PLE_DIM), 1.0),
        'p_sample': nrm(8, (L, DEC_BATCH, DEC_SEQ, PLE_DIM), 1.0),
        'ln0_g': 1.0 + nrm(9, (D_MODEL,), 0.02),
        'ln0_b': nrm(10, (D_MODEL,), 0.02),
        'w_in': nrm(11, (L, D_MODEL, IN_COLS), D_MODEL ** -0.5),
        'b_gate': nrm(12, (L, 2 * D_MODEL), 0.1),
        'conv_w': nrm(13, (L, CONV_WIDTH, CONV_CH), CONV_WIDTH ** -0.5),
        'conv_b': nrm(14, (L, CONV_CH), 0.02),
        'lnc_g': 1.0 + nrm(15, (L, CONV_CH), 0.02),
        'lnc_b': nrm(16, (L, CONV_CH), 0.02),
        'w_conv_out': nrm(17, (L, CONV_CH, D_MODEL), CONV_CH ** -0.5),
        'w_attn_out': nrm(18, (L, ATT_W, D_MODEL), ATT_W ** -0.5),
        'w_o': nrm(19, (L, D_MODEL, D_MODEL), DEEPNORM_BETA * D_MODEL ** -0.5),
        'ln1_g': 1.0 + nrm(20, (L, D_MODEL), 0.02),
        'ln1_b': nrm(21, (L, D_MODEL), 0.02),
        'w_rg': nrm(22, (L, D_MODEL, N_GROUPS), D_MODEL ** -0.5),
        'b_rg': nrm(23, (L, N_GROUPS), 0.01),
        'w_re': nrm(24, (L, D_MODEL, N_EXPERTS), D_MODEL ** -0.5),
        'b_re': nrm(25, (L, N_EXPERTS), 0.01),
        'w_eg': nrm(26, (L, N_EXPERTS, D_MODEL, D_EXPERT), D_MODEL ** -0.5),
        'w_eu': nrm(27, (L, N_EXPERTS, D_MODEL, D_EXPERT), D_MODEL ** -0.5),
        'w_ed': nrm(28, (L, N_EXPERTS, D_EXPERT, D_MODEL), DEEPNORM_BETA * D_EXPERT ** -0.5),
        'w_pg': nrm(29, (L, D_MODEL, D_MODEL), D_MODEL ** -0.5),
        'w_pp': nrm(30, (L, PLE_DIM, D_MODEL), DEEPNORM_BETA * PLE_DIM ** -0.5),
        'ln2_g': 1.0 + nrm(31, (L, D_MODEL), 0.02),
        'ln2_b': nrm(32, (L, D_MODEL), 0.02),
    }


def reference(x_prompt, x_sample, cache_k, cache_v, cache_kidx, state_conv, page_table, p_prompt, p_sample,
              ln0_g, ln0_b, w_in, b_gate, conv_w, conv_b, lnc_g, lnc_b, w_conv_out, w_attn_out, w_o,
              ln1_g, ln1_b, w_rg, b_rg, w_re, b_re, w_eg, w_eu, w_ed, w_pg, w_pp, ln2_g, ln2_b):
    hp = layer_norm(x_prompt, ln0_g, ln0_b)
    hs = layer_norm(x_sample, ln0_g, ln0_b)
    kp_l, vp_l, kip_l, cp_l = [], [], [], []
    ks_l, vs_l, kis_l, cs_l = [], [], [], []
    for l in range(DEPTH):
        mix_w = (w_in[l], b_gate[l], conv_w[l], conv_b[l], lnc_g[l], lnc_b[l], w_conv_out[l], w_attn_out[l], w_o[l])
        ffn_w = (w_rg[l], b_rg[l], w_re[l], b_re[l], w_eg[l], w_eu[l], w_ed[l], w_pg[l], w_pp[l], ln2_g[l], ln2_b[l])
        zero_prefix = jnp.zeros((hp.shape[0], CONV_WIDTH - 1, CONV_CH), hp.dtype)
        mix_p, k_p, v_p, ki_p, c_p = token_mixing(hp, zero_prefix, sparse_attention_prompt, *mix_w)
        x1p = layer_norm(DEEPNORM_ALPHA * hp + mix_p, ln1_g[l], ln1_b[l])
        hp = channel_stage(x1p, p_prompt[l], *ffn_w)
        sample_attend = functools.partial(sparse_attention_sample, cache_k=cache_k, cache_v=cache_v,
                                          cache_kidx=cache_kidx, page_table=page_table, layer=l)
        mix_s, k_s, v_s, ki_s, c_s = token_mixing(hs, state_conv[l], sample_attend, *mix_w)
        x1s = layer_norm(DEEPNORM_ALPHA * hs + mix_s, ln1_g[l], ln1_b[l])
        hs = channel_stage(x1s, p_sample[l], *ffn_w)
        kp_l.append(k_p); vp_l.append(v_p); kip_l.append(ki_p); cp_l.append(c_p)
        ks_l.append(k_s); vs_l.append(v_s); kis_l.append(ki_s); cs_l.append(c_s)
    new_k_prompt = jnp.stack(kp_l)
    new_v_prompt = jnp.stack(vp_l)
    new_kidx_prompt = jnp.stack(kip_l)
    new_conv_prompt = jnp.stack(cp_l)
    new_k_sample = jnp.stack(ks_l)
    new_v_sample = jnp.stack(vs_l)
    new_kidx_sample = jnp.stack(kis_l)
    new_conv_sample = jnp.stack(cs_l)
    return (hp, hs, new_k_prompt, new_v_prompt, new_kidx_prompt, new_conv_prompt,
            new_k_sample, new_v_sample, new_kidx_sample, new_conv_sample)
```

```python
import functools

import jax
import jax.numpy as jnp
from jax import lax
from jax.experimental import pallas as pl
from jax.experimental.pallas import tpu as pltpu

D_MODEL = 1024
CONV_CH = 512
CONV_WIDTH = 31
CONV_HIST = CONV_WIDTH - 1
N_HEADS = 8
HEAD_DIM = 64
ATT_W = N_HEADS * HEAD_DIM
N_IDX_HEADS = 8
IDX_DIM = 64
IDX_TOPK_MAX = 256
N_GROUPS = 4
EXPERTS_PER_GROUP = 8
N_EXPERTS = N_GROUPS * EXPERTS_PER_GROUP
D_EXPERT = 256
PLE_DIM = 256
LN_EPS = 1e-5
IDX_SCALE = (N_IDX_HEADS * IDX_DIM) ** -0.5
ATT_SCALE = HEAD_DIM ** -0.5
COL_GLU_A, COL_GLU_B, COL_Q, COL_K, COL_V, COL_QI, COL_KI, COL_WI, COL_GC = (
    0, 512, 1024, 1536, 2048, 2560, 3072, 3136, 3144)
COL_END = COL_GC + 2 * D_MODEL

LANES = 128
SUBLANES = 8
PREFIX_ROWS = 32
VMEM_LIMIT = 48 * 1024 * 1024

F32 = jnp.float32
BF16 = jnp.bfloat16
NEG_INF = float("-inf")
INT_MIN = -2 ** 31


def _cparams(sem):
    return pltpu.CompilerParams(dimension_semantics=sem, vmem_limit_bytes=VMEM_LIMIT)


def _layer_norm(x, g, b):
    mu = jnp.mean(x, axis=-1, keepdims=True)
    xc = x - mu
    var = jnp.mean(xc * xc, axis=-1, keepdims=True)
    return xc * lax.rsqrt(var + LN_EPS) * g + b


def _sigmoid(x):
    return 1.0 / (1.0 + jnp.exp(-x))


def _dot(a, b):
    return jnp.dot(a, b, preferred_element_type=F32)


def _dot_nt(a, b):
    return lax.dot_general(a, b, (((1,), (1,)), ((), ())), preferred_element_type=F32)


def _proj_kernel(x_ref, g0_ref, b0_ref, ws_ref, wt_ref,
                 glu_ref, q_ref, qi_ref, wi_ref, kt_ref, vt_ref, ktb_ref, vtb_ref, kit_ref, kitb_ref):
    h = _layer_norm(x_ref[...], g0_ref[...], b0_ref[...])
    hb = h.astype(BF16)

    def seg(col, width=512):
        return _dot(hb, ws_ref[:, col:col + width])

    glu_ref[...] = seg(0) * _sigmoid(seg(512))
    q = seg(1024) * ATT_SCALE
    qi = seg(1536)
    for hd in range(N_HEADS):
        sl = slice(hd * HEAD_DIM, (hd + 1) * HEAD_DIM)
        q_ref[hd] = q[:, sl].astype(BF16)
        qi_ref[hd] = qi[:, sl].astype(BF16)
    wi_ref[...] = seg(2048, LANES)[:, :N_IDX_HEADS]

    tm = hb.shape[0]
    kt = _dot_nt(wt_ref[0:ATT_W, :], hb).reshape(N_HEADS, HEAD_DIM, tm)
    kt_ref[...] = kt
    ktb_ref[...] = kt.astype(BF16)
    vt = _dot_nt(wt_ref[ATT_W:2 * ATT_W, :], hb).reshape(N_HEADS, HEAD_DIM, tm)
    vt_ref[...] = vt
    vtb_ref[...] = vt.astype(BF16)
    kit = _dot_nt(wt_ref[2 * ATT_W:2 * ATT_W + IDX_DIM, :], hb)
    kit_ref[...] = kit
    kitb_ref[...] = kit.astype(BF16)


def _proj(x, g0, b0, ws, wt, b, t, tm):
    n = b * t
    nt = t // tm
    row = lambda i: (i, 0)
    head = lambda i: (0, i, 0)
    full = lambda i: (0, 0)
    hm = jax.ShapeDtypeStruct((N_HEADS, n, HEAD_DIM), BF16)
    hm_spec = pl.BlockSpec((N_HEADS, tm, HEAD_DIM), head)
    kv_spec = pl.BlockSpec((None, N_HEADS, HEAD_DIM, tm), lambda i: (i // nt, 0, 0, i % nt))
    ki_spec = pl.BlockSpec((None, IDX_DIM, tm), lambda i: (i // nt, 0, i % nt))
    kv_shape = lambda dt: jax.ShapeDtypeStruct((b, N_HEADS, HEAD_DIM, t), dt)
    ki_shape = lambda dt: jax.ShapeDtypeStruct((b, IDX_DIM, t), dt)
    return pl.pallas_call(
        _proj_kernel,
        grid=(n // tm,),
        in_specs=[pl.BlockSpec((tm, D_MODEL), row),
                  pl.BlockSpec((1, D_MODEL), full), pl.BlockSpec((1, D_MODEL), full),
                  pl.BlockSpec(ws.shape, full), pl.BlockSpec(wt.shape, full)],
        out_specs=[pl.BlockSpec((tm, CONV_CH), row), hm_spec, hm_spec,
                   pl.BlockSpec((tm, N_IDX_HEADS), row),
                   kv_spec, kv_spec, kv_spec, kv_spec, ki_spec, ki_spec],
        out_shape=[jax.ShapeDtypeStruct((n, CONV_CH), F32), hm, hm,
                   jax.ShapeDtypeStruct((n, N_IDX_HEADS), F32),
                   kv_shape(F32), kv_shape(F32), kv_shape(BF16), kv_shape(BF16), ki_shape(F32), ki_shape(BF16)],
        compiler_params=_cparams(("parallel",)),
        name="proj",
    )(x, g0, b0, ws, wt)


def _conv_kernel(*refs, tc, nt, has_state):
    it = iter(refs)
    glu_ref = next(it)
    prev_ref = next(it) if nt > 1 else None
    state_ref = next(it) if has_state else None
    w_ref, cb_ref, g_ref, b_ref, c_ref, newconv_ref, buf_ref, acc_ref = it
    t = pl.program_id(1)

    @pl.when(t == 0)
    def _():
        if has_state:
            buf_ref[2:PREFIX_ROWS, :] = state_ref[...]
        else:
            buf_ref[0:PREFIX_ROWS, :] = jnp.zeros((PREFIX_ROWS, CONV_CH), F32)

    if nt > 1:
        @pl.when(t > 0)
        def _():
            buf_ref[0:PREFIX_ROWS, :] = prev_ref[...]

    buf_ref[PREFIX_ROWS:PREFIX_ROWS + tc, :] = glu_ref[...]

    rb = min(tc, 64)
    for c0 in range(0, CONV_CH, LANES):
        cs = slice(c0, c0 + LANES)
        taps = [w_ref[j:j + 1, cs] for j in range(CONV_WIDTH)]
        bias = cb_ref[:, cs]
        for r0 in range(0, tc, rb):
            acc = jnp.zeros((rb, LANES), F32) + bias
            for j in range(CONV_WIDTH):
                acc = acc + taps[j] * buf_ref[r0 + 2 + j:r0 + 2 + j + rb, cs]
            acc_ref[r0:r0 + rb, cs] = acc

    y = _layer_norm(acc_ref[...], g_ref[...], b_ref[...])
    c_ref[...] = (y * _sigmoid(y)).astype(BF16)

    @pl.when(t == nt - 1)
    def _():
        newconv_ref[...] = buf_ref[tc + 2:tc + PREFIX_ROWS, :]


def _conv(glu, state, conv_w, conv_b, lnc_g, lnc_b, tc):
    b, t, _ = glu.shape
    nt = t // tc
    blocks_per_tile = tc // PREFIX_ROWS
    in_specs = [pl.BlockSpec((None, tc, CONV_CH), lambda i, j: (i, j, 0))]
    args = [glu]
    if nt > 1:
        in_specs.append(pl.BlockSpec((None, PREFIX_ROWS, CONV_CH),
                                     lambda i, j: (i, jnp.maximum(j * blocks_per_tile - 1, 0), 0)))
        args.append(glu)
    if state is not None:
        in_specs.append(pl.BlockSpec((None, CONV_HIST, CONV_CH), lambda i, j: (i, 0, 0)))
        args.append(state)
    full = lambda i, j: (0, 0)
    in_specs += [pl.BlockSpec((CONV_WIDTH, CONV_CH), full), pl.BlockSpec((1, CONV_CH), full),
                 pl.BlockSpec((1, CONV_CH), full), pl.BlockSpec((1, CONV_CH), full)]
    args += [conv_w, conv_b, lnc_g, lnc_b]
    return pl.pallas_call(
        functools.partial(_conv_kernel, tc=tc, nt=nt, has_state=state is not None),
        grid=(b, nt),
        in_specs=in_specs,
        out_specs=[pl.BlockSpec((None, tc, CONV_CH), lambda i, j: (i, j, 0)),
                   pl.BlockSpec((None, CONV_HIST, CONV_CH), lambda i, j: (i, 0, 0))],
        out_shape=[jax.ShapeDtypeStruct((b, t, CONV_CH), BF16),
                   jax.ShapeDtypeStruct((b, CONV_HIST, CONV_CH), F32)],
        scratch_shapes=[pltpu.VMEM((PREFIX_ROWS + tc, CONV_CH), F32), pltpu.VMEM((tc, CONV_CH), F32)],
        compiler_params=_cparams(("parallel", "arbitrary")),
        name="conv",
    )(*args)


def _ordered_to_f32(c):
    return pltpu.bitcast(c ^ ((c >> 31) & jnp.int32(0x7FFFFFFF)), F32)


def _kth_largest(count_ge, rows, topk):
    def body(i, tau):
        bit = lax.shift_left(jnp.int32(1), jnp.int32(31) - i)
        cand = tau ^ bit
        return jnp.where(count_ge(_ordered_to_f32(cand)) >= topk, cand, tau)
    return _ordered_to_f32(lax.fori_loop(0, 32, body, jnp.full((rows, 1), INT_MIN, jnp.int32)))


def _count(mask):
    return jnp.sum(mask.astype(jnp.int32), axis=1, keepdims=True)


def _tie_rank(eq_chunks, rows):
    r = lax.broadcasted_iota(jnp.int32, (LANES, LANES), 0)
    c = lax.broadcasted_iota(jnp.int32, (LANES, LANES), 1)
    upper = jnp.where(r < c, 1.0, 0.0).astype(BF16)
    prefix = jnp.zeros((rows, 1), F32)
    for idx, eq in eq_chunks:
        eqf = jnp.where(eq, 1.0, 0.0)
        yield idx, _dot(eqf.astype(BF16), upper) + prefix
        prefix = prefix + jnp.sum(eqf, axis=1, keepdims=True)


def _attn_prompt_kernel(qi_ref, wi_ref, q_ref, kitb_ref, ktb_ref, vtb_ref, o_ref, sc_ref, bias_ref,
                        *, tq, t_eff, topk, first_block):
    qb = first_block + pl.program_id(1)
    wi = wi_ref[...]
    kitb = kitb_ref[...]
    acc = jnp.zeros((tq, t_eff), F32)
    for hd in range(N_IDX_HEADS):
        acc = acc + jnp.maximum(_dot(qi_ref[hd], kitb), 0.0) * wi[:, hd:hd + 1]
    q_pos = qb * tq + lax.broadcasted_iota(jnp.int32, (tq, 1), 0)
    k_pos = lax.broadcasted_iota(jnp.int32, (1, t_eff), 1)
    adm = k_pos <= q_pos
    sc_ref[...] = jnp.where(adm, acc * IDX_SCALE, NEG_INF)

    tau = _kth_largest(lambda cand: _count(sc_ref[...] >= cand), tq, topk)
    take_all = (q_pos + 1) <= topk
    sc = sc_ref[...]
    eq = (sc == tau) & adm
    need = topk - _count(sc > tau)
    excess = jnp.max(jnp.where(take_all, 0, _count(eq) - need)) > 0

    @pl.when(jnp.logical_not(excess))
    def _():
        bias_ref[...] = jnp.where(adm & ((sc >= tau) | take_all), 0.0, NEG_INF)

    @pl.when(excess)
    def _():
        needf = need.astype(F32)
        chunks = [(i, (sc_ref[:, i * LANES:(i + 1) * LANES] == tau) & adm[:, i * LANES:(i + 1) * LANES])
                  for i in range(t_eff // LANES)]
        for i, rank in _tie_rank(chunks, tq):
            cs = slice(i * LANES, (i + 1) * LANES)
            kc = sc_ref[:, cs]
            sel = ((kc > tau) | ((kc == tau) & (rank < needf)) | take_all) & adm[:, cs]
            bias_ref[:, cs] = jnp.where(sel, 0.0, NEG_INF)

    bias = bias_ref[...]
    for hd in range(N_HEADS):
        logit = _dot(q_ref[hd], ktb_ref[hd]) + bias
        m = jnp.max(logit, axis=1, keepdims=True)
        p = jnp.exp(logit - m)
        denom = jnp.sum(p, axis=1, keepdims=True)
        o = _dot_nt(p.astype(BF16), vtb_ref[hd]) / denom
        o_ref[:, hd * HEAD_DIM:(hd + 1) * HEAD_DIM] = o.astype(BF16)


def _attn_prompt(qi, wi, q, kitb, ktb, vtb, b, t, tq, first_block, n_blocks, t_eff):
    nq = t // tq
    topk = min(IDX_TOPK_MAX, t // 4)
    assert (first_block + n_blocks) * tq <= t_eff
    qrow = lambda i, j: i * nq + first_block + j
    qspec = pl.BlockSpec((N_HEADS, tq, HEAD_DIM), lambda i, j: (0, qrow(i, j), 0))
    kspec = pl.BlockSpec((None, N_HEADS, HEAD_DIM, t_eff), lambda i, j: (i, 0, 0, 0))
    return pl.pallas_call(
        functools.partial(_attn_prompt_kernel, tq=tq, t_eff=t_eff, topk=topk, first_block=first_block),
        grid=(b, n_blocks),
        in_specs=[qspec, pl.BlockSpec((tq, N_IDX_HEADS), lambda i, j: (qrow(i, j), 0)), qspec,
                  pl.BlockSpec((None, IDX_DIM, t_eff), lambda i, j: (i, 0, 0)), kspec, kspec],
        out_specs=pl.BlockSpec((None, tq, ATT_W), lambda i, j: (i, j, 0)),
        out_shape=jax.ShapeDtypeStruct((b, n_blocks * tq, ATT_W), BF16),
        scratch_shapes=[pltpu.VMEM((tq, t_eff), F32), pltpu.VMEM((tq, t_eff), F32)],
        compiler_params=_cparams(("parallel", "arbitrary")),
        name="attn_prompt",
    )(qi, wi, q, kitb, ktb, vtb)


def _indexer_rows(qi, keys_t, wi_col):
    s = jnp.maximum(_dot(qi, keys_t), 0.0) * wi_col
    n = s.shape[1]
    return jnp.sum(s.reshape(N_IDX_HEADS, SUBLANES, n), axis=0) * IDX_SCALE


def _sample_scores_kernel(pt_ref, qi_ref, wi_ref, *refs, pages_per_step, page):
    del pt_ref
    page_refs, out_ref = refs[:pages_per_step], refs[pages_per_step]
    qi = qi_ref[...]
    wi_col = wi_ref[...]
    for i in range(pages_per_step):
        out_ref[:, i * page:(i + 1) * page] = _indexer_rows(qi, page_refs[i][...].astype(BF16), wi_col)


def _sample_scores(page_table, qi_s, wi_col, cache_kidx_t, layer, pages_per_step):
    bsz = qi_s.shape[0]
    n_pages = page_table.shape[0] // bsz
    page = cache_kidx_t.shape[-1]
    steps = n_pages // pages_per_step

    def page_spec(i):
        return pl.BlockSpec((None, None, IDX_DIM, page),
                            lambda b, g, pt: (layer, pt[b * n_pages + g * pages_per_step + i], 0, 0))

    grid_spec = pltpu.PrefetchScalarGridSpec(
        num_scalar_prefetch=1,
        grid=(bsz, steps),
        in_specs=[pl.BlockSpec((None, N_IDX_HEADS * SUBLANES, IDX_DIM), lambda b, g, pt: (b, 0, 0)),
                  pl.BlockSpec((None, N_IDX_HEADS * SUBLANES, 1), lambda b, g, pt: (b, 0, 0))]
                 + [page_spec(i) for i in range(pages_per_step)],
        out_specs=pl.BlockSpec((None, SUBLANES, pages_per_step * page), lambda b, g, pt: (b, 0, g)),
    )
    return pl.pallas_call(
        functools.partial(_sample_scores_kernel, pages_per_step=pages_per_step, page=page),
        grid_spec=grid_spec,
        out_shape=jax.ShapeDtypeStruct((bsz, SUBLANES, n_pages * page), F32),
        compiler_params=_cparams(("parallel", "arbitrary")),
        name="sample_scores",
    )(page_table, qi_s, wi_col, *([cache_kidx_t] * pages_per_step))


def _sample_select_kernel(sp_ref, qi_ref, wi_ref, kin_ref, selp_ref, seln_ref, *, t_new, past, topk):
    sn = _indexer_rows(qi_ref[...], kin_ref[...], wi_ref[...])
    jn = lax.broadcasted_iota(jnp.int32, (SUBLANES, LANES), 1)
    tn = lax.broadcasted_iota(jnp.int32, (SUBLANES, LANES), 0)
    adm_n = (jn <= tn) & (jn < t_new)
    sn = jnp.where(adm_n, sn, NEG_INF)

    tau = _kth_largest(lambda cand: _count(sp_ref[...] >= cand) + _count(sn >= cand), SUBLANES, topk)
    sp = sp_ref[...]
    need = topk - (_count(sp > tau) + _count(sn > tau))
    eqn = (sn == tau) & adm_n
    excess = jnp.max(_count(sp == tau) + _count(eqn) - need) > 0

    @pl.when(jnp.logical_not(excess))
    def _():
        selp_ref[...] = jnp.where(sp >= tau, 1.0, 0.0)
        seln_ref[...] = jnp.where((sn >= tau) & adm_n, 1.0, 0.0)

    @pl.when(excess)
    def _():
        needf = need.astype(F32)
        n_chunks = past // LANES
        chunks = [(i, sp_ref[:, i * LANES:(i + 1) * LANES] == tau) for i in range(n_chunks)]
        chunks.append((n_chunks, eqn))
        for i, rank in _tie_rank(chunks, SUBLANES):
            if i < n_chunks:
                kc = sp_ref[:, i * LANES:(i + 1) * LANES]
                sel = (kc > tau) | ((kc == tau) & (rank < needf))
                selp_ref[:, i * LANES:(i + 1) * LANES] = jnp.where(sel, 1.0, 0.0)
            else:
                sel = ((sn > tau) | ((sn == tau) & (rank < needf))) & adm_n
                seln_ref[...] = jnp.where(sel, 1.0, 0.0)


def _sample_select(scores_past, qi_s, wi_col, kitb_new_pad, t_new):
    bsz, _, past = scores_past.shape
    assert past >= IDX_TOPK_MAX, "every query must see at least top-k admissible keys"
    topk = min(IDX_TOPK_MAX, (past + t_new) // 4)
    per_b = lambda shape: pl.BlockSpec((None,) + shape, lambda b: (b, 0, 0))
    return pl.pallas_call(
        functools.partial(_sample_select_kernel, t_new=t_new, past=past, topk=topk),
        grid=(bsz,),
        in_specs=[per_b((SUBLANES, past)), per_b((N_IDX_HEADS * SUBLANES, IDX_DIM)),
                  per_b((N_IDX_HEADS * SUBLANES, 1)), per_b((IDX_DIM, LANES))],
        out_specs=[per_b((SUBLANES, past)), per_b((SUBLANES, LANES))],
        out_shape=[jax.ShapeDtypeStruct((bsz, SUBLANES, past), F32),
                   jax.ShapeDtypeStruct((bsz, SUBLANES, LANES), F32)],
        compiler_params=_cparams(("parallel",)),
        name="sample_select",
    )(scores_past, qi_s, wi_col, kitb_new_pad)


def _sample_attend_kernel(pt_ref, q_ref, selp_ref, seln_ref, *refs, n_groups, pages_per_step, page):
    del pt_ref
    g = pages_per_step
    k_refs, v_refs = refs[:g], refs[g:2 * g]
    kn_ref, vn_ref, o_ref, m_ref, l_ref, acc_ref = refs[2 * g:]
    step = pl.program_id(1)
    rows = N_HEADS * SUBLANES

    @pl.when(step == 0)
    def _():
        m_ref[...] = jnp.full((rows, 1), NEG_INF, F32)
        l_ref[...] = jnp.zeros((rows, 1), F32)
        acc_ref[...] = jnp.zeros((rows, ATT_W), F32)

    qrow = lax.broadcasted_iota(jnp.int32, (rows, ATT_W), 0)
    qcol = lax.broadcasted_iota(jnp.int32, (rows, ATT_W), 1)
    q_bd = jnp.where((qcol >> 6) == (qrow >> 3), jnp.concatenate([q_ref[...]] * N_HEADS, axis=1), 0.0).astype(BF16)

    def process(k_pages, v_pages, sel):
        logit = jnp.concatenate([_dot(q_bd, kp.reshape(ATT_W, page).astype(BF16)) for kp in k_pages], axis=1)
        valid = jnp.concatenate([sel] * N_HEADS, axis=0) > 0.5
        logit = jnp.where(valid, logit, NEG_INF)
        m_old = m_ref[...]
        m_new = jnp.maximum(m_old, jnp.max(logit, axis=1, keepdims=True))
        m_safe = jnp.where(m_new == NEG_INF, 0.0, m_new)
        pr = jnp.exp(logit - m_safe)
        alpha = jnp.exp(m_old - m_safe)
        l_ref[...] = alpha * l_ref[...] + jnp.sum(pr, axis=1, keepdims=True)
        prb = pr.astype(BF16)
        pv = jnp.zeros((rows, ATT_W), F32)
        for i, vp in enumerate(v_pages):
            pv = pv + _dot_nt(prb[:, i * page:(i + 1) * page], vp.reshape(ATT_W, page).astype(BF16))
        acc_ref[...] = alpha * acc_ref[...] + pv
        m_ref[...] = m_new

    @pl.when(step < n_groups)
    def _():
        process([r[...] for r in k_refs], [r[...] for r in v_refs], selp_ref[...])

    @pl.when(step == n_groups)
    def _():
        process([kn_ref[...]], [vn_ref[...]], seln_ref[...])
        out = acc_ref[...] / l_ref[...]
        for hd in range(N_HEADS):
            o_ref[:, hd * HEAD_DIM:(hd + 1) * HEAD_DIM] = out[hd * SUBLANES:(hd + 1) * SUBLANES,
                                                              hd * HEAD_DIM:(hd + 1) * HEAD_DIM]


def _sample_attend(page_table, q_s, selp, seln, cache_kt, cache_vt, kt_new_pad, vt_new_pad, layer, pages_per_step):
    bsz = q_s.shape[0]
    n_pages = page_table.shape[0] // bsz
    page = cache_kt.shape[-1]
    g = pages_per_step
    n_groups = n_pages // g
    last = n_groups - 1

    def page_spec(i):
        return pl.BlockSpec(
            (None, None, N_HEADS, HEAD_DIM, page),
            lambda b, s, pt: (layer, pt[b * n_pages + jnp.minimum(s, last) * g + i], 0, 0, 0))

    new_spec = pl.BlockSpec((None, N_HEADS, HEAD_DIM, page), lambda b, s, pt: (b, 0, 0, 0))
    rows = N_HEADS * SUBLANES
    grid_spec = pltpu.PrefetchScalarGridSpec(
        num_scalar_prefetch=1,
        grid=(bsz, n_groups + 1),
        in_specs=[pl.BlockSpec((None, rows, HEAD_DIM), lambda b, s, pt: (b, 0, 0)),
                  pl.BlockSpec((None, SUBLANES, g * page), lambda b, s, pt: (b, 0, jnp.minimum(s, last))),
                  pl.BlockSpec((None, SUBLANES, LANES), lambda b, s, pt: (b, 0, 0))]
                 + [page_spec(i) for i in range(g)] + [page_spec(i) for i in range(g)]
                 + [new_spec, new_spec],
        out_specs=pl.BlockSpec((None, SUBLANES, ATT_W), lambda b, s, pt: (b, 0, 0)),
        scratch_shapes=[pltpu.VMEM((rows, 1), F32), pltpu.VMEM((rows, 1), F32), pltpu.VMEM((rows, ATT_W), F32)],
    )
    return pl.pallas_call(
        functools.partial(_sample_attend_kernel, n_groups=n_groups, pages_per_step=g, page=page),
        grid_spec=grid_spec,
        out_shape=jax.ShapeDtypeStruct((bsz, SUBLANES, ATT_W), F32),
        compiler_params=_cparams(("parallel", "arbitrary")),
        name="sample_attend",
    )(page_table, q_s, selp, seln, *([cache_kt] * g), *([cache_vt] * g), kt_new_pad, vt_new_pad)


def _mix_kernel(x_ref, c_ref, o_ref, g0_ref, b0_ref, wg_ref, bg_ref, wco_ref, wao_ref, wo_ref,
                g1_ref, b1_ref, wrh_ref, wrl_ref, br_ref, x1_ref, comb_ref, *, alpha):
    h = _layer_norm(x_ref[...], g0_ref[...], b0_ref[...])
    hb = h.astype(BF16)
    gc = _sigmoid(_dot(hb, wg_ref[:, :D_MODEL]) + bg_ref[:, :D_MODEL])
    ga = _sigmoid(_dot(hb, wg_ref[:, D_MODEL:]) + bg_ref[:, D_MODEL:])
    merged = gc * _dot(c_ref[...], wco_ref[...]) + ga * _dot(o_ref[...], wao_ref[...])
    mix = _dot(merged.astype(BF16), wo_ref[...])
    x1 = _layer_norm(alpha * h + mix, g1_ref[...], b1_ref[...])
    x1_ref[...] = x1

    xh = x1.astype(BF16)
    xl = (x1 - xh.astype(F32)).astype(BF16)
    logits = _dot(xh, wrh_ref[...]) + (_dot(xl, wrh_ref[...]) + _dot(xh, wrl_ref[...])) + br_ref[...]
    lane = lax.broadcasted_iota(jnp.int32, logits.shape, 1)
    is_grp = (lane >= N_EXPERTS) & (lane < N_EXPERTS + N_GROUPS)
    gl = jnp.where(is_grp, logits, NEG_INF)
    gmax = jnp.max(gl, axis=1, keepdims=True)
    grp = jnp.min(jnp.where(gl == gmax, lane, 4 * LANES), axis=1, keepdims=True) - N_EXPERTS
    p_grp = 1.0 / jnp.sum(jnp.exp(gl - gmax), axis=1, keepdims=True)
    in_grp = (lane < N_EXPERTS) & ((lane >> 3) == grp)
    el = jnp.where(in_grp, logits, NEG_INF)
    emax = jnp.max(el, axis=1, keepdims=True)
    ee = jnp.exp(el - emax)
    pe = ee / jnp.sum(ee, axis=1, keepdims=True)
    pe = jnp.where(in_grp, pe, -1.0)
    p1 = jnp.max(pe, axis=1, keepdims=True)
    i1 = jnp.min(jnp.where(pe == p1, lane, 4 * LANES), axis=1, keepdims=True)
    pe2 = jnp.where(lane == i1, -1.0, pe)
    p2 = jnp.max(pe2, axis=1, keepdims=True)
    i2 = jnp.min(jnp.where(pe2 == p2, lane, 4 * LANES), axis=1, keepdims=True)
    tot = p1 + p2
    comb_ref[...] = (jnp.where(lane == i1, p_grp * (p1 / tot), 0.0)
                     + jnp.where(lane == i2, p_grp * (p2 / tot), 0.0))


def _mix(x, c, o, g0, b0, wg, bg, wco, wao, wo, g1, b1, wrh, wrl, br, alpha, tm):
    n = x.shape[0]
    row = lambda i: (i, 0)
    full = lambda i: (0, 0)
    fs = lambda a: pl.BlockSpec(a.shape, full)
    return pl.pallas_call(
        functools.partial(_mix_kernel, alpha=alpha),
        grid=(n // tm,),
        in_specs=[pl.BlockSpec((tm, D_MODEL), row), pl.BlockSpec((tm, CONV_CH), row),
                  pl.BlockSpec((tm, ATT_W), row)] + [fs(a) for a in (g0, b0, wg, bg, wco, wao, wo, g1, b1, wrh, wrl, br)],
        out_specs=[pl.BlockSpec((tm, D_MODEL), row), pl.BlockSpec((tm, LANES), row)],
        out_shape=[jax.ShapeDtypeStruct((n, D_MODEL), F32), jax.ShapeDtypeStruct((n, LANES), F32)],
        compiler_params=_cparams(("parallel",)),
        name="mix",
    )(x, c, o, g0, b0, wg, bg, wco, wao, wo, g1, b1, wrh, wrl, br)


def _ffn_kernel(x1_ref, comb_ref, p_ref, wpg_ref, wpp_ref, weg_ref, weu_ref, wed_ref, g2_ref, b2_ref,
                y_ref, acc_ref, xb_ref, *, alpha):
    e = pl.program_id(1)

    @pl.when(e == 0)
    def _():
        x1 = x1_ref[...]
        xb = x1.astype(BF16)
        xb_ref[...] = xb
        ple = _sigmoid(_dot(xb, wpg_ref[...])) * _dot(p_ref[...].astype(BF16), wpp_ref[...])
        acc_ref[...] = alpha * x1 + ple

    xb = xb_ref[...]
    comb = comb_ref[...]
    lane = lax.broadcasted_iota(jnp.int32, comb.shape, 1)
    ce = jnp.sum(jnp.where(lane == e, comb, 0.0), axis=1, keepdims=True)
    hg = _dot(xb, weg_ref[...])
    hu = _dot(xb, weu_ref[...])
    hidden = (hg * _sigmoid(hg)) * hu * ce
    acc_ref[...] += _dot(hidden.astype(BF16), wed_ref[...])

    @pl.when(e == N_EXPERTS - 1)
    def _():
        y_ref[...] = _layer_norm(acc_ref[...], g2_ref[...], b2_ref[...])


def _ffn(x1, comb, p, wpg, wpp, weg, weu, wed, g2, b2, alpha, tm):
    n = x1.shape[0]
    row = lambda i, e: (i, 0)
    full = lambda i, e: (0, 0)
    exp = lambda i, e: (e, 0, 0)
    return pl.pallas_call(
        functools.partial(_ffn_kernel, alpha=alpha),
        grid=(n // tm, N_EXPERTS),
        in_specs=[pl.BlockSpec((tm, D_MODEL), row), pl.BlockSpec((tm, LANES), row),
                  pl.BlockSpec((tm, PLE_DIM), row),
                  pl.BlockSpec((D_MODEL, D_MODEL), full), pl.BlockSpec((PLE_DIM, D_MODEL), full),
                  pl.BlockSpec((None, D_MODEL, D_EXPERT), exp), pl.BlockSpec((None, D_MODEL, D_EXPERT), exp),
                  pl.BlockSpec((None, D_EXPERT, D_MODEL), exp),
                  pl.BlockSpec((1, D_MODEL), full), pl.BlockSpec((1, D_MODEL), full)],
        out_specs=pl.BlockSpec((tm, D_MODEL), row),
        out_shape=jax.ShapeDtypeStruct((n, D_MODEL), F32),
        scratch_shapes=[pltpu.VMEM((tm, D_MODEL), F32), pltpu.VMEM((tm, D_MODEL), BF16)],
        compiler_params=_cparams(("parallel", "arbitrary")),
        name="ffn",
    )(x1, comb, p, wpg, wpp, weg, weu, wed, g2, b2)


def _tile(n, target):
    t = min(n, target)
    assert n % t == 0, (n, t)
    return t


def _row(v):
    return v.reshape(1, -1).astype(F32)


def kernel(x_prompt, x_sample, cache_k, cache_v, cache_kidx, state_conv, page_table, p_prompt, p_sample,
           ln0_g, ln0_b, w_in, b_gate, conv_w, conv_b, lnc_g, lnc_b, w_conv_out, w_attn_out, w_o,
           ln1_g, ln1_b, w_rg, b_rg, w_re, b_re, w_eg, w_eu, w_ed, w_pg, w_pp, ln2_g, ln2_b):
    depth = w_in.shape[0]
    assert depth == 1, "single-layer step"
    assert w_in.shape[1:] == (D_MODEL, COL_END)
    layer = 0
    alpha = (2.0 * depth) ** 0.25
    bp, tp, _ = x_prompt.shape
    bs, ts, _ = x_sample.shape
    assert ts == SUBLANES, "sample step length must fill one sublane tile"
    page = cache_k.shape[2]
    n_pages = page_table.shape[1]
    assert page == LANES

    g0, b0 = _row(ln0_g), _row(ln0_b)
    w = w_in[layer]
    wi_cols = jnp.pad(w[:, COL_WI:COL_GC], ((0, 0), (0, LANES - N_IDX_HEADS)))
    ws = jnp.concatenate([w[:, COL_GLU_A:COL_K], w[:, COL_QI:COL_KI], wi_cols], axis=1).astype(BF16)
    wt = w[:, COL_K:COL_QI].T
    wt = jnp.concatenate([wt, w[:, COL_KI:COL_WI].T], axis=0).astype(BF16)
    wg = w[:, COL_GC:].astype(BF16)
    bg = _row(b_gate[layer])
    wr = jnp.pad(jnp.concatenate([w_re[layer], w_rg[layer]], axis=1), ((0, 0), (0, LANES - N_EXPERTS - N_GROUPS)))
    wrh = wr.astype(BF16)
    wrl = (wr - wrh.astype(F32)).astype(BF16)
    br = jnp.pad(jnp.concatenate([b_re[layer], b_rg[layer]]), (0, LANES - N_EXPERTS - N_GROUPS)).reshape(1, LANES)
    wco, wao, wo = w_conv_out[layer].astype(BF16), w_attn_out[layer].astype(BF16), w_o[layer].astype(BF16)
    wpg, wpp = w_pg[layer].astype(BF16), w_pp[layer].astype(BF16)
    weg, weu, wed = w_eg[layer].astype(BF16), w_eu[layer].astype(BF16), w_ed[layer].astype(BF16)
    cw, cb = conv_w[layer], _row(conv_b[layer])
    lcg, lcb = _row(lnc_g[layer]), _row(lnc_b[layer])
    g1, b1, g2, b2 = _row(ln1_g[layer]), _row(ln1_b[layer]), _row(ln2_g[layer]), _row(ln2_b[layer])
    cache_kt = jnp.transpose(cache_k, (0, 1, 3, 4, 2))
    cache_vt = jnp.transpose(cache_v, (0, 1, 3, 4, 2))
    cache_kidx_t = jnp.transpose(cache_kidx, (0, 1, 3, 2))

    def tail(x1, comb, p):
        n = x1.shape[0]
        return _ffn(x1, comb, p.reshape(n, PLE_DIM), wpg, wpp, weg, weu, wed, g2, b2, alpha, _tile(n, 1024))

    def mix(x, c, o):
        n = x.shape[0]
        return _mix(x, c, o, g0, b0, wg, bg, wco, wao, wo, g1, b1, wrh, wrl, br, alpha, _tile(n, 256))

    np_ = bp * tp
    xp = x_prompt.reshape(np_, D_MODEL)
    glu, q, qi, wi, kt, vt, ktb, vtb, kit, kitb = _proj(xp, g0, b0, ws, wt, bp, tp, _tile(tp, 512))
    c, newconv_p = _conv(glu.reshape(bp, tp, CONV_CH), None, cw, cb, lcg, lcb, _tile(tp, 256))
    tq = _tile(tp, 256)
    bucket = 2 * tq if tp % (2 * tq) == 0 else tp
    per_bucket = bucket // tq
    o = jnp.concatenate(
        [_attn_prompt(qi, wi, q, kitb, ktb, vtb, bp, tp, tq, i * per_bucket, per_bucket, (i + 1) * bucket)
         for i in range(tp // bucket)], axis=1).reshape(np_, ATT_W)
    x1, comb = mix(xp, c.reshape(np_, CONV_CH), o)
    y_prompt = tail(x1, comb, p_prompt[layer]).reshape(bp, tp, D_MODEL)
    new_k_p = jnp.transpose(kt, (0, 3, 1, 2))[None]
    new_v_p = jnp.transpose(vt, (0, 3, 1, 2))[None]
    new_ki_p = jnp.transpose(kit, (0, 2, 1))[None]

    ns = bs * ts
    xs = x_sample.reshape(ns, D_MODEL)
    glu, q, qi, wi, kt, vt, _, _, kit, kitb = _proj(xs, g0, b0, ws, wt, 1, ns, ns)
    c, newconv_s = _conv(glu.reshape(bs, ts, CONV_CH), state_conv[layer], cw, cb, lcg, lcb, ts)
    to_rows = lambda a: a.reshape(N_HEADS, bs, ts, HEAD_DIM).transpose(1, 0, 2, 3).reshape(bs, N_HEADS * ts, HEAD_DIM)
    qi_s, q_s = to_rows(qi), to_rows(q)
    wi_col = wi.reshape(bs, ts, N_IDX_HEADS).transpose(0, 2, 1).reshape(bs, N_IDX_HEADS * ts, 1)
    kt_s = kt.reshape(N_HEADS, HEAD_DIM, bs, ts)
    vt_s = vt.reshape(N_HEADS, HEAD_DIM, bs, ts)
    kit_s = kit.reshape(IDX_DIM, bs, ts)
    pad_keys = lambda a: jnp.pad(a, ((0, 0),) * (a.ndim - 1) + ((0, page - ts),))
    pt_flat = page_table.reshape(-1).astype(jnp.int32)
    group = 8 if n_pages % 8 == 0 else 1
    scores_past = _sample_scores(pt_flat, qi_s, wi_col, cache_kidx_t, layer, group)
    selp, seln = _sample_select(scores_past, qi_s, wi_col,
                                pad_keys(kitb.reshape(IDX_DIM, bs, ts).transpose(1, 0, 2)), ts)
    o = _sample_attend(pt_flat, q_s, selp, seln, cache_kt, cache_vt,
                       pad_keys(kt_s.transpose(2, 0, 1, 3)), pad_keys(vt_s.transpose(2, 0, 1, 3)), layer, group)
    x1, comb = mix(xs, c.reshape(ns, CONV_CH), o.reshape(ns, ATT_W).astype(BF16))
    y_sample = tail(x1, comb, p_sample[layer]).reshape(bs, ts, D_MODEL)
    new_k_s = kt_s.transpose(2, 3, 0, 1)[None]
    new_v_s = vt_s.transpose(2, 3, 0, 1)[None]
    new_ki_s = kit_s.transpose(1, 2, 0)[None]

    return (y_prompt, y_sample, new_k_p, new_v_p, new_ki_p, newconv_p[None],
            new_k_s, new_v_s, new_ki_s, newconv_s[None])
```

```python
import functools

import jax
import jax.numpy as jnp
from jax import lax
from jax.experimental import pallas as pl
from jax.experimental.pallas import tpu as pltpu

D_MODEL = 1024
CONV_CH = 512
CONV_WIDTH = 31
CONV_HIST = CONV_WIDTH - 1
N_HEADS = 8
HEAD_DIM = 64
ATT_W = N_HEADS * HEAD_DIM
N_IDX_HEADS = 8
IDX_DIM = 64
IDX_TOPK_MAX = 256
N_GROUPS = 4
EXPERTS_PER_GROUP = 8
N_EXPERTS = N_GROUPS * EXPERTS_PER_GROUP
D_EXPERT = 256
PLE_DIM = 256
LN_EPS = 1e-5
IDX_SCALE = (N_IDX_HEADS * IDX_DIM) ** -0.5
ATT_SCALE = HEAD_DIM ** -0.5
COL_GLU_A, COL_GLU_B, COL_Q, COL_K, COL_V, COL_QI, COL_KI, COL_WI, COL_GC = (
    0, 512, 1024, 1536, 2048, 2560, 3072, 3136, 3144)
COL_END = COL_GC + 2 * D_MODEL

LANES = 128
SUBLANES = 8
PREFIX_ROWS = 32
GROUP_LANE = LANES - 1
VMEM_LIMIT = 48 * 1024 * 1024
VMEM_LIMIT_FFN = 56 * 1024 * 1024

F32 = jnp.float32
BF16 = jnp.bfloat16
NEG_INF = float("-inf")
INT_MIN = -2 ** 31


def _cparams(sem):
    return pltpu.CompilerParams(dimension_semantics=sem, vmem_limit_bytes=VMEM_LIMIT)


def _layer_norm(x, g, b):
    mu = jnp.mean(x, axis=-1, keepdims=True)
    xc = x - mu
    var = jnp.mean(xc * xc, axis=-1, keepdims=True)
    return xc * lax.rsqrt(var + LN_EPS) * g + b


def _sigmoid(x):
    return 1.0 / (1.0 + jnp.exp(-x))


def _dot(a, b):
    return jnp.dot(a, b, preferred_element_type=F32)


def _dot_nt(a, b):
    return lax.dot_general(a, b, (((1,), (1,)), ((), ())), preferred_element_type=F32)


def _proj_kernel(x_ref, g0_ref, b0_ref, ws_ref, wt_ref,
                 glu_ref, q_ref, qi_ref, wi_ref, kt_ref, vt_ref, ktb_ref, vtb_ref, kit_ref, kitb_ref):
    h = _layer_norm(x_ref[...], g0_ref[...], b0_ref[...])
    hb = h.astype(BF16)

    def seg(col, width=512):
        return _dot(hb, ws_ref[:, col:col + width])

    glu_ref[...] = seg(0) * _sigmoid(seg(512))
    q = seg(1024) * ATT_SCALE
    qi = seg(1536)
    for hd in range(N_HEADS):
        sl = slice(hd * HEAD_DIM, (hd + 1) * HEAD_DIM)
        q_ref[hd] = q[:, sl].astype(BF16)
        qi_ref[hd] = qi[:, sl].astype(BF16)
    wi_ref[...] = seg(2048, LANES)[:, :N_IDX_HEADS]

    tm = hb.shape[0]
    kt = _dot_nt(wt_ref[0:ATT_W, :], hb).reshape(N_HEADS, HEAD_DIM, tm)
    kt_ref[...] = kt
    ktb_ref[...] = kt.astype(BF16)
    vt = _dot_nt(wt_ref[ATT_W:2 * ATT_W, :], hb).reshape(N_HEADS, HEAD_DIM, tm)
    vt_ref[...] = vt
    vtb_ref[...] = vt.astype(BF16)
    kit = _dot_nt(wt_ref[2 * ATT_W:2 * ATT_W + IDX_DIM, :], hb)
    kit_ref[...] = kit
    kitb_ref[...] = kit.astype(BF16)


def _proj(x, g0, b0, ws, wt, b, t, tm):
    n = b * t
    nt = t // tm
    row = lambda i: (i, 0)
    head = lambda i: (0, i, 0)
    full = lambda i: (0, 0)
    hm = jax.ShapeDtypeStruct((N_HEADS, n, HEAD_DIM), BF16)
    hm_spec = pl.BlockSpec((N_HEADS, tm, HEAD_DIM), head)
    kv_spec = pl.BlockSpec((None, N_HEADS, HEAD_DIM, tm), lambda i: (i // nt, 0, 0, i % nt))
    ki_spec = pl.BlockSpec((None, IDX_DIM, tm), lambda i: (i // nt, 0, i % nt))
    kv_shape = lambda dt: jax.ShapeDtypeStruct((b, N_HEADS, HEAD_DIM, t), dt)
    ki_shape = lambda dt: jax.ShapeDtypeStruct((b, IDX_DIM, t), dt)
    return pl.pallas_call(
        _proj_kernel,
        grid=(n // tm,),
        in_specs=[pl.BlockSpec((tm, D_MODEL), row),
                  pl.BlockSpec((1, D_MODEL), full), pl.BlockSpec((1, D_MODEL), full),
                  pl.BlockSpec(ws.shape, full), pl.BlockSpec(wt.shape, full)],
        out_specs=[pl.BlockSpec((tm, CONV_CH), row), hm_spec, hm_spec,
                   pl.BlockSpec((tm, N_IDX_HEADS), row),
                   kv_spec, kv_spec, kv_spec, kv_spec, ki_spec, ki_spec],
        out_shape=[jax.ShapeDtypeStruct((n, CONV_CH), F32), hm, hm,
                   jax.ShapeDtypeStruct((n, N_IDX_HEADS), F32),
                   kv_shape(F32), kv_shape(F32), kv_shape(BF16), kv_shape(BF16), ki_shape(F32), ki_shape(BF16)],
        compiler_params=_cparams(("parallel",)),
        name="proj",
    )(x, g0, b0, ws, wt)


def _conv_kernel(*refs, tc, nt, has_state):
    it = iter(refs)
    glu_ref = next(it)
    prev_ref = next(it) if nt > 1 else None
    state_ref = next(it) if has_state else None
    w_ref, cb_ref, g_ref, b_ref, c_ref, newconv_ref, buf_ref, acc_ref = it
    t = pl.program_id(1)

    @pl.when(t == 0)
    def _():
        if has_state:
            buf_ref[2:PREFIX_ROWS, :] = state_ref[...]
        else:
            buf_ref[0:PREFIX_ROWS, :] = jnp.zeros((PREFIX_ROWS, CONV_CH), F32)

    if nt > 1:
        @pl.when(t > 0)
        def _():
            buf_ref[0:PREFIX_ROWS, :] = prev_ref[...]

    buf_ref[PREFIX_ROWS:PREFIX_ROWS + tc, :] = glu_ref[...]

    rb = min(tc, 64)
    for c0 in range(0, CONV_CH, LANES):
        cs = slice(c0, c0 + LANES)
        taps = [w_ref[j:j + 1, cs] for j in range(CONV_WIDTH)]
        bias = cb_ref[:, cs]
        for r0 in range(0, tc, rb):
            acc = jnp.zeros((rb, LANES), F32) + bias
            for j in range(CONV_WIDTH):
                acc = acc + taps[j] * buf_ref[r0 + 2 + j:r0 + 2 + j + rb, cs]
            acc_ref[r0:r0 + rb, cs] = acc

    y = _layer_norm(acc_ref[...], g_ref[...], b_ref[...])
    c_ref[...] = (y * _sigmoid(y)).astype(BF16)

    @pl.when(t == nt - 1)
    def _():
        newconv_ref[...] = buf_ref[tc + 2:tc + PREFIX_ROWS, :]


def _conv(glu, state, conv_w, conv_b, lnc_g, lnc_b, tc):
    b, t, _ = glu.shape
    nt = t // tc
    blocks_per_tile = tc // PREFIX_ROWS
    in_specs = [pl.BlockSpec((None, tc, CONV_CH), lambda i, j: (i, j, 0))]
    args = [glu]
    if nt > 1:
        in_specs.append(pl.BlockSpec((None, PREFIX_ROWS, CONV_CH),
                                     lambda i, j: (i, jnp.maximum(j * blocks_per_tile - 1, 0), 0)))
        args.append(glu)
    if state is not None:
        in_specs.append(pl.BlockSpec((None, CONV_HIST, CONV_CH), lambda i, j: (i, 0, 0)))
        args.append(state)
    full = lambda i, j: (0, 0)
    in_specs += [pl.BlockSpec((CONV_WIDTH, CONV_CH), full), pl.BlockSpec((1, CONV_CH), full),
                 pl.BlockSpec((1, CONV_CH), full), pl.BlockSpec((1, CONV_CH), full)]
    args += [conv_w, conv_b, lnc_g, lnc_b]
    return pl.pallas_call(
        functools.partial(_conv_kernel, tc=tc, nt=nt, has_state=state is not None),
        grid=(b, nt),
        in_specs=in_specs,
        out_specs=[pl.BlockSpec((None, tc, CONV_CH), lambda i, j: (i, j, 0)),
                   pl.BlockSpec((None, CONV_HIST, CONV_CH), lambda i, j: (i, 0, 0))],
        out_shape=[jax.ShapeDtypeStruct((b, t, CONV_CH), BF16),
                   jax.ShapeDtypeStruct((b, CONV_HIST, CONV_CH), F32)],
        scratch_shapes=[pltpu.VMEM((PREFIX_ROWS + tc, CONV_CH), F32), pltpu.VMEM((tc, CONV_CH), F32)],
        compiler_params=_cparams(("parallel", "arbitrary")),
        name="conv",
    )(*args)


def _ordered_to_f32(c):
    return pltpu.bitcast(c ^ ((c >> 31) & jnp.int32(0x7FFFFFFF)), F32)


def _kth_largest(count_ge, rows, topk):
    def body(i, tau):
        bit = lax.shift_left(jnp.int32(1), jnp.int32(31) - i)
        cand = tau ^ bit
        return jnp.where(count_ge(_ordered_to_f32(cand)) >= topk, cand, tau)
    return _ordered_to_f32(lax.fori_loop(0, 32, body, jnp.full((rows, 1), INT_MIN, jnp.int32)))


def _count(mask):
    return jnp.sum(mask.astype(jnp.int32), axis=1, keepdims=True)


def _tie_rank(eq_chunks, rows):
    r = lax.broadcasted_iota(jnp.int32, (LANES, LANES), 0)
    c = lax.broadcasted_iota(jnp.int32, (LANES, LANES), 1)
    upper = jnp.where(r < c, 1.0, 0.0).astype(BF16)
    prefix = jnp.zeros((rows, 1), F32)
    for idx, eq in eq_chunks:
        eqf = jnp.where(eq, 1.0, 0.0)
        yield idx, _dot(eqf.astype(BF16), upper) + prefix
        prefix = prefix + jnp.sum(eqf, axis=1, keepdims=True)


def _attn_prompt_kernel(qi_ref, wi_ref, q_ref, kitb_ref, ktb_ref, vtb_ref, o_ref, sc_ref, bias_ref,
                        *, tq, t_eff, topk, first_block):
    qb = first_block + pl.program_id(1)
    wi = wi_ref[...]
    kitb = kitb_ref[...]
    acc = jnp.zeros((tq, t_eff), F32)
    for hd in range(N_IDX_HEADS):
        acc = acc + jnp.maximum(_dot(qi_ref[hd], kitb), 0.0) * wi[:, hd:hd + 1]
    q_pos = qb * tq + lax.broadcasted_iota(jnp.int32, (tq, 1), 0)
    k_pos = lax.broadcasted_iota(jnp.int32, (1, t_eff), 1)
    adm = k_pos <= q_pos
    sc_ref[...] = jnp.where(adm, acc * IDX_SCALE, NEG_INF)

    tau = _kth_largest(lambda cand: _count(sc_ref[...] >= cand), tq, topk)
    take_all = (q_pos + 1) <= topk
    sc = sc_ref[...]
    eq = (sc == tau) & adm
    need = topk - _count(sc > tau)
    excess = jnp.max(jnp.where(take_all, 0, _count(eq) - need)) > 0

    @pl.when(jnp.logical_not(excess))
    def _():
        bias_ref[...] = jnp.where(adm & ((sc >= tau) | take_all), 0.0, NEG_INF)

    @pl.when(excess)
    def _():
        needf = need.astype(F32)
        chunks = [(i, (sc_ref[:, i * LANES:(i + 1) * LANES] == tau) & adm[:, i * LANES:(i + 1) * LANES])
                  for i in range(t_eff // LANES)]
        for i, rank in _tie_rank(chunks, tq):
            cs = slice(i * LANES, (i + 1) * LANES)
            kc = sc_ref[:, cs]
            sel = ((kc > tau) | ((kc == tau) & (rank < needf)) | take_all) & adm[:, cs]
            bias_ref[:, cs] = jnp.where(sel, 0.0, NEG_INF)

    bias = bias_ref[...]
    for hd in range(N_HEADS):
        logit = _dot(q_ref[hd], ktb_ref[hd]) + bias
        m = jnp.max(logit, axis=1, keepdims=True)
        p = jnp.exp(logit - m)
        denom = jnp.sum(p, axis=1, keepdims=True)
        o = _dot_nt(p.astype(BF16), vtb_ref[hd]) / denom
        o_ref[:, hd * HEAD_DIM:(hd + 1) * HEAD_DIM] = o.astype(BF16)


def _attn_prompt(qi, wi, q, kitb, ktb, vtb, b, t, tq, first_block, n_blocks, t_eff):
    nq = t // tq
    topk = min(IDX_TOPK_MAX, t // 4)
    assert (first_block + n_blocks) * tq <= t_eff
    qrow = lambda i, j: i * nq + first_block + j
    qspec = pl.BlockSpec((N_HEADS, tq, HEAD_DIM), lambda i, j: (0, qrow(i, j), 0))
    kspec = pl.BlockSpec((None, N_HEADS, HEAD_DIM, t_eff), lambda i, j: (i, 0, 0, 0))
    return pl.pallas_call(
        functools.partial(_attn_prompt_kernel, tq=tq, t_eff=t_eff, topk=topk, first_block=first_block),
        grid=(b, n_blocks),
        in_specs=[qspec, pl.BlockSpec((tq, N_IDX_HEADS), lambda i, j: (qrow(i, j), 0)), qspec,
                  pl.BlockSpec((None, IDX_DIM, t_eff), lambda i, j: (i, 0, 0)), kspec, kspec],
        out_specs=pl.BlockSpec((None, tq, ATT_W), lambda i, j: (i, j, 0)),
        out_shape=jax.ShapeDtypeStruct((b, n_blocks * tq, ATT_W), BF16),
        scratch_shapes=[pltpu.VMEM((tq, t_eff), F32), pltpu.VMEM((tq, t_eff), F32)],
        compiler_params=_cparams(("parallel", "arbitrary")),
        name="attn_prompt",
    )(qi, wi, q, kitb, ktb, vtb)


def _indexer_rows(qi, keys_t, wi_col):
    s = jnp.maximum(_dot(qi, keys_t), 0.0) * wi_col
    n = s.shape[1]
    return jnp.sum(s.reshape(N_IDX_HEADS, SUBLANES, n), axis=0) * IDX_SCALE


def _sample_scores_kernel(pt_ref, qi_ref, wi_ref, kin_ref, *refs, pages_per_step, page):
    del pt_ref
    page_refs, out_ref, new_ref = refs[:pages_per_step], refs[pages_per_step], refs[pages_per_step + 1]
    qi = qi_ref[...]
    wi_col = wi_ref[...]
    for i in range(pages_per_step):
        out_ref[:, i * page:(i + 1) * page] = _indexer_rows(qi, page_refs[i][...].astype(BF16), wi_col)

    @pl.when(pl.program_id(1) == 0)
    def _():
        new_ref[...] = _indexer_rows(qi, kin_ref[...], wi_col)


def _sample_scores(page_table, qi_s, wi_col, kitb_new_pad, cache_kidx_t, layer, pages_per_step):
    bsz = qi_s.shape[0]
    n_pages = page_table.shape[0] // bsz
    page = cache_kidx_t.shape[-1]
    steps = n_pages // pages_per_step

    def page_spec(i):
        return pl.BlockSpec((None, None, IDX_DIM, page),
                            lambda b, g, pt: (layer, pt[b * n_pages + g * pages_per_step + i], 0, 0))

    grid_spec = pltpu.PrefetchScalarGridSpec(
        num_scalar_prefetch=1,
        grid=(bsz, steps),
        in_specs=[pl.BlockSpec((None, N_IDX_HEADS * SUBLANES, IDX_DIM), lambda b, g, pt: (b, 0, 0)),
                  pl.BlockSpec((None, N_IDX_HEADS * SUBLANES, 1), lambda b, g, pt: (b, 0, 0)),
                  pl.BlockSpec((None, IDX_DIM, LANES), lambda b, g, pt: (b, 0, 0))]
                 + [page_spec(i) for i in range(pages_per_step)],
        out_specs=[pl.BlockSpec((SUBLANES, pages_per_step * page), lambda b, g, pt: (b, g)),
                   pl.BlockSpec((SUBLANES, LANES), lambda b, g, pt: (b, 0))],
    )
    return pl.pallas_call(
        functools.partial(_sample_scores_kernel, pages_per_step=pages_per_step, page=page),
        grid_spec=grid_spec,
        out_shape=[jax.ShapeDtypeStruct((bsz * SUBLANES, n_pages * page), F32),
                   jax.ShapeDtypeStruct((bsz * SUBLANES, LANES), F32)],
        compiler_params=_cparams(("parallel", "arbitrary")),
        name="sample_scores",
    )(page_table, qi_s, wi_col, kitb_new_pad, *([cache_kidx_t] * pages_per_step))


def _sample_select_kernel(sp_ref, sn_ref, selp_ref, seln_ref, *, rows, t_new, past, topk):
    jn = lax.broadcasted_iota(jnp.int32, (rows, LANES), 1)
    tn = lax.broadcasted_iota(jnp.int32, (rows, LANES), 0) & (SUBLANES - 1)
    adm_n = (jn <= tn) & (jn < t_new)
    sn = jnp.where(adm_n, sn_ref[...], NEG_INF)

    tau = _kth_largest(lambda cand: _count(sp_ref[...] >= cand) + _count(sn >= cand), rows, topk)
    sp = sp_ref[...]
    need = topk - (_count(sp > tau) + _count(sn > tau))
    eqn = (sn == tau) & adm_n
    excess = jnp.max(_count(sp == tau) + _count(eqn) - need) > 0

    @pl.when(jnp.logical_not(excess))
    def _():
        selp_ref[...] = jnp.where(sp >= tau, 1.0, 0.0)
        seln_ref[...] = jnp.where((sn >= tau) & adm_n, 1.0, 0.0)

    @pl.when(excess)
    def _():
        needf = need.astype(F32)
        n_chunks = past // LANES
        chunks = [(i, sp_ref[:, i * LANES:(i + 1) * LANES] == tau) for i in range(n_chunks)]
        chunks.append((n_chunks, eqn))
        for i, rank in _tie_rank(chunks, rows):
            if i < n_chunks:
                kc = sp_ref[:, i * LANES:(i + 1) * LANES]
                sel = (kc > tau) | ((kc == tau) & (rank < needf))
                selp_ref[:, i * LANES:(i + 1) * LANES] = jnp.where(sel, 1.0, 0.0)
            else:
                sel = ((sn > tau) | ((sn == tau) & (rank < needf))) & adm_n
                seln_ref[...] = jnp.where(sel, 1.0, 0.0)


def _sample_select(scores_past, scores_new, t_new, rows):
    n, past = scores_past.shape
    assert past >= IDX_TOPK_MAX, "every query must see at least top-k admissible keys"
    assert t_new == SUBLANES and n % rows == 0
    topk = min(IDX_TOPK_MAX, (past + t_new) // 4)
    blk = lambda w: pl.BlockSpec((rows, w), lambda i: (i, 0))
    return pl.pallas_call(
        functools.partial(_sample_select_kernel, rows=rows, t_new=t_new, past=past, topk=topk),
        grid=(n // rows,),
        in_specs=[blk(past), blk(LANES)],
        out_specs=[blk(past), blk(LANES)],
        out_shape=[jax.ShapeDtypeStruct((n, past), F32), jax.ShapeDtypeStruct((n, LANES), F32)],
        compiler_params=_cparams(("parallel",)),
        name="sample_select",
    )(scores_past, scores_new)


def _sample_attend_kernel(pt_ref, q_ref, selp_ref, seln_ref, *refs, n_groups, pages_per_step, page):
    del pt_ref
    g = pages_per_step
    k_refs, v_refs = refs[:g], refs[g:2 * g]
    kn_ref, vn_ref, o_ref, m_ref, l_ref, acc_ref = refs[2 * g:]
    step = pl.program_id(1)
    rows = N_HEADS * SUBLANES

    @pl.when(step == 0)
    def _():
        m_ref[...] = jnp.full((rows, 1), NEG_INF, F32)
        l_ref[...] = jnp.zeros((rows, 1), F32)
        acc_ref[...] = jnp.zeros((rows, ATT_W), F32)

    qrow = lax.broadcasted_iota(jnp.int32, (rows, ATT_W), 0)
    qcol = lax.broadcasted_iota(jnp.int32, (rows, ATT_W), 1)
    q_bd = jnp.where((qcol >> 6) == (qrow >> 3), jnp.concatenate([q_ref[...]] * N_HEADS, axis=1), 0.0).astype(BF16)

    def process(k_pages, v_pages, sel):
        logit = jnp.concatenate([_dot(q_bd, kp.reshape(ATT_W, page).astype(BF16)) for kp in k_pages], axis=1)
        valid = jnp.concatenate([sel] * N_HEADS, axis=0) > 0.5
        logit = jnp.where(valid, logit, NEG_INF)
        m_old = m_ref[...]
        m_new = jnp.maximum(m_old, jnp.max(logit, axis=1, keepdims=True))
        m_safe = jnp.where(m_new == NEG_INF, 0.0, m_new)
        pr = jnp.exp(logit - m_safe)
        alpha = jnp.exp(m_old - m_safe)
        l_ref[...] = alpha * l_ref[...] + jnp.sum(pr, axis=1, keepdims=True)
        prb = pr.astype(BF16)
        pv = jnp.zeros((rows, ATT_W), F32)
        for i, vp in enumerate(v_pages):
            pv = pv + _dot_nt(prb[:, i * page:(i + 1) * page], vp.reshape(ATT_W, page).astype(BF16))
        acc_ref[...] = alpha * acc_ref[...] + pv
        m_ref[...] = m_new

    @pl.when(step < n_groups)
    def _():
        process([r[...] for r in k_refs], [r[...] for r in v_refs], selp_ref[...])

    @pl.when(step == n_groups)
    def _():
        process([kn_ref[...]], [vn_ref[...]], seln_ref[...])
        out = acc_ref[...] / l_ref[...]
        for hd in range(N_HEADS):
            o_ref[:, hd * HEAD_DIM:(hd + 1) * HEAD_DIM] = out[hd * SUBLANES:(hd + 1) * SUBLANES,
                                                              hd * HEAD_DIM:(hd + 1) * HEAD_DIM]


def _sample_attend(page_table, q_s, selp, seln, cache_kt, cache_vt, kt_new_pad, vt_new_pad, layer, pages_per_step):
    bsz = q_s.shape[0]
    n_pages = page_table.shape[0] // bsz
    page = cache_kt.shape[-1]
    g = pages_per_step
    n_groups = n_pages // g
    last = n_groups - 1

    def page_spec(i):
        return pl.BlockSpec(
            (None, None, N_HEADS, HEAD_DIM, page),
            lambda b, s, pt: (layer, pt[b * n_pages + jnp.minimum(s, last) * g + i], 0, 0, 0))

    new_spec = pl.BlockSpec((None, N_HEADS, HEAD_DIM, page), lambda b, s, pt: (b, 0, 0, 0))
    rows = N_HEADS * SUBLANES
    grid_spec = pltpu.PrefetchScalarGridSpec(
        num_scalar_prefetch=1,
        grid=(bsz, n_groups + 1),
        in_specs=[pl.BlockSpec((None, rows, HEAD_DIM), lambda b, s, pt: (b, 0, 0)),
                  pl.BlockSpec((None, SUBLANES, g * page), lambda b, s, pt: (b, 0, jnp.minimum(s, last))),
                  pl.BlockSpec((None, SUBLANES, LANES), lambda b, s, pt: (b, 0, 0))]
                 + [page_spec(i) for i in range(g)] + [page_spec(i) for i in range(g)]
                 + [new_spec, new_spec],
        out_specs=pl.BlockSpec((None, SUBLANES, ATT_W), lambda b, s, pt: (b, 0, 0)),
        scratch_shapes=[pltpu.VMEM((rows, 1), F32), pltpu.VMEM((rows, 1), F32), pltpu.VMEM((rows, ATT_W), F32)],
    )
    return pl.pallas_call(
        functools.partial(_sample_attend_kernel, n_groups=n_groups, pages_per_step=g, page=page),
        grid_spec=grid_spec,
        out_shape=jax.ShapeDtypeStruct((bsz, SUBLANES, ATT_W), F32),
        compiler_params=_cparams(("parallel", "arbitrary")),
        name="sample_attend",
    )(page_table, q_s, selp, seln, *([cache_kt] * g), *([cache_vt] * g), kt_new_pad, vt_new_pad)


def _mix_kernel(x_ref, c_ref, o_ref, g0_ref, b0_ref, wg_ref, bg_ref, wco_ref, wao_ref, wo_ref,
                g1_ref, b1_ref, wrh_ref, wrl_ref, br_ref, x1_ref, comb_ref, *, alpha):
    h = _layer_norm(x_ref[...], g0_ref[...], b0_ref[...])
    hb = h.astype(BF16)
    gc = _sigmoid(_dot(hb, wg_ref[:, :D_MODEL]) + bg_ref[:, :D_MODEL])
    ga = _sigmoid(_dot(hb, wg_ref[:, D_MODEL:]) + bg_ref[:, D_MODEL:])
    merged = gc * _dot(c_ref[...], wco_ref[...]) + ga * _dot(o_ref[...], wao_ref[...])
    mix = _dot(merged.astype(BF16), wo_ref[...])
    x1 = _layer_norm(alpha * h + mix, g1_ref[...], b1_ref[...])
    x1_ref[...] = x1

    xh = x1.astype(BF16)
    xl = (x1 - xh.astype(F32)).astype(BF16)
    logits = _dot(xh, wrh_ref[...]) + (_dot(xl, wrh_ref[...]) + _dot(xh, wrl_ref[...])) + br_ref[...]
    lane = lax.broadcasted_iota(jnp.int32, logits.shape, 1)
    is_grp = (lane >= N_EXPERTS) & (lane < N_EXPERTS + N_GROUPS)
    gl = jnp.where(is_grp, logits, NEG_INF)
    gmax = jnp.max(gl, axis=1, keepdims=True)
    grp = jnp.min(jnp.where(gl == gmax, lane, 4 * LANES), axis=1, keepdims=True) - N_EXPERTS
    p_grp = 1.0 / jnp.sum(jnp.exp(gl - gmax), axis=1, keepdims=True)
    in_grp = (lane < N_EXPERTS) & ((lane >> 3) == grp)
    el = jnp.where(in_grp, logits, NEG_INF)
    emax = jnp.max(el, axis=1, keepdims=True)
    ee = jnp.exp(el - emax)
    pe = ee / jnp.sum(ee, axis=1, keepdims=True)
    pe = jnp.where(in_grp, pe, -1.0)
    p1 = jnp.max(pe, axis=1, keepdims=True)
    i1 = jnp.min(jnp.where(pe == p1, lane, 4 * LANES), axis=1, keepdims=True)
    pe2 = jnp.where(lane == i1, -1.0, pe)
    p2 = jnp.max(pe2, axis=1, keepdims=True)
    i2 = jnp.min(jnp.where(pe2 == p2, lane, 4 * LANES), axis=1, keepdims=True)
    tot = p1 + p2
    comb_ref[...] = (jnp.where(lane == i1, p_grp * (p1 / tot), 0.0)
                     + jnp.where(lane == i2, p_grp * (p2 / tot), 0.0)
                     + jnp.where(lane == GROUP_LANE, grp.astype(F32), 0.0))


def _mix(x, c, o, g0, b0, wg, bg, wco, wao, wo, g1, b1, wrh, wrl, br, alpha, tm):
    n = x.shape[0]
    row = lambda i: (i, 0)
    full = lambda i: (0, 0)
    fs = lambda a: pl.BlockSpec(a.shape, full)
    return pl.pallas_call(
        functools.partial(_mix_kernel, alpha=alpha),
        grid=(n // tm,),
        in_specs=[pl.BlockSpec((tm, D_MODEL), row), pl.BlockSpec((tm, CONV_CH), row),
                  pl.BlockSpec((tm, ATT_W), row)] + [fs(a) for a in (g0, b0, wg, bg, wco, wao, wo, g1, b1, wrh, wrl, br)],
        out_specs=[pl.BlockSpec((tm, D_MODEL), row), pl.BlockSpec((tm, LANES), row)],
        out_shape=[jax.ShapeDtypeStruct((n, D_MODEL), F32), jax.ShapeDtypeStruct((n, LANES), F32)],
        compiler_params=_cparams(("parallel",)),
        name="mix",
    )(x, c, o, g0, b0, wg, bg, wco, wao, wo, g1, b1, wrh, wrl, br)


def _split3(a):
    hi = a.astype(BF16)
    r1 = a - hi.astype(F32)
    mid = r1.astype(BF16)
    lo = (r1 - mid.astype(F32)).astype(BF16)
    return hi, mid, lo


def _permute_rows(perm, a):
    hi, mid, lo = _split3(a)
    return _dot(perm, hi) + _dot(perm, mid) + _dot(perm, lo)


def _ffn_kernel(x1_ref, comb_ref, p_ref, wpg_ref, wpp_ref, weg_ref, weu_ref, wed_ref, g2_ref, b2_ref,
                y_ref, to_slot_ref, to_token_ref, xs_ref, cs_ref, acc_ref, meta_ref, *, alpha, tm, rb, eps):
    s = pl.program_id(1)

    @pl.when(s == 0)
    def _():
        comb = comb_ref[...]
        lane = lax.broadcasted_iota(jnp.int32, (tm, LANES), 1)
        lane_row = lax.broadcasted_iota(jnp.int32, (1, LANES), 1)
        grp = comb[:, GROUP_LANE:GROUP_LANE + 1]
        onehot = jnp.where((lane.astype(F32) == grp) & (lane < N_GROUPS), 1.0, 0.0)
        r = lax.broadcasted_iota(jnp.int32, (tm, tm), 0)
        c = lax.broadcasted_iota(jnp.int32, (tm, tm), 1)
        earlier = jnp.where(c < r, 1.0, 0.0).astype(BF16)
        before = _dot(earlier, onehot.astype(BF16))
        cnt = jnp.sum(onehot, axis=0, keepdims=True)
        base = jnp.zeros((1, LANES), F32)
        run = jnp.zeros((1, 1), F32)
        for g in range(N_GROUPS):
            base = base + jnp.where(lane_row == g, run, 0.0)
            meta_ref[g] = run[0, 0].astype(jnp.int32)
            meta_ref[N_GROUPS + g] = cnt[0, g].astype(jnp.int32)
            run = run + cnt[:, g:g + 1]
        slot = jnp.sum(onehot * (before + base), axis=1, keepdims=True)
        slot_row = jnp.transpose(jnp.broadcast_to(slot, (tm, LANES)))[0:1, :]
        to_slot_ref[...] = jnp.where(slot == c.astype(F32), 1.0, 0.0).astype(BF16)
        to_token = jnp.where(slot_row == r.astype(F32), 1.0, 0.0).astype(BF16)
        to_token_ref[...] = to_token
        xs_ref[...] = _dot(to_token, x1_ref[...].astype(BF16)).astype(BF16)
        cs_ref[...] = _permute_rows(to_token, comb)
        acc_ref[...] = jnp.zeros((tm, D_MODEL), F32)

    g = (s * eps) // EXPERTS_PER_GROUP
    lo = meta_ref[g]
    hi = lo + meta_ref[N_GROUPS + g]
    for blk in range(tm // rb):
        @pl.when((lo < (blk + 1) * rb) & (hi > blk * rb))
        def _():
            rows = slice(blk * rb, (blk + 1) * rb)
            xb = xs_ref[rows, :]
            cs = cs_ref[rows, :]
            lane = lax.broadcasted_iota(jnp.int32, (rb, LANES), 1)
            out = jnp.zeros((rb, D_MODEL), F32)
            for j in range(eps):
                ce = jnp.sum(jnp.where(lane == s * eps + j, cs, 0.0), axis=1, keepdims=True)
                hg = _dot(xb, weg_ref[j])
                hu = _dot(xb, weu_ref[j])
                hidden = (hg * _sigmoid(hg)) * hu * ce
                out = out + _dot(hidden.astype(BF16), wed_ref[j])
            acc_ref[rows, :] += out

    @pl.when(s == N_EXPERTS // eps - 1)
    def _():
        x1 = x1_ref[...]
        ffn = _permute_rows(to_slot_ref[...], acc_ref[...])
        ple = _sigmoid(_dot(x1.astype(BF16), wpg_ref[...])) * _dot(p_ref[...].astype(BF16), wpp_ref[...])
        y_ref[...] = _layer_norm(alpha * x1 + ffn + ple, g2_ref[...], b2_ref[...])


def _ffn(x1, comb, p, wpg, wpp, weg, weu, wed, g2, b2, alpha, tm, rb, eps):
    n = x1.shape[0]
    assert tm % rb == 0 and EXPERTS_PER_GROUP % eps == 0
    row = lambda i, e: (i, 0)
    full = lambda i, e: (0, 0)
    exp = lambda i, e: (e, 0, 0)
    return pl.pallas_call(
        functools.partial(_ffn_kernel, alpha=alpha, tm=tm, rb=rb, eps=eps),
        grid=(n // tm, N_EXPERTS // eps),
        in_specs=[pl.BlockSpec((tm, D_MODEL), row), pl.BlockSpec((tm, LANES), row),
                  pl.BlockSpec((tm, PLE_DIM), row),
                  pl.BlockSpec((D_MODEL, D_MODEL), full), pl.BlockSpec((PLE_DIM, D_MODEL), full),
                  pl.BlockSpec((eps, D_MODEL, D_EXPERT), exp), pl.BlockSpec((eps, D_MODEL, D_EXPERT), exp),
                  pl.BlockSpec((eps, D_EXPERT, D_MODEL), exp),
                  pl.BlockSpec((1, D_MODEL), full), pl.BlockSpec((1, D_MODEL), full)],
        out_specs=pl.BlockSpec((tm, D_MODEL), row),
        out_shape=jax.ShapeDtypeStruct((n, D_MODEL), F32),
        scratch_shapes=[pltpu.VMEM((tm, tm), BF16), pltpu.VMEM((tm, tm), BF16),
                        pltpu.VMEM((tm, D_MODEL), BF16), pltpu.VMEM((tm, LANES), F32),
                        pltpu.VMEM((tm, D_MODEL), F32), pltpu.SMEM((2 * N_GROUPS,), jnp.int32)],
        compiler_params=pltpu.CompilerParams(dimension_semantics=("parallel", "arbitrary"),
                                             vmem_limit_bytes=VMEM_LIMIT_FFN),
        name="ffn",
    )(x1, comb, p, wpg, wpp, weg, weu, wed, g2, b2)


def _tile(n, target):
    t = min(n, target)
    assert n % t == 0, (n, t)
    return t


def _row(v):
    return v.reshape(1, -1).astype(F32)


def kernel(x_prompt, x_sample, cache_k, cache_v, cache_kidx, state_conv, page_table, p_prompt, p_sample,
           ln0_g, ln0_b, w_in, b_gate, conv_w, conv_b, lnc_g, lnc_b, w_conv_out, w_attn_out, w_o,
           ln1_g, ln1_b, w_rg, b_rg, w_re, b_re, w_eg, w_eu, w_ed, w_pg, w_pp, ln2_g, ln2_b):
    depth = w_in.shape[0]
    assert depth == 1, "single-layer step"
    assert w_in.shape[1:] == (D_MODEL, COL_END)
    layer = 0
    alpha = (2.0 * depth) ** 0.25
    bp, tp, _ = x_prompt.shape
    bs, ts, _ = x_sample.shape
    assert ts == SUBLANES, "sample step length must fill one sublane tile"
    page = cache_k.shape[2]
    n_pages = page_table.shape[1]
    assert page == LANES

    g0, b0 = _row(ln0_g), _row(ln0_b)
    w = w_in[layer]
    wi_cols = jnp.pad(w[:, COL_WI:COL_GC], ((0, 0), (0, LANES - N_IDX_HEADS)))
    ws = jnp.concatenate([w[:, COL_GLU_A:COL_K], w[:, COL_QI:COL_KI], wi_cols], axis=1).astype(BF16)
    wt = w[:, COL_K:COL_QI].T
    wt = jnp.concatenate([wt, w[:, COL_KI:COL_WI].T], axis=0).astype(BF16)
    wg = w[:, COL_GC:].astype(BF16)
    bg = _row(b_gate[layer])
    wr = jnp.pad(jnp.concatenate([w_re[layer], w_rg[layer]], axis=1), ((0, 0), (0, LANES - N_EXPERTS - N_GROUPS)))
    wrh = wr.astype(BF16)
    wrl = (wr - wrh.astype(F32)).astype(BF16)
    br = jnp.pad(jnp.concatenate([b_re[layer], b_rg[layer]]), (0, LANES - N_EXPERTS - N_GROUPS)).reshape(1, LANES)
    wco, wao, wo = w_conv_out[layer].astype(BF16), w_attn_out[layer].astype(BF16), w_o[layer].astype(BF16)
    wpg, wpp = w_pg[layer].astype(BF16), w_pp[layer].astype(BF16)
    weg, weu, wed = w_eg[layer].astype(BF16), w_eu[layer].astype(BF16), w_ed[layer].astype(BF16)
    cw, cb = conv_w[layer], _row(conv_b[layer])
    lcg, lcb = _row(lnc_g[layer]), _row(lnc_b[layer])
    g1, b1, g2, b2 = _row(ln1_g[layer]), _row(ln1_b[layer]), _row(ln2_g[layer]), _row(ln2_b[layer])
    cache_kt = jnp.transpose(cache_k, (0, 1, 3, 4, 2))
    cache_vt = jnp.transpose(cache_v, (0, 1, 3, 4, 2))
    cache_kidx_t = jnp.transpose(cache_kidx, (0, 1, 3, 2))

    def tail(x1, comb, p):
        n = x1.shape[0]
        tm = _tile(n, 1024)
        return _ffn(x1, comb, p.reshape(n, PLE_DIM), wpg, wpp, weg, weu, wed, g2, b2, alpha, tm, _tile(tm, 256), 4)

    def mix(x, c, o):
        n = x.shape[0]
        return _mix(x, c, o, g0, b0, wg, bg, wco, wao, wo, g1, b1, wrh, wrl, br, alpha, _tile(n, 512))

    np_ = bp * tp
    xp = x_prompt.reshape(np_, D_MODEL)
    glu, q, qi, wi, kt, vt, ktb, vtb, kit, kitb = _proj(xp, g0, b0, ws, wt, bp, tp, _tile(tp, 512))
    c, newconv_p = _conv(glu.reshape(bp, tp, CONV_CH), None, cw, cb, lcg, lcb, _tile(tp, 256))
    tq = _tile(tp, 256)
    bucket = 2 * tq if tp % (2 * tq) == 0 else tp
    per_bucket = bucket // tq
    o = jnp.concatenate(
        [_attn_prompt(qi, wi, q, kitb, ktb, vtb, bp, tp, tq, i * per_bucket, per_bucket, (i + 1) * bucket)
         for i in range(tp // bucket)], axis=1).reshape(np_, ATT_W)
    x1, comb = mix(xp, c.reshape(np_, CONV_CH), o)
    y_prompt = tail(x1, comb, p_prompt[layer]).reshape(bp, tp, D_MODEL)
    new_k_p = jnp.transpose(kt, (0, 3, 1, 2))[None]
    new_v_p = jnp.transpose(vt, (0, 3, 1, 2))[None]
    new_ki_p = jnp.transpose(kit, (0, 2, 1))[None]

    ns = bs * ts
    xs = x_sample.reshape(ns, D_MODEL)
    glu, q, qi, wi, kt, vt, _, _, kit, kitb = _proj(xs, g0, b0, ws, wt, 1, ns, ns)
    c, newconv_s = _conv(glu.reshape(bs, ts, CONV_CH), state_conv[layer], cw, cb, lcg, lcb, ts)
    to_rows = lambda a: a.reshape(N_HEADS, bs, ts, HEAD_DIM).transpose(1, 0, 2, 3).reshape(bs, N_HEADS * ts, HEAD_DIM)
    qi_s, q_s = to_rows(qi), to_rows(q)
    wi_col = wi.reshape(bs, ts, N_IDX_HEADS).transpose(0, 2, 1).reshape(bs, N_IDX_HEADS * ts, 1)
    kt_s = kt.reshape(N_HEADS, HEAD_DIM, bs, ts)
    vt_s = vt.reshape(N_HEADS, HEAD_DIM, bs, ts)
    kit_s = kit.reshape(IDX_DIM, bs, ts)
    pad_keys = lambda a: jnp.pad(a, ((0, 0),) * (a.ndim - 1) + ((0, page - ts),))
    pt_flat = page_table.reshape(-1).astype(jnp.int32)
    pages_per = lambda target: max(d for d in range(1, target + 1) if n_pages % d == 0)
    scores_past, scores_new = _sample_scores(
        pt_flat, qi_s, wi_col, pad_keys(kitb.reshape(IDX_DIM, bs, ts).transpose(1, 0, 2)),
        cache_kidx_t, layer, pages_per(32))
    selp, seln = _sample_select(scores_past, scores_new, ts, _tile(ns, 64))
    o = _sample_attend(pt_flat, q_s, selp.reshape(bs, ts, -1), seln.reshape(bs, ts, LANES), cache_kt, cache_vt,
                       pad_keys(kt_s.transpose(2, 0, 1, 3)), pad_keys(vt_s.transpose(2, 0, 1, 3)),
                       layer, pages_per(16))
    x1, comb = mix(xs, c.reshape(ns, CONV_CH), o.reshape(ns, ATT_W).astype(BF16))
    y_sample = tail(x1, comb, p_sample[layer]).reshape(bs, ts, D_MODEL)
    new_k_s = kt_s.transpose(2, 3, 0, 1)[None]
    new_v_s = vt_s.transpose(2, 3, 0, 1)[None]
    new_ki_s = kit_s.transpose(1, 2, 0)[None]

    return (y_prompt, y_sample, new_k_p, new_v_p, new_ki_p, newconv_p[None],
            new_k_s, new_v_s, new_ki_s, newconv_s[None])
```

```python
import functools

import jax
import jax.numpy as jnp
from jax import lax
from jax.experimental import pallas as pl
from jax.experimental.pallas import tpu as pltpu

D_MODEL = 1024
CONV_CH = 512
CONV_WIDTH = 31
CONV_HIST = CONV_WIDTH - 1
N_HEADS = 8
HEAD_DIM = 64
ATT_W = N_HEADS * HEAD_DIM
N_IDX_HEADS = 8
IDX_DIM = 64
IDX_TOPK_MAX = 256
N_GROUPS = 4
EXPERTS_PER_GROUP = 8
N_EXPERTS = N_GROUPS * EXPERTS_PER_GROUP
D_EXPERT = 256
PLE_DIM = 256
LN_EPS = 1e-5
IDX_SCALE = (N_IDX_HEADS * IDX_DIM) ** -0.5
ATT_SCALE = HEAD_DIM ** -0.5
COL_GLU_A, COL_GLU_B, COL_Q, COL_K, COL_V, COL_QI, COL_KI, COL_WI, COL_GC = (
    0, 512, 1024, 1536, 2048, 2560, 3072, 3136, 3144)
COL_END = COL_GC + 2 * D_MODEL

LANES = 128
SUBLANES = 8
PREFIX_ROWS = 32
GROUP_LANE = LANES - 1
VMEM_LIMIT = 48 * 1024 * 1024
VMEM_LIMIT_FFN = 56 * 1024 * 1024

F32 = jnp.float32
BF16 = jnp.bfloat16
NEG_INF = float("-inf")
INT_MIN = -2 ** 31


def _cparams(sem):
    return pltpu.CompilerParams(dimension_semantics=sem, vmem_limit_bytes=VMEM_LIMIT)


def _layer_norm(x, g, b):
    mu = jnp.mean(x, axis=-1, keepdims=True)
    xc = x - mu
    var = jnp.mean(xc * xc, axis=-1, keepdims=True)
    return xc * lax.rsqrt(var + LN_EPS) * g + b


def _sigmoid(x):
    return 1.0 / (1.0 + jnp.exp(-x))


def _dot(a, b):
    return jnp.dot(a, b, preferred_element_type=F32)


def _dot_nt(a, b):
    return lax.dot_general(a, b, (((1,), (1,)), ((), ())), preferred_element_type=F32)


def _proj_kernel(x_ref, g0_ref, b0_ref, ws_ref, wt_ref,
                 glu_ref, q_ref, qi_ref, wi_ref, kt_ref, vt_ref, ktb_ref, vtb_ref, kit_ref, kitb_ref):
    h = _layer_norm(x_ref[...], g0_ref[...], b0_ref[...])
    hb = h.astype(BF16)

    def seg(col, width=512):
        return _dot(hb, ws_ref[:, col:col + width])

    glu_ref[...] = seg(0) * _sigmoid(seg(512))
    q = seg(1024) * ATT_SCALE
    qi = seg(1536)
    for hd in range(N_HEADS):
        sl = slice(hd * HEAD_DIM, (hd + 1) * HEAD_DIM)
        q_ref[hd] = q[:, sl].astype(BF16)
        qi_ref[hd] = qi[:, sl].astype(BF16)
    wi_ref[...] = seg(2048, LANES)[:, :N_IDX_HEADS]

    tm = hb.shape[0]
    kt = _dot_nt(wt_ref[0:ATT_W, :], hb).reshape(N_HEADS, HEAD_DIM, tm)
    kt_ref[...] = kt
    ktb_ref[...] = kt.astype(BF16)
    vt = _dot_nt(wt_ref[ATT_W:2 * ATT_W, :], hb).reshape(N_HEADS, HEAD_DIM, tm)
    vt_ref[...] = vt
    vtb_ref[...] = vt.astype(BF16)
    kit = _dot_nt(wt_ref[2 * ATT_W:2 * ATT_W + IDX_DIM, :], hb)
    kit_ref[...] = kit
    kitb_ref[...] = kit.astype(BF16)


def _proj(x, g0, b0, ws, wt, b, t, tm):
    n = b * t
    nt = t // tm
    row = lambda i: (i, 0)
    head = lambda i: (0, i, 0)
    full = lambda i: (0, 0)
    hm = jax.ShapeDtypeStruct((N_HEADS, n, HEAD_DIM), BF16)
    hm_spec = pl.BlockSpec((N_HEADS, tm, HEAD_DIM), head)
    kv_spec = pl.BlockSpec((None, N_HEADS, HEAD_DIM, tm), lambda i: (i // nt, 0, 0, i % nt))
    ki_spec = pl.BlockSpec((None, IDX_DIM, tm), lambda i: (i // nt, 0, i % nt))
    kv_shape = lambda dt: jax.ShapeDtypeStruct((b, N_HEADS, HEAD_DIM, t), dt)
    ki_shape = lambda dt: jax.ShapeDtypeStruct((b, IDX_DIM, t), dt)
    return pl.pallas_call(
        _proj_kernel,
        grid=(n // tm,),
        in_specs=[pl.BlockSpec((tm, D_MODEL), row),
                  pl.BlockSpec((1, D_MODEL), full), pl.BlockSpec((1, D_MODEL), full),
                  pl.BlockSpec(ws.shape, full), pl.BlockSpec(wt.shape, full)],
        out_specs=[pl.BlockSpec((tm, CONV_CH), row), hm_spec, hm_spec,
                   pl.BlockSpec((tm, N_IDX_HEADS), row),
                   kv_spec, kv_spec, kv_spec, kv_spec, ki_spec, ki_spec],
        out_shape=[jax.ShapeDtypeStruct((n, CONV_CH), F32), hm, hm,
                   jax.ShapeDtypeStruct((n, N_IDX_HEADS), F32),
                   kv_shape(F32), kv_shape(F32), kv_shape(BF16), kv_shape(BF16), ki_shape(F32), ki_shape(BF16)],
        compiler_params=_cparams(("parallel",)),
        name="proj",
    )(x, g0, b0, ws, wt)


def _conv_kernel(*refs, tc, nt, has_state):
    it = iter(refs)
    glu_ref = next(it)
    prev_ref = next(it) if nt > 1 else None
    state_ref = next(it) if has_state else None
    w_ref, cb_ref, g_ref, b_ref, c_ref, newconv_ref, buf_ref, shift_ref, acc_ref = it
    t = pl.program_id(1)

    @pl.when(t == 0)
    def _():
        buf_ref[0:PREFIX_ROWS, :] = jnp.zeros((PREFIX_ROWS, CONV_CH), F32)
        if has_state:
            buf_ref[2:PREFIX_ROWS, :] = state_ref[...]

    if nt > 1:
        @pl.when(t > 0)
        def _():
            buf_ref[0:PREFIX_ROWS, :] = prev_ref[...]

    buf_ref[PREFIX_ROWS:PREFIX_ROWS + tc, :] = glu_ref[...]

    shifted_rows = tc + PREFIX_ROWS - SUBLANES
    for s in range(1, SUBLANES):
        shift_ref[s - 1] = buf_ref[s:s + shifted_rows, :]

    rb = min(tc, 32)
    for c0 in range(0, CONV_CH, LANES):
        cs = slice(c0, c0 + LANES)
        bias = cb_ref[:, cs]
        for r0 in range(0, tc, rb):
            acc = jnp.zeros((rb, LANES), F32) + bias
            for j in range(CONV_WIDTH):
                a, s = divmod(2 + j, SUBLANES)
                lo = r0 + a * SUBLANES
                rows = buf_ref[lo:lo + rb, cs] if s == 0 else shift_ref[s - 1, lo:lo + rb, cs]
                acc = acc + w_ref[j:j + 1, cs] * rows
            acc_ref[r0:r0 + rb, cs] = acc

    y = _layer_norm(acc_ref[...], g_ref[...], b_ref[...])
    c_ref[...] = (y * _sigmoid(y)).astype(BF16)

    @pl.when(t == nt - 1)
    def _():
        newconv_ref[...] = buf_ref[tc + 2:tc + PREFIX_ROWS, :]


def _conv(glu, state, conv_w, conv_b, lnc_g, lnc_b, tc):
    b, t, _ = glu.shape
    nt = t // tc
    blocks_per_tile = tc // PREFIX_ROWS
    in_specs = [pl.BlockSpec((None, tc, CONV_CH), lambda i, j: (i, j, 0))]
    args = [glu]
    if nt > 1:
        in_specs.append(pl.BlockSpec((None, PREFIX_ROWS, CONV_CH),
                                     lambda i, j: (i, jnp.maximum(j * blocks_per_tile - 1, 0), 0)))
        args.append(glu)
    if state is not None:
        in_specs.append(pl.BlockSpec((None, CONV_HIST, CONV_CH), lambda i, j: (i, 0, 0)))
        args.append(state)
    full = lambda i, j: (0, 0)
    in_specs += [pl.BlockSpec((CONV_WIDTH, CONV_CH), full), pl.BlockSpec((1, CONV_CH), full),
                 pl.BlockSpec((1, CONV_CH), full), pl.BlockSpec((1, CONV_CH), full)]
    args += [conv_w, conv_b, lnc_g, lnc_b]
    return pl.pallas_call(
        functools.partial(_conv_kernel, tc=tc, nt=nt, has_state=state is not None),
        grid=(b, nt),
        in_specs=in_specs,
        out_specs=[pl.BlockSpec((None, tc, CONV_CH), lambda i, j: (i, j, 0)),
                   pl.BlockSpec((None, CONV_HIST, CONV_CH), lambda i, j: (i, 0, 0))],
        out_shape=[jax.ShapeDtypeStruct((b, t, CONV_CH), BF16),
                   jax.ShapeDtypeStruct((b, CONV_HIST, CONV_CH), F32)],
        scratch_shapes=[pltpu.VMEM((PREFIX_ROWS + tc, CONV_CH), F32),
                        pltpu.VMEM((SUBLANES - 1, tc + PREFIX_ROWS - SUBLANES, CONV_CH), F32),
                        pltpu.VMEM((tc, CONV_CH), F32)],
        compiler_params=_cparams(("parallel", "arbitrary")),
        name="conv",
    )(*args)


def _ordered_to_f32(c):
    return pltpu.bitcast(c ^ ((c >> 31) & jnp.int32(0x7FFFFFFF)), F32)


def _kth_largest(count_ge_parts, rows, topk):
    def body(i, taus):
        bit = lax.shift_left(jnp.int32(1), jnp.int32(31) - i)
        out = []
        for count_ge, tau in zip(count_ge_parts, taus):
            cand = tau ^ bit
            out.append(jnp.where(count_ge(_ordered_to_f32(cand)) >= topk, cand, tau))
        return tuple(out)
    init = tuple(jnp.full((rows, 1), INT_MIN, jnp.int32) for _ in count_ge_parts)
    return [_ordered_to_f32(t) for t in lax.fori_loop(0, 32, body, init, unroll=2)]


def _count(mask):
    return jnp.sum(mask.astype(jnp.int32), axis=1, keepdims=True)


def _tie_rank(eq_chunks, rows):
    r = lax.broadcasted_iota(jnp.int32, (LANES, LANES), 0)
    c = lax.broadcasted_iota(jnp.int32, (LANES, LANES), 1)
    upper = jnp.where(r < c, 1.0, 0.0).astype(BF16)
    prefix = jnp.zeros((rows, 1), F32)
    for idx, eq in eq_chunks:
        eqf = jnp.where(eq, 1.0, 0.0)
        yield idx, _dot(eqf.astype(BF16), upper) + prefix
        prefix = prefix + jnp.sum(eqf, axis=1, keepdims=True)


def _attn_prompt_kernel(qi_ref, wi_ref, q_ref, kitb_ref, ktb_ref, vtb_ref, o_ref, sc_ref, bias_ref,
                        *, tq, t_eff, topk, first_block):
    qb = first_block + pl.program_id(1)
    wi = wi_ref[...]
    kitb = kitb_ref[...]
    acc = jnp.zeros((tq, t_eff), F32)
    for hd in range(N_IDX_HEADS):
        acc = acc + jnp.maximum(_dot(qi_ref[hd], kitb), 0.0) * wi[:, hd:hd + 1]
    q_pos = qb * tq + lax.broadcasted_iota(jnp.int32, (tq, 1), 0)
    k_pos = lax.broadcasted_iota(jnp.int32, (1, t_eff), 1)
    adm = k_pos <= q_pos
    sc_ref[...] = jnp.where(adm, acc * IDX_SCALE, NEG_INF)

    half = tq // 2
    tau = jnp.concatenate(
        _kth_largest([lambda cand: _count(sc_ref[0:half, :] >= cand),
                      lambda cand: _count(sc_ref[half:tq, :] >= cand)], half, topk), axis=0)
    take_all = (q_pos + 1) <= topk
    sc = sc_ref[...]
    eq = (sc == tau) & adm
    need = topk - _count(sc > tau)
    excess = jnp.max(jnp.where(take_all, 0, _count(eq) - need)) > 0

    @pl.when(jnp.logical_not(excess))
    def _():
        bias_ref[...] = jnp.where(adm & ((sc >= tau) | take_all), 0.0, NEG_INF)

    @pl.when(excess)
    def _():
        needf = need.astype(F32)
        chunks = [(i, (sc_ref[:, i * LANES:(i + 1) * LANES] == tau) & adm[:, i * LANES:(i + 1) * LANES])
                  for i in range(t_eff // LANES)]
        for i, rank in _tie_rank(chunks, tq):
            cs = slice(i * LANES, (i + 1) * LANES)
            kc = sc_ref[:, cs]
            sel = ((kc > tau) | ((kc == tau) & (rank < needf)) | take_all) & adm[:, cs]
            bias_ref[:, cs] = jnp.where(sel, 0.0, NEG_INF)

    bias = bias_ref[...]
    for hd in range(N_HEADS):
        logit = _dot(q_ref[hd], ktb_ref[hd]) + bias
        m = jnp.max(logit, axis=1, keepdims=True)
        p = jnp.exp(logit - m)
        denom = jnp.sum(p, axis=1, keepdims=True)
        o = _dot_nt(p.astype(BF16), vtb_ref[hd]) / denom
        o_ref[:, hd * HEAD_DIM:(hd + 1) * HEAD_DIM] = o.astype(BF16)


def _attn_prompt(qi, wi, q, kitb, ktb, vtb, b, t, tq, first_block, n_blocks, t_eff):
    nq = t // tq
    topk = min(IDX_TOPK_MAX, t // 4)
    assert (first_block + n_blocks) * tq <= t_eff
    qrow = lambda i, j: i * nq + first_block + j
    qspec = pl.BlockSpec((N_HEADS, tq, HEAD_DIM), lambda i, j: (0, qrow(i, j), 0))
    kspec = pl.BlockSpec((None, N_HEADS, HEAD_DIM, t_eff), lambda i, j: (i, 0, 0, 0))
    return pl.pallas_call(
        functools.partial(_attn_prompt_kernel, tq=tq, t_eff=t_eff, topk=topk, first_block=first_block),
        grid=(b, n_blocks),
        in_specs=[qspec, pl.BlockSpec((tq, N_IDX_HEADS), lambda i, j: (qrow(i, j), 0)), qspec,
                  pl.BlockSpec((None, IDX_DIM, t_eff), lambda i, j: (i, 0, 0)), kspec, kspec],
        out_specs=pl.BlockSpec((None, tq, ATT_W), lambda i, j: (i, j, 0)),
        out_shape=jax.ShapeDtypeStruct((b, n_blocks * tq, ATT_W), BF16),
        scratch_shapes=[pltpu.VMEM((tq, t_eff), F32), pltpu.VMEM((tq, t_eff), F32)],
        compiler_params=_cparams(("parallel", "arbitrary")),
        name="attn_prompt",
    )(qi, wi, q, kitb, ktb, vtb)


def _indexer_rows(qi, keys_t, wi_col):
    s = jnp.maximum(_dot(qi, keys_t), 0.0) * wi_col
    n = s.shape[1]
    return jnp.sum(s.reshape(N_IDX_HEADS, SUBLANES, n), axis=0) * IDX_SCALE


def _sample_scores_kernel(pt_ref, qi_ref, wi_ref, kin_ref, *refs, pages_per_step, page):
    del pt_ref
    page_refs, out_ref, new_ref = refs[:pages_per_step], refs[pages_per_step], refs[pages_per_step + 1]
    qi = qi_ref[...]
    wi_col = wi_ref[...]
    for i in range(pages_per_step):
        out_ref[:, i * page:(i + 1) * page] = _indexer_rows(qi, page_refs[i][...].astype(BF16), wi_col)

    @pl.when(pl.program_id(1) == 0)
    def _():
        new_ref[...] = _indexer_rows(qi, kin_ref[...], wi_col)


def _sample_scores(page_table, qi_s, wi_col, kitb_new_pad, cache_kidx_t, layer, pages_per_step):
    bsz = qi_s.shape[0]
    n_pages = page_table.shape[0] // bsz
    page = cache_kidx_t.shape[-1]
    steps = n_pages // pages_per_step

    def page_spec(i):
        return pl.BlockSpec((None, None, IDX_DIM, page),
                            lambda b, g, pt: (layer, pt[b * n_pages + g * pages_per_step + i], 0, 0))

    grid_spec = pltpu.PrefetchScalarGridSpec(
        num_scalar_prefetch=1,
        grid=(bsz, steps),
        in_specs=[pl.BlockSpec((None, N_IDX_HEADS * SUBLANES, IDX_DIM), lambda b, g, pt: (b, 0, 0)),
                  pl.BlockSpec((None, N_IDX_HEADS * SUBLANES, 1), lambda b, g, pt: (b, 0, 0)),
                  pl.BlockSpec((None, IDX_DIM, LANES), lambda b, g, pt: (b, 0, 0))]
                 + [page_spec(i) for i in range(pages_per_step)],
        out_specs=[pl.BlockSpec((SUBLANES, pages_per_step * page), lambda b, g, pt: (b, g)),
                   pl.BlockSpec((SUBLANES, LANES), lambda b, g, pt: (b, 0))],
    )
    return pl.pallas_call(
        functools.partial(_sample_scores_kernel, pages_per_step=pages_per_step, page=page),
        grid_spec=grid_spec,
        out_shape=[jax.ShapeDtypeStruct((bsz * SUBLANES, n_pages * page), F32),
                   jax.ShapeDtypeStruct((bsz * SUBLANES, LANES), F32)],
        compiler_params=_cparams(("parallel", "arbitrary")),
        name="sample_scores",
    )(page_table, qi_s, wi_col, kitb_new_pad, *([cache_kidx_t] * pages_per_step))


def _sample_select_kernel(sp_ref, sn_ref, selp_ref, seln_ref, *, rows, t_new, past, topk):
    jn = lax.broadcasted_iota(jnp.int32, (rows, LANES), 1)
    tn = lax.broadcasted_iota(jnp.int32, (rows, LANES), 0) & (SUBLANES - 1)
    adm_n = (jn <= tn) & (jn < t_new)
    sn = jnp.where(adm_n, sn_ref[...], NEG_INF)

    half = rows // 2
    tau = jnp.concatenate(
        _kth_largest([lambda cand: _count(sp_ref[0:half, :] >= cand) + _count(sn[0:half] >= cand),
                      lambda cand: _count(sp_ref[half:rows, :] >= cand) + _count(sn[half:rows] >= cand)],
                     half, topk), axis=0)
    sp = sp_ref[...]
    need = topk - (_count(sp > tau) + _count(sn > tau))
    eqn = (sn == tau) & adm_n
    excess = jnp.max(_count(sp == tau) + _count(eqn) - need) > 0

    @pl.when(jnp.logical_not(excess))
    def _():
        selp_ref[...] = jnp.where(sp >= tau, 1.0, 0.0)
        seln_ref[...] = jnp.where((sn >= tau) & adm_n, 1.0, 0.0)

    @pl.when(excess)
    def _():
        needf = need.astype(F32)
        n_chunks = past // LANES
        chunks = [(i, sp_ref[:, i * LANES:(i + 1) * LANES] == tau) for i in range(n_chunks)]
        chunks.append((n_chunks, eqn))
        for i, rank in _tie_rank(chunks, rows):
            if i < n_chunks:
                kc = sp_ref[:, i * LANES:(i + 1) * LANES]
                sel = (kc > tau) | ((kc == tau) & (rank < needf))
                selp_ref[:, i * LANES:(i + 1) * LANES] = jnp.where(sel, 1.0, 0.0)
            else:
                sel = ((sn > tau) | ((sn == tau) & (rank < needf))) & adm_n
                seln_ref[...] = jnp.where(sel, 1.0, 0.0)


def _sample_select(scores_past, scores_new, t_new, rows):
    n, past = scores_past.shape
    assert past >= IDX_TOPK_MAX, "every query must see at least top-k admissible keys"
    assert t_new == SUBLANES and n % rows == 0
    topk = min(IDX_TOPK_MAX, (past + t_new) // 4)
    blk = lambda w: pl.BlockSpec((rows, w), lambda i: (i, 0))
    return pl.pallas_call(
        functools.partial(_sample_select_kernel, rows=rows, t_new=t_new, past=past, topk=topk),
        grid=(n // rows,),
        in_specs=[blk(past), blk(LANES)],
        out_specs=[blk(past), blk(LANES)],
        out_shape=[jax.ShapeDtypeStruct((n, past), F32), jax.ShapeDtypeStruct((n, LANES), F32)],
        compiler_params=_cparams(("parallel",)),
        name="sample_select",
    )(scores_past, scores_new)


def _sample_attend_kernel(pt_ref, q_ref, selp_ref, seln_ref, *refs, n_groups, pages_per_step, page):
    del pt_ref
    g = pages_per_step
    k_refs, v_refs = refs[:g], refs[g:2 * g]
    kn_ref, vn_ref, o_ref, m_ref, l_ref, acc_ref = refs[2 * g:]
    step = pl.program_id(1)
    rows = N_HEADS * SUBLANES

    @pl.when(step == 0)
    def _():
        m_ref[...] = jnp.full((rows, 1), NEG_INF, F32)
        l_ref[...] = jnp.zeros((rows, 1), F32)
        acc_ref[...] = jnp.zeros((rows, ATT_W), F32)

    qrow = lax.broadcasted_iota(jnp.int32, (rows, ATT_W), 0)
    qcol = lax.broadcasted_iota(jnp.int32, (rows, ATT_W), 1)
    q_bd = jnp.where((qcol >> 6) == (qrow >> 3), jnp.concatenate([q_ref[...]] * N_HEADS, axis=1), 0.0).astype(BF16)

    def process(k_pages, v_pages, sel):
        logit = jnp.concatenate([_dot(q_bd, kp.reshape(ATT_W, page).astype(BF16)) for kp in k_pages], axis=1)
        valid = jnp.concatenate([sel] * N_HEADS, axis=0) > 0.5
        logit = jnp.where(valid, logit, NEG_INF)
        m_old = m_ref[...]
        m_new = jnp.maximum(m_old, jnp.max(logit, axis=1, keepdims=True))
        m_safe = jnp.where(m_new == NEG_INF, 0.0, m_new)
        pr = jnp.exp(logit - m_safe)
        alpha = jnp.exp(m_old - m_safe)
        l_ref[...] = alpha * l_ref[...] + jnp.sum(pr, axis=1, keepdims=True)
        prb = pr.astype(BF16)
        pv = jnp.zeros((rows, ATT_W), F32)
        for i, vp in enumerate(v_pages):
            pv = pv + _dot_nt(prb[:, i * page:(i + 1) * page], vp.reshape(ATT_W, page).astype(BF16))
        acc_ref[...] = alpha * acc_ref[...] + pv
        m_ref[...] = m_new

    @pl.when(step < n_groups)
    def _():
        process([r[...] for r in k_refs], [r[...] for r in v_refs], selp_ref[...])

    @pl.when(step == n_groups)
    def _():
        process([kn_ref[...]], [vn_ref[...]], seln_ref[...])
        out = acc_ref[...] / l_ref[...]
        for hd in range(N_HEADS):
            o_ref[:, hd * HEAD_DIM:(hd + 1) * HEAD_DIM] = out[hd * SUBLANES:(hd + 1) * SUBLANES,
                                                              hd * HEAD_DIM:(hd + 1) * HEAD_DIM]


def _sample_attend(page_table, q_s, selp, seln, cache_kt, cache_vt, kt_new_pad, vt_new_pad, layer, pages_per_step):
    bsz = q_s.shape[0]
    n_pages = page_table.shape[0] // bsz
    page = cache_kt.shape[-1]
    g = pages_per_step
    n_groups = n_pages // g
    last = n_groups - 1

    def page_spec(i):
        return pl.BlockSpec(
            (None, None, N_HEADS, HEAD_DIM, page),
            lambda b, s, pt: (layer, pt[b * n_pages + jnp.minimum(s, last) * g + i], 0, 0, 0))

    new_spec = pl.BlockSpec((None, N_HEADS, HEAD_DIM, page), lambda b, s, pt: (b, 0, 0, 0))
    rows = N_HEADS * SUBLANES
    grid_spec = pltpu.PrefetchScalarGridSpec(
        num_scalar_prefetch=1,
        grid=(bsz, n_groups + 1),
        in_specs=[pl.BlockSpec((None, rows, HEAD_DIM), lambda b, s, pt: (b, 0, 0)),
                  pl.BlockSpec((None, SUBLANES, g * page), lambda b, s, pt: (b, 0, jnp.minimum(s, last))),
                  pl.BlockSpec((None, SUBLANES, LANES), lambda b, s, pt: (b, 0, 0))]
                 + [page_spec(i) for i in range(g)] + [page_spec(i) for i in range(g)]
                 + [new_spec, new_spec],
        out_specs=pl.BlockSpec((None, SUBLANES, ATT_W), lambda b, s, pt: (b, 0, 0)),
        scratch_shapes=[pltpu.VMEM((rows, 1), F32), pltpu.VMEM((rows, 1), F32), pltpu.VMEM((rows, ATT_W), F32)],
    )
    return pl.pallas_call(
        functools.partial(_sample_attend_kernel, n_groups=n_groups, pages_per_step=g, page=page),
        grid_spec=grid_spec,
        out_shape=jax.ShapeDtypeStruct((bsz, SUBLANES, ATT_W), F32),
        compiler_params=_cparams(("parallel", "arbitrary")),
        name="sample_attend",
    )(page_table, q_s, selp, seln, *([cache_kt] * g), *([cache_vt] * g), kt_new_pad, vt_new_pad)


def _mix_kernel(x_ref, c_ref, o_ref, g0_ref, b0_ref, wg_ref, bg_ref, wco_ref, wao_ref, wo_ref,
                g1_ref, b1_ref, wrh_ref, wrl_ref, br_ref, x1_ref, comb_ref, *, alpha):
    h = _layer_norm(x_ref[...], g0_ref[...], b0_ref[...])
    hb = h.astype(BF16)
    gc = _sigmoid(_dot(hb, wg_ref[:, :D_MODEL]) + bg_ref[:, :D_MODEL])
    ga = _sigmoid(_dot(hb, wg_ref[:, D_MODEL:]) + bg_ref[:, D_MODEL:])
    merged = gc * _dot(c_ref[...], wco_ref[...]) + ga * _dot(o_ref[...], wao_ref[...])
    mix = _dot(merged.astype(BF16), wo_ref[...])
    x1 = _layer_norm(alpha * h + mix, g1_ref[...], b1_ref[...])
    x1_ref[...] = x1

    xh = x1.astype(BF16)
    xl = (x1 - xh.astype(F32)).astype(BF16)
    logits = _dot(xh, wrh_ref[...]) + (_dot(xl, wrh_ref[...]) + _dot(xh, wrl_ref[...])) + br_ref[...]
    lane = lax.broadcasted_iota(jnp.int32, logits.shape, 1)
    is_grp = (lane >= N_EXPERTS) & (lane < N_EXPERTS + N_GROUPS)
    gl = jnp.where(is_grp, logits, NEG_INF)
    gmax = jnp.max(gl, axis=1, keepdims=True)
    grp = jnp.min(jnp.where(gl == gmax, lane, 4 * LANES), axis=1, keepdims=True) - N_EXPERTS
    p_grp = 1.0 / jnp.sum(jnp.exp(gl - gmax), axis=1, keepdims=True)
    in_grp = (lane < N_EXPERTS) & ((lane >> 3) == grp)
    el = jnp.where(in_grp, logits, NEG_INF)
    emax = jnp.max(el, axis=1, keepdims=True)
    ee = jnp.exp(el - emax)
    pe = ee / jnp.sum(ee, axis=1, keepdims=True)
    pe = jnp.where(in_grp, pe, -1.0)
    p1 = jnp.max(pe, axis=1, keepdims=True)
    i1 = jnp.min(jnp.where(pe == p1, lane, 4 * LANES), axis=1, keepdims=True)
    pe2 = jnp.where(lane == i1, -1.0, pe)
    p2 = jnp.max(pe2, axis=1, keepdims=True)
    i2 = jnp.min(jnp.where(pe2 == p2, lane, 4 * LANES), axis=1, keepdims=True)
    tot = p1 + p2
    comb_ref[...] = (jnp.where(lane == i1, p_grp * (p1 / tot), 0.0)
                     + jnp.where(lane == i2, p_grp * (p2 / tot), 0.0)
                     + jnp.where(lane == GROUP_LANE, grp.astype(F32), 0.0))


def _mix(x, c, o, g0, b0, wg, bg, wco, wao, wo, g1, b1, wrh, wrl, br, alpha, tm):
    n = x.shape[0]
    row = lambda i: (i, 0)
    full = lambda i: (0, 0)
    fs = lambda a: pl.BlockSpec(a.shape, full)
    return pl.pallas_call(
        functools.partial(_mix_kernel, alpha=alpha),
        grid=(n // tm,),
        in_specs=[pl.BlockSpec((tm, D_MODEL), row), pl.BlockSpec((tm, CONV_CH), row),
                  pl.BlockSpec((tm, ATT_W), row)] + [fs(a) for a in (g0, b0, wg, bg, wco, wao, wo, g1, b1, wrh, wrl, br)],
        out_specs=[pl.BlockSpec((tm, D_MODEL), row), pl.BlockSpec((tm, LANES), row)],
        out_shape=[jax.ShapeDtypeStruct((n, D_MODEL), F32), jax.ShapeDtypeStruct((n, LANES), F32)],
        compiler_params=_cparams(("parallel",)),
        name="mix",
    )(x, c, o, g0, b0, wg, bg, wco, wao, wo, g1, b1, wrh, wrl, br)


def _split3(a):
    hi = a.astype(BF16)
    r1 = a - hi.astype(F32)
    mid = r1.astype(BF16)
    lo = (r1 - mid.astype(F32)).astype(BF16)
    return hi, mid, lo


def _permute_rows(perm, a):
    hi, mid, lo = _split3(a)
    return _dot(perm, hi) + _dot(perm, mid) + _dot(perm, lo)


def _ffn_kernel(x1_ref, comb_ref, p_ref, wpg_ref, wpp_ref, weg_ref, weu_ref, wed_ref, g2_ref, b2_ref,
                y_ref, to_slot_ref, to_token_ref, xs_ref, cs_ref, acc_ref, meta_ref, *, alpha, tm, rb, eps):
    s = pl.program_id(1)

    @pl.when(s == 0)
    def _():
        comb = comb_ref[...]
        lane = lax.broadcasted_iota(jnp.int32, (tm, LANES), 1)
        lane_row = lax.broadcasted_iota(jnp.int32, (1, LANES), 1)
        grp = comb[:, GROUP_LANE:GROUP_LANE + 1]
        onehot = jnp.where((lane.astype(F32) == grp) & (lane < N_GROUPS), 1.0, 0.0)
        r = lax.broadcasted_iota(jnp.int32, (tm, tm), 0)
        c = lax.broadcasted_iota(jnp.int32, (tm, tm), 1)
        earlier = jnp.where(c < r, 1.0, 0.0).astype(BF16)
        before = _dot(earlier, onehot.astype(BF16))
        cnt = jnp.sum(onehot, axis=0, keepdims=True)
        base = jnp.zeros((1, LANES), F32)
        run = jnp.zeros((1, 1), F32)
        for g in range(N_GROUPS):
            base = base + jnp.where(lane_row == g, run, 0.0)
            meta_ref[g] = run[0, 0].astype(jnp.int32)
            meta_ref[N_GROUPS + g] = cnt[0, g].astype(jnp.int32)
            run = run + cnt[:, g:g + 1]
        slot = jnp.sum(onehot * (before + base), axis=1, keepdims=True)
        slot_row = jnp.transpose(jnp.broadcast_to(slot, (tm, LANES)))[0:1, :]
        to_slot_ref[...] = jnp.where(slot == c.astype(F32), 1.0, 0.0).astype(BF16)
        to_token = jnp.where(slot_row == r.astype(F32), 1.0, 0.0).astype(BF16)
        to_token_ref[...] = to_token
        xs_ref[...] = _dot(to_token, x1_ref[...].astype(BF16)).astype(BF16)
        cs_ref[...] = _permute_rows(to_token, comb)
        acc_ref[...] = jnp.zeros((tm, D_MODEL), F32)

    g = (s * eps) // EXPERTS_PER_GROUP
    lo = meta_ref[g]
    hi = lo + meta_ref[N_GROUPS + g]
    for blk in range(tm // rb):
        @pl.when((lo < (blk + 1) * rb) & (hi > blk * rb))
        def _():
            rows = slice(blk * rb, (blk + 1) * rb)
            xb = xs_ref[rows, :]
            cs = cs_ref[rows, :]
            lane = lax.broadcasted_iota(jnp.int32, (rb, LANES), 1)
            out = jnp.zeros((rb, D_MODEL), F32)
            for j in range(eps):
                ce = jnp.sum(jnp.where(lane == s * eps + j, cs, 0.0), axis=1, keepdims=True)
                hg = _dot(xb, weg_ref[j])
                hu = _dot(xb, weu_ref[j])
                hidden = (hg * _sigmoid(hg)) * hu * ce
                out = out + _dot(hidden.astype(BF16), wed_ref[j])
            acc_ref[rows, :] += out

    @pl.when(s == N_EXPERTS // eps - 1)
    def _():
        x1 = x1_ref[...]
        ffn = _permute_rows(to_slot_ref[...], acc_ref[...])
        ple = _sigmoid(_dot(x1.astype(BF16), wpg_ref[...])) * _dot(p_ref[...].astype(BF16), wpp_ref[...])
        y_ref[...] = _layer_norm(alpha * x1 + ffn + ple, g2_ref[...], b2_ref[...])


def _ffn(x1, comb, p, wpg, wpp, weg, weu, wed, g2, b2, alpha, tm, rb, eps):
    n = x1.shape[0]
    assert tm % rb == 0 and EXPERTS_PER_GROUP % eps == 0
    row = lambda i, e: (i, 0)
    full = lambda i, e: (0, 0)
    exp = lambda i, e: (e, 0, 0)
    return pl.pallas_call(
        functools.partial(_ffn_kernel, alpha=alpha, tm=tm, rb=rb, eps=eps),
        grid=(n // tm, N_EXPERTS // eps),
        in_specs=[pl.BlockSpec((tm, D_MODEL), row), pl.BlockSpec((tm, LANES), row),
                  pl.BlockSpec((tm, PLE_DIM), row),
                  pl.BlockSpec((D_MODEL, D_MODEL), full), pl.BlockSpec((PLE_DIM, D_MODEL), full),
                  pl.BlockSpec((eps, D_MODEL, D_EXPERT), exp), pl.BlockSpec((eps, D_MODEL, D_EXPERT), exp),
                  pl.BlockSpec((eps, D_EXPERT, D_MODEL), exp),
                  pl.BlockSpec((1, D_MODEL), full), pl.BlockSpec((1, D_MODEL), full)],
        out_specs=pl.BlockSpec((tm, D_MODEL), row),
        out_shape=jax.ShapeDtypeStruct((n, D_MODEL), F32),
        scratch_shapes=[pltpu.VMEM((tm, tm), BF16), pltpu.VMEM((tm, tm), BF16),
                        pltpu.VMEM((tm, D_MODEL), BF16), pltpu.VMEM((tm, LANES), F32),
                        pltpu.VMEM((tm, D_MODEL), F32), pltpu.SMEM((2 * N_GROUPS,), jnp.int32)],
        compiler_params=pltpu.CompilerParams(dimension_semantics=("parallel", "arbitrary"),
                                             vmem_limit_bytes=VMEM_LIMIT_FFN),
        name="ffn",
    )(x1, comb, p, wpg, wpp, weg, weu, wed, g2, b2)


def _tile(n, target):
    t = min(n, target)
    assert n % t == 0, (n, t)
    return t


def _row(v):
    return v.reshape(1, -1).astype(F32)


def kernel(x_prompt, x_sample, cache_k, cache_v, cache_kidx, state_conv, page_table, p_prompt, p_sample,
           ln0_g, ln0_b, w_in, b_gate, conv_w, conv_b, lnc_g, lnc_b, w_conv_out, w_attn_out, w_o,
           ln1_g, ln1_b, w_rg, b_rg, w_re, b_re, w_eg, w_eu, w_ed, w_pg, w_pp, ln2_g, ln2_b):
    depth = w_in.shape[0]
    assert depth == 1, "single-layer step"
    assert w_in.shape[1:] == (D_MODEL, COL_END)
    layer = 0
    alpha = (2.0 * depth) ** 0.25
    bp, tp, _ = x_prompt.shape
    bs, ts, _ = x_sample.shape
    assert ts == SUBLANES, "sample step length must fill one sublane tile"
    page = cache_k.shape[2]
    n_pages = page_table.shape[1]
    assert page == LANES

    g0, b0 = _row(ln0_g), _row(ln0_b)
    w = w_in[layer]
    wi_cols = jnp.pad(w[:, COL_WI:COL_GC], ((0, 0), (0, LANES - N_IDX_HEADS)))
    ws = jnp.concatenate([w[:, COL_GLU_A:COL_K], w[:, COL_QI:COL_KI], wi_cols], axis=1).astype(BF16)
    wt = w[:, COL_K:COL_QI].T
    wt = jnp.concatenate([wt, w[:, COL_KI:COL_WI].T], axis=0).astype(BF16)
    wg = w[:, COL_GC:].astype(BF16)
    bg = _row(b_gate[layer])
    wr = jnp.pad(jnp.concatenate([w_re[layer], w_rg[layer]], axis=1), ((0, 0), (0, LANES - N_EXPERTS - N_GROUPS)))
    wrh = wr.astype(BF16)
    wrl = (wr - wrh.astype(F32)).astype(BF16)
    br = jnp.pad(jnp.concatenate([b_re[layer], b_rg[layer]]), (0, LANES - N_EXPERTS - N_GROUPS)).reshape(1, LANES)
    wco, wao, wo = w_conv_out[layer].astype(BF16), w_attn_out[layer].astype(BF16), w_o[layer].astype(BF16)
    wpg, wpp = w_pg[layer].astype(BF16), w_pp[layer].astype(BF16)
    weg, weu, wed = w_eg[layer].astype(BF16), w_eu[layer].astype(BF16), w_ed[layer].astype(BF16)
    cw, cb = conv_w[layer], _row(conv_b[layer])
    lcg, lcb = _row(lnc_g[layer]), _row(lnc_b[layer])
    g1, b1, g2, b2 = _row(ln1_g[layer]), _row(ln1_b[layer]), _row(ln2_g[layer]), _row(ln2_b[layer])
    cache_kt = jnp.transpose(cache_k, (0, 1, 3, 4, 2))
    cache_vt = jnp.transpose(cache_v, (0, 1, 3, 4, 2))
    cache_kidx_t = jnp.transpose(cache_kidx, (0, 1, 3, 2))

    def tail(x1, comb, p):
        n = x1.shape[0]
        tm = _tile(n, 1024)
        return _ffn(x1, comb, p.reshape(n, PLE_DIM), wpg, wpp, weg, weu, wed, g2, b2, alpha, tm, _tile(tm, 256), 4)

    def mix(x, c, o):
        n = x.shape[0]
        return _mix(x, c, o, g0, b0, wg, bg, wco, wao, wo, g1, b1, wrh, wrl, br, alpha, _tile(n, 512))

    np_ = bp * tp
    xp = x_prompt.reshape(np_, D_MODEL)
    glu, q, qi, wi, kt, vt, ktb, vtb, kit, kitb = _proj(xp, g0, b0, ws, wt, bp, tp, _tile(tp, 512))
    c, newconv_p = _conv(glu.reshape(bp, tp, CONV_CH), None, cw, cb, lcg, lcb, _tile(tp, 256))
    tq = _tile(tp, 256)
    o = jnp.concatenate(
        [_attn_prompt(qi, wi, q, kitb, ktb, vtb, bp, tp, tq, i, 1, (i + 1) * tq) for i in range(tp // tq)],
        axis=1).reshape(np_, ATT_W)
    x1, comb = mix(xp, c.reshape(np_, CONV_CH), o)
    y_prompt = tail(x1, comb, p_prompt[layer]).reshape(bp, tp, D_MODEL)
    new_k_p = jnp.transpose(kt, (0, 3, 1, 2))[None]
    new_v_p = jnp.transpose(vt, (0, 3, 1, 2))[None]
    new_ki_p = jnp.transpose(kit, (0, 2, 1))[None]

    ns = bs * ts
    xs = x_sample.reshape(ns, D_MODEL)
    glu, q, qi, wi, kt, vt, _, _, kit, kitb = _proj(xs, g0, b0, ws, wt, 1, ns, ns)
    c, newconv_s = _conv(glu.reshape(bs, ts, CONV_CH), state_conv[layer], cw, cb, lcg, lcb, ts)
    to_rows = lambda a: a.reshape(N_HEADS, bs, ts, HEAD_DIM).transpose(1, 0, 2, 3).reshape(bs, N_HEADS * ts, HEAD_DIM)
    qi_s, q_s = to_rows(qi), to_rows(q)
    wi_col = wi.reshape(bs, ts, N_IDX_HEADS).transpose(0, 2, 1).reshape(bs, N_IDX_HEADS * ts, 1)
    kt_s = kt.reshape(N_HEADS, HEAD_DIM, bs, ts)
    vt_s = vt.reshape(N_HEADS, HEAD_DIM, bs, ts)
    kit_s = kit.reshape(IDX_DIM, bs, ts)
    pad_keys = lambda a: jnp.pad(a, ((0, 0),) * (a.ndim - 1) + ((0, page - ts),))
    pt_flat = page_table.reshape(-1).astype(jnp.int32)
    pages_per = lambda target: max(d for d in range(1, target + 1) if n_pages % d == 0)
    scores_past, scores_new = _sample_scores(
        pt_flat, qi_s, wi_col, pad_keys(kitb.reshape(IDX_DIM, bs, ts).transpose(1, 0, 2)),
        cache_kidx_t, layer, pages_per(32))
    selp, seln = _sample_select(scores_past, scores_new, ts, _tile(ns, 64))
    o = _sample_attend(pt_flat, q_s, selp.reshape(bs, ts, -1), seln.reshape(bs, ts, LANES), cache_kt, cache_vt,
                       pad_keys(kt_s.transpose(2, 0, 1, 3)), pad_keys(vt_s.transpose(2, 0, 1, 3)),
                       layer, pages_per(16))
    x1, comb = mix(xs, c.reshape(ns, CONV_CH), o.reshape(ns, ATT_W).astype(BF16))
    y_sample = tail(x1, comb, p_sample[layer]).reshape(bs, ts, D_MODEL)
    new_k_s = kt_s.transpose(2, 3, 0, 1)[None]
    new_v_s = vt_s.transpose(2, 3, 0, 1)[None]
    new_ki_s = kit_s.transpose(1, 2, 0)[None]

    return (y_prompt, y_sample, new_k_p, new_v_p, new_ki_p, newconv_p[None],
            new_k_s, new_v_s, new_ki_s, newconv_s[None])
```

```python
import functools

import jax
import jax.numpy as jnp
from jax import lax
from jax.experimental import pallas as pl
from jax.experimental.pallas import tpu as pltpu

D_MODEL = 1024
CONV_CH = 512
CONV_WIDTH = 31
CONV_HIST = CONV_WIDTH - 1
N_HEADS = 8
HEAD_DIM = 64
ATT_W = N_HEADS * HEAD_DIM
N_IDX_HEADS = 8
IDX_DIM = 64
IDX_TOPK_MAX = 256
N_GROUPS = 4
EXPERTS_PER_GROUP = 8
N_EXPERTS = N_GROUPS * EXPERTS_PER_GROUP
D_EXPERT = 256
PLE_DIM = 256
LN_EPS = 1e-5
IDX_SCALE = (N_IDX_HEADS * IDX_DIM) ** -0.5
ATT_SCALE = HEAD_DIM ** -0.5
COL_GLU_A, COL_GLU_B, COL_Q, COL_K, COL_V, COL_QI, COL_KI, COL_WI, COL_GC = (
    0, 512, 1024, 1536, 2048, 2560, 3072, 3136, 3144)
COL_END = COL_GC + 2 * D_MODEL

LANES = 128
SUBLANES = 8
PREFIX_ROWS = 32
GROUP_LANE = LANES - 1
GROUP_ALIGN = 16
VMEM_LIMIT = 48 * 1024 * 1024
VMEM_LIMIT_FFN = 60 * 1024 * 1024

F32 = jnp.float32
BF16 = jnp.bfloat16
NEG_INF = float("-inf")
INT_MIN = -2 ** 31


def _cparams(sem):
    return pltpu.CompilerParams(dimension_semantics=sem, vmem_limit_bytes=VMEM_LIMIT)


def _layer_norm(x, g, b):
    mu = jnp.mean(x, axis=-1, keepdims=True)
    xc = x - mu
    var = jnp.mean(xc * xc, axis=-1, keepdims=True)
    return xc * lax.rsqrt(var + LN_EPS) * g + b


def _sigmoid(x):
    return 1.0 / (1.0 + jnp.exp(-x))


def _dot(a, b):
    return jnp.dot(a, b, preferred_element_type=F32)


def _dot_nt(a, b):
    return lax.dot_general(a, b, (((1,), (1,)), ((), ())), preferred_element_type=F32)


def _proj_kernel(x_ref, g0_ref, b0_ref, ws_ref, wt_ref,
                 glu_ref, q_ref, qi_ref, wi_ref, kt_ref, vt_ref, ktb_ref, vtb_ref, kit_ref, kitb_ref):
    h = _layer_norm(x_ref[...], g0_ref[...], b0_ref[...])
    hb = h.astype(BF16)

    def seg(col, width=512):
        return _dot(hb, ws_ref[:, col:col + width])

    glu_ref[...] = seg(0) * _sigmoid(seg(512))
    q = seg(1024) * ATT_SCALE
    qi = seg(1536)
    for hd in range(N_HEADS):
        sl = slice(hd * HEAD_DIM, (hd + 1) * HEAD_DIM)
        q_ref[hd] = q[:, sl].astype(BF16)
        qi_ref[hd] = qi[:, sl].astype(BF16)
    wi_ref[...] = seg(2048, LANES)[:, :N_IDX_HEADS]

    tm = hb.shape[0]
    kt = _dot_nt(wt_ref[0:ATT_W, :], hb).reshape(N_HEADS, HEAD_DIM, tm)
    kt_ref[...] = kt
    ktb_ref[...] = kt.astype(BF16)
    vt = _dot_nt(wt_ref[ATT_W:2 * ATT_W, :], hb).reshape(N_HEADS, HEAD_DIM, tm)
    vt_ref[...] = vt
    vtb_ref[...] = vt.astype(BF16)
    kit = _dot_nt(wt_ref[2 * ATT_W:2 * ATT_W + IDX_DIM, :], hb)
    kit_ref[...] = kit
    kitb_ref[...] = kit.astype(BF16)


def _proj(x, g0, b0, ws, wt, b, t, tm):
    n = b * t
    nt = t // tm
    row = lambda i: (i, 0)
    head = lambda i: (0, i, 0)
    full = lambda i: (0, 0)
    hm = jax.ShapeDtypeStruct((N_HEADS, n, HEAD_DIM), BF16)
    hm_spec = pl.BlockSpec((N_HEADS, tm, HEAD_DIM), head)
    kv_spec = pl.BlockSpec((None, N_HEADS, HEAD_DIM, tm), lambda i: (i // nt, 0, 0, i % nt))
    ki_spec = pl.BlockSpec((None, IDX_DIM, tm), lambda i: (i // nt, 0, i % nt))
    kv_shape = lambda dt: jax.ShapeDtypeStruct((b, N_HEADS, HEAD_DIM, t), dt)
    ki_shape = lambda dt: jax.ShapeDtypeStruct((b, IDX_DIM, t), dt)
    return pl.pallas_call(
        _proj_kernel,
        grid=(n // tm,),
        in_specs=[pl.BlockSpec((tm, D_MODEL), row),
                  pl.BlockSpec((1, D_MODEL), full), pl.BlockSpec((1, D_MODEL), full),
                  pl.BlockSpec(ws.shape, full), pl.BlockSpec(wt.shape, full)],
        out_specs=[pl.BlockSpec((tm, CONV_CH), row), hm_spec, hm_spec,
                   pl.BlockSpec((tm, N_IDX_HEADS), row),
                   kv_spec, kv_spec, kv_spec, kv_spec, ki_spec, ki_spec],
        out_shape=[jax.ShapeDtypeStruct((n, CONV_CH), F32), hm, hm,
                   jax.ShapeDtypeStruct((n, N_IDX_HEADS), F32),
                   kv_shape(F32), kv_shape(F32), kv_shape(BF16), kv_shape(BF16), ki_shape(F32), ki_shape(BF16)],
        compiler_params=_cparams(("parallel",)),
        name="proj",
    )(x, g0, b0, ws, wt)


def _conv_kernel(*refs, tc, nt, has_state):
    it = iter(refs)
    glu_ref = next(it)
    prev_ref = next(it) if nt > 1 else None
    state_ref = next(it) if has_state else None
    w_ref, cb_ref, g_ref, b_ref, c_ref, newconv_ref, buf_ref, shift_ref, acc_ref = it
    t = pl.program_id(1)

    @pl.when(t == 0)
    def _():
        buf_ref[0:PREFIX_ROWS, :] = jnp.zeros((PREFIX_ROWS, CONV_CH), F32)
        if has_state:
            buf_ref[2:PREFIX_ROWS, :] = state_ref[...]

    if nt > 1:
        @pl.when(t > 0)
        def _():
            buf_ref[0:PREFIX_ROWS, :] = prev_ref[...]

    buf_ref[PREFIX_ROWS:PREFIX_ROWS + tc, :] = glu_ref[...]

    shifted_rows = tc + PREFIX_ROWS - SUBLANES
    for s in range(1, SUBLANES):
        shift_ref[s - 1] = buf_ref[s:s + shifted_rows, :]

    rb = min(tc, 32)
    for c0 in range(0, CONV_CH, LANES):
        cs = slice(c0, c0 + LANES)
        bias = cb_ref[:, cs]
        for r0 in range(0, tc, rb):
            acc = jnp.zeros((rb, LANES), F32) + bias
            for j in range(CONV_WIDTH):
                a, s = divmod(2 + j, SUBLANES)
                lo = r0 + a * SUBLANES
                rows = buf_ref[lo:lo + rb, cs] if s == 0 else shift_ref[s - 1, lo:lo + rb, cs]
                acc = acc + w_ref[j:j + 1, cs] * rows
            acc_ref[r0:r0 + rb, cs] = acc

    y = _layer_norm(acc_ref[...], g_ref[...], b_ref[...])
    c_ref[...] = (y * _sigmoid(y)).astype(BF16)

    @pl.when(t == nt - 1)
    def _():
        newconv_ref[...] = buf_ref[tc + 2:tc + PREFIX_ROWS, :]


def _conv(glu, state, conv_w, conv_b, lnc_g, lnc_b, tc):
    b, t, _ = glu.shape
    nt = t // tc
    blocks_per_tile = tc // PREFIX_ROWS
    in_specs = [pl.BlockSpec((None, tc, CONV_CH), lambda i, j: (i, j, 0))]
    args = [glu]
    if nt > 1:
        in_specs.append(pl.BlockSpec((None, PREFIX_ROWS, CONV_CH),
                                     lambda i, j: (i, jnp.maximum(j * blocks_per_tile - 1, 0), 0)))
        args.append(glu)
    if state is not None:
        in_specs.append(pl.BlockSpec((None, CONV_HIST, CONV_CH), lambda i, j: (i, 0, 0)))
        args.append(state)
    full = lambda i, j: (0, 0)
    in_specs += [pl.BlockSpec((CONV_WIDTH, CONV_CH), full), pl.BlockSpec((1, CONV_CH), full),
                 pl.BlockSpec((1, CONV_CH), full), pl.BlockSpec((1, CONV_CH), full)]
    args += [conv_w, conv_b, lnc_g, lnc_b]
    return pl.pallas_call(
        functools.partial(_conv_kernel, tc=tc, nt=nt, has_state=state is not None),
        grid=(b, nt),
        in_specs=in_specs,
        out_specs=[pl.BlockSpec((None, tc, CONV_CH), lambda i, j: (i, j, 0)),
                   pl.BlockSpec((None, CONV_HIST, CONV_CH), lambda i, j: (i, 0, 0))],
        out_shape=[jax.ShapeDtypeStruct((b, t, CONV_CH), BF16),
                   jax.ShapeDtypeStruct((b, CONV_HIST, CONV_CH), F32)],
        scratch_shapes=[pltpu.VMEM((PREFIX_ROWS + tc, CONV_CH), F32),
                        pltpu.VMEM((SUBLANES - 1, tc + PREFIX_ROWS - SUBLANES, CONV_CH), F32),
                        pltpu.VMEM((tc, CONV_CH), F32)],
        compiler_params=_cparams(("parallel", "arbitrary")),
        name="conv",
    )(*args)


def _ordered_to_f32(c):
    return pltpu.bitcast(c ^ ((c >> 31) & jnp.int32(0x7FFFFFFF)), F32)


def _kth_largest(count_ge_parts, rows, topk):
    def body(i, taus):
        bit = lax.shift_left(jnp.int32(1), jnp.int32(31) - i)
        out = []
        for count_ge, tau in zip(count_ge_parts, taus):
            cand = tau ^ bit
            out.append(jnp.where(count_ge(_ordered_to_f32(cand)) >= topk, cand, tau))
        return tuple(out)
    init = tuple(jnp.full((rows, 1), INT_MIN, jnp.int32) for _ in count_ge_parts)
    return [_ordered_to_f32(t) for t in lax.fori_loop(0, 32, body, init, unroll=2)]


def _count(mask):
    return jnp.sum(mask.astype(jnp.int32), axis=1, keepdims=True)


def _tie_rank(eq_chunks, rows):
    r = lax.broadcasted_iota(jnp.int32, (LANES, LANES), 0)
    c = lax.broadcasted_iota(jnp.int32, (LANES, LANES), 1)
    upper = jnp.where(r < c, 1.0, 0.0).astype(BF16)
    prefix = jnp.zeros((rows, 1), F32)
    for idx, eq in eq_chunks:
        eqf = jnp.where(eq, 1.0, 0.0)
        yield idx, _dot(eqf.astype(BF16), upper) + prefix
        prefix = prefix + jnp.sum(eqf, axis=1, keepdims=True)


def _attn_prompt_kernel(qi_ref, wi_ref, q_ref, kitb_ref, ktb_ref, vtb_ref, o_ref, sc_ref, bias_ref,
                        *, tq, t_eff, topk, first_block):
    qb = first_block + pl.program_id(1)
    wi = wi_ref[...]
    kitb = kitb_ref[...]
    acc = jnp.zeros((tq, t_eff), F32)
    for hd in range(N_IDX_HEADS):
        acc = acc + jnp.maximum(_dot(qi_ref[hd], kitb), 0.0) * wi[:, hd:hd + 1]
    q_pos = qb * tq + lax.broadcasted_iota(jnp.int32, (tq, 1), 0)
    k_pos = lax.broadcasted_iota(jnp.int32, (1, t_eff), 1)
    adm = k_pos <= q_pos
    sc_ref[...] = jnp.where(adm, acc * IDX_SCALE, NEG_INF)

    half = tq // 2
    tau = jnp.concatenate(
        _kth_largest([lambda cand: _count(sc_ref[0:half, :] >= cand),
                      lambda cand: _count(sc_ref[half:tq, :] >= cand)], half, topk), axis=0)
    take_all = (q_pos + 1) <= topk
    sc = sc_ref[...]
    eq = (sc == tau) & adm
    need = topk - _count(sc > tau)
    excess = jnp.max(jnp.where(take_all, 0, _count(eq) - need)) > 0

    @pl.when(jnp.logical_not(excess))
    def _():
        bias_ref[...] = jnp.where(adm & ((sc >= tau) | take_all), 0.0, NEG_INF)

    @pl.when(excess)
    def _():
        needf = need.astype(F32)
        chunks = [(i, (sc_ref[:, i * LANES:(i + 1) * LANES] == tau) & adm[:, i * LANES:(i + 1) * LANES])
                  for i in range(t_eff // LANES)]
        for i, rank in _tie_rank(chunks, tq):
            cs = slice(i * LANES, (i + 1) * LANES)
            kc = sc_ref[:, cs]
            sel = ((kc > tau) | ((kc == tau) & (rank < needf)) | take_all) & adm[:, cs]
            bias_ref[:, cs] = jnp.where(sel, 0.0, NEG_INF)

    bias = bias_ref[...]
    for hd in range(N_HEADS):
        logit = _dot(q_ref[hd], ktb_ref[hd]) + bias
        m = jnp.max(logit, axis=1, keepdims=True)
        p = jnp.exp(logit - m)
        denom = jnp.sum(p, axis=1, keepdims=True)
        o = _dot_nt(p.astype(BF16), vtb_ref[hd]) / denom
        o_ref[:, hd * HEAD_DIM:(hd + 1) * HEAD_DIM] = o.astype(BF16)


def _attn_prompt(qi, wi, q, kitb, ktb, vtb, b, t, tq, first_block, n_blocks, t_eff):
    nq = t // tq
    topk = min(IDX_TOPK_MAX, t // 4)
    assert (first_block + n_blocks) * tq <= t_eff
    qrow = lambda i, j: i * nq + first_block + j
    qspec = pl.BlockSpec((N_HEADS, tq, HEAD_DIM), lambda i, j: (0, qrow(i, j), 0))
    kspec = pl.BlockSpec((None, N_HEADS, HEAD_DIM, t_eff), lambda i, j: (i, 0, 0, 0))
    return pl.pallas_call(
        functools.partial(_attn_prompt_kernel, tq=tq, t_eff=t_eff, topk=topk, first_block=first_block),
        grid=(b, n_blocks),
        in_specs=[qspec, pl.BlockSpec((tq, N_IDX_HEADS), lambda i, j: (qrow(i, j), 0)), qspec,
                  pl.BlockSpec((None, IDX_DIM, t_eff), lambda i, j: (i, 0, 0)), kspec, kspec],
        out_specs=pl.BlockSpec((None, tq, ATT_W), lambda i, j: (i, j, 0)),
        out_shape=jax.ShapeDtypeStruct((b, n_blocks * tq, ATT_W), BF16),
        scratch_shapes=[pltpu.VMEM((tq, t_eff), F32), pltpu.VMEM((tq, t_eff), F32)],
        compiler_params=_cparams(("parallel", "arbitrary")),
        name="attn_prompt",
    )(qi, wi, q, kitb, ktb, vtb)


def _indexer_rows(qi, keys_t, wi_col):
    s = jnp.maximum(_dot(qi, keys_t), 0.0) * wi_col
    n = s.shape[1]
    return jnp.sum(s.reshape(N_IDX_HEADS, SUBLANES, n), axis=0) * IDX_SCALE


def _sample_scores_kernel(pt_ref, qi_ref, wi_ref, kin_ref, *refs, pages_per_step, page):
    del pt_ref
    page_refs, out_ref, new_ref = refs[:pages_per_step], refs[pages_per_step], refs[pages_per_step + 1]
    qi = qi_ref[...]
    wi_col = wi_ref[...]
    for i in range(pages_per_step):
        out_ref[:, i * page:(i + 1) * page] = _indexer_rows(qi, page_refs[i][...].astype(BF16), wi_col)

    @pl.when(pl.program_id(1) == 0)
    def _():
        new_ref[...] = _indexer_rows(qi, kin_ref[...], wi_col)


def _sample_scores(page_table, qi_s, wi_col, kitb_new_pad, cache_kidx_t, layer, pages_per_step):
    bsz = qi_s.shape[0]
    n_pages = page_table.shape[0] // bsz
    page = cache_kidx_t.shape[-1]
    steps = n_pages // pages_per_step

    def page_spec(i):
        return pl.BlockSpec((None, None, IDX_DIM, page),
                            lambda b, g, pt: (layer, pt[b * n_pages + g * pages_per_step + i], 0, 0))

    grid_spec = pltpu.PrefetchScalarGridSpec(
        num_scalar_prefetch=1,
        grid=(bsz, steps),
        in_specs=[pl.BlockSpec((None, N_IDX_HEADS * SUBLANES, IDX_DIM), lambda b, g, pt: (b, 0, 0)),
                  pl.BlockSpec((None, N_IDX_HEADS * SUBLANES, 1), lambda b, g, pt: (b, 0, 0)),
                  pl.BlockSpec((None, IDX_DIM, LANES), lambda b, g, pt: (b, 0, 0))]
                 + [page_spec(i) for i in range(pages_per_step)],
        out_specs=[pl.BlockSpec((SUBLANES, pages_per_step * page), lambda b, g, pt: (b, g)),
                   pl.BlockSpec((SUBLANES, LANES), lambda b, g, pt: (b, 0))],
    )
    return pl.pallas_call(
        functools.partial(_sample_scores_kernel, pages_per_step=pages_per_step, page=page),
        grid_spec=grid_spec,
        out_shape=[jax.ShapeDtypeStruct((bsz * SUBLANES, n_pages * page), F32),
                   jax.ShapeDtypeStruct((bsz * SUBLANES, LANES), F32)],
        compiler_params=_cparams(("parallel", "arbitrary")),
        name="sample_scores",
    )(page_table, qi_s, wi_col, kitb_new_pad, *([cache_kidx_t] * pages_per_step))


def _sample_select_kernel(sp_ref, sn_ref, selp_ref, seln_ref, *, rows, t_new, past, topk):
    jn = lax.broadcasted_iota(jnp.int32, (rows, LANES), 1)
    tn = lax.broadcasted_iota(jnp.int32, (rows, LANES), 0) & (SUBLANES - 1)
    adm_n = (jn <= tn) & (jn < t_new)
    sn = jnp.where(adm_n, sn_ref[...], NEG_INF)

    half = rows // 2
    tau = jnp.concatenate(
        _kth_largest([lambda cand: _count(sp_ref[0:half, :] >= cand) + _count(sn[0:half] >= cand),
                      lambda cand: _count(sp_ref[half:rows, :] >= cand) + _count(sn[half:rows] >= cand)],
                     half, topk), axis=0)
    sp = sp_ref[...]
    need = topk - (_count(sp > tau) + _count(sn > tau))
    eqn = (sn == tau) & adm_n
    excess = jnp.max(_count(sp == tau) + _count(eqn) - need) > 0

    @pl.when(jnp.logical_not(excess))
    def _():
        selp_ref[...] = jnp.where(sp >= tau, 1.0, 0.0)
        seln_ref[...] = jnp.where((sn >= tau) & adm_n, 1.0, 0.0)

    @pl.when(excess)
    def _():
        needf = need.astype(F32)
        n_chunks = past // LANES
        chunks = [(i, sp_ref[:, i * LANES:(i + 1) * LANES] == tau) for i in range(n_chunks)]
        chunks.append((n_chunks, eqn))
        for i, rank in _tie_rank(chunks, rows):
            if i < n_chunks:
                kc = sp_ref[:, i * LANES:(i + 1) * LANES]
                sel = (kc > tau) | ((kc == tau) & (rank < needf))
                selp_ref[:, i * LANES:(i + 1) * LANES] = jnp.where(sel, 1.0, 0.0)
            else:
                sel = ((sn > tau) | ((sn == tau) & (rank < needf))) & adm_n
                seln_ref[...] = jnp.where(sel, 1.0, 0.0)


def _sample_select(scores_past, scores_new, t_new, rows):
    n, past = scores_past.shape
    assert past >= IDX_TOPK_MAX, "every query must see at least top-k admissible keys"
    assert t_new == SUBLANES and n % rows == 0
    topk = min(IDX_TOPK_MAX, (past + t_new) // 4)
    blk = lambda w: pl.BlockSpec((rows, w), lambda i: (i, 0))
    return pl.pallas_call(
        functools.partial(_sample_select_kernel, rows=rows, t_new=t_new, past=past, topk=topk),
        grid=(n // rows,),
        in_specs=[blk(past), blk(LANES)],
        out_specs=[blk(past), blk(LANES)],
        out_shape=[jax.ShapeDtypeStruct((n, past), F32), jax.ShapeDtypeStruct((n, LANES), F32)],
        compiler_params=_cparams(("parallel",)),
        name="sample_select",
    )(scores_past, scores_new)


def _sample_attend_kernel(pt_ref, q_ref, selp_ref, seln_ref, *refs, n_groups, pages_per_step, page):
    del pt_ref
    g = pages_per_step
    k_refs, v_refs = refs[:g], refs[g:2 * g]
    kn_ref, vn_ref, o_ref, m_ref, l_ref, acc_ref = refs[2 * g:]
    step = pl.program_id(1)
    rows = N_HEADS * SUBLANES

    @pl.when(step == 0)
    def _():
        m_ref[...] = jnp.full((rows, 1), NEG_INF, F32)
        l_ref[...] = jnp.zeros((rows, 1), F32)
        acc_ref[...] = jnp.zeros((rows, ATT_W), F32)

    qrow = lax.broadcasted_iota(jnp.int32, (rows, ATT_W), 0)
    qcol = lax.broadcasted_iota(jnp.int32, (rows, ATT_W), 1)
    q_bd = jnp.where((qcol >> 6) == (qrow >> 3), jnp.concatenate([q_ref[...]] * N_HEADS, axis=1), 0.0).astype(BF16)

    def process(k_pages, v_pages, sel):
        logit = jnp.concatenate([_dot(q_bd, kp.reshape(ATT_W, page).astype(BF16)) for kp in k_pages], axis=1)
        valid = jnp.concatenate([sel] * N_HEADS, axis=0) > 0.5
        logit = jnp.where(valid, logit, NEG_INF)
        m_old = m_ref[...]
        m_new = jnp.maximum(m_old, jnp.max(logit, axis=1, keepdims=True))
        m_safe = jnp.where(m_new == NEG_INF, 0.0, m_new)
        pr = jnp.exp(logit - m_safe)
        alpha = jnp.exp(m_old - m_safe)
        l_ref[...] = alpha * l_ref[...] + jnp.sum(pr, axis=1, keepdims=True)
        prb = pr.astype(BF16)
        pv = jnp.zeros((rows, ATT_W), F32)
        for i, vp in enumerate(v_pages):
            pv = pv + _dot_nt(prb[:, i * page:(i + 1) * page], vp.reshape(ATT_W, page).astype(BF16))
        acc_ref[...] = alpha * acc_ref[...] + pv
        m_ref[...] = m_new

    process([r[...] for r in k_refs], [r[...] for r in v_refs], selp_ref[...])

    @pl.when(step == n_groups - 1)
    def _():
        process([kn_ref[...]], [vn_ref[...]], seln_ref[...])
        out = acc_ref[...] / l_ref[...]
        for hd in range(N_HEADS):
            o_ref[:, hd * HEAD_DIM:(hd + 1) * HEAD_DIM] = out[hd * SUBLANES:(hd + 1) * SUBLANES,
                                                              hd * HEAD_DIM:(hd + 1) * HEAD_DIM]


def _sample_attend(page_table, q_s, selp, seln, cache_kt, cache_vt, kt_new_pad, vt_new_pad, layer, pages_per_step):
    bsz = q_s.shape[0]
    n_pages = page_table.shape[0] // bsz
    page = cache_kt.shape[-1]
    g = pages_per_step
    n_groups = n_pages // g

    def page_spec(i):
        return pl.BlockSpec(
            (None, None, N_HEADS, HEAD_DIM, page),
            lambda b, s, pt: (layer, pt[b * n_pages + s * g + i], 0, 0, 0))

    new_spec = pl.BlockSpec((None, N_HEADS, HEAD_DIM, page), lambda b, s, pt: (b, 0, 0, 0))
    rows = N_HEADS * SUBLANES
    grid_spec = pltpu.PrefetchScalarGridSpec(
        num_scalar_prefetch=1,
        grid=(bsz, n_groups),
        in_specs=[pl.BlockSpec((None, rows, HEAD_DIM), lambda b, s, pt: (b, 0, 0)),
                  pl.BlockSpec((None, SUBLANES, g * page), lambda b, s, pt: (b, 0, s)),
                  pl.BlockSpec((None, SUBLANES, LANES), lambda b, s, pt: (b, 0, 0))]
                 + [page_spec(i) for i in range(g)] + [page_spec(i) for i in range(g)]
                 + [new_spec, new_spec],
        out_specs=pl.BlockSpec((None, SUBLANES, ATT_W), lambda b, s, pt: (b, 0, 0)),
        scratch_shapes=[pltpu.VMEM((rows, 1), F32), pltpu.VMEM((rows, 1), F32), pltpu.VMEM((rows, ATT_W), F32)],
    )
    return pl.pallas_call(
        functools.partial(_sample_attend_kernel, n_groups=n_groups, pages_per_step=g, page=page),
        grid_spec=grid_spec,
        out_shape=jax.ShapeDtypeStruct((bsz, SUBLANES, ATT_W), F32),
        compiler_params=_cparams(("parallel", "arbitrary")),
        name="sample_attend",
    )(page_table, q_s, selp, seln, *([cache_kt] * g), *([cache_vt] * g), kt_new_pad, vt_new_pad)


def _mix_kernel(x_ref, c_ref, o_ref, g0_ref, b0_ref, wg_ref, bg_ref, wco_ref, wao_ref, wo_ref,
                g1_ref, b1_ref, wrh_ref, wrl_ref, br_ref, x1_ref, comb_ref, *, alpha):
    h = _layer_norm(x_ref[...], g0_ref[...], b0_ref[...])
    hb = h.astype(BF16)
    gc = _sigmoid(_dot(hb, wg_ref[:, :D_MODEL]) + bg_ref[:, :D_MODEL])
    ga = _sigmoid(_dot(hb, wg_ref[:, D_MODEL:]) + bg_ref[:, D_MODEL:])
    merged = gc * _dot(c_ref[...], wco_ref[...]) + ga * _dot(o_ref[...], wao_ref[...])
    mix = _dot(merged.astype(BF16), wo_ref[...])
    x1 = _layer_norm(alpha * h + mix, g1_ref[...], b1_ref[...])
    x1_ref[...] = x1

    xh = x1.astype(BF16)
    xl = (x1 - xh.astype(F32)).astype(BF16)
    logits = _dot(xh, wrh_ref[...]) + (_dot(xl, wrh_ref[...]) + _dot(xh, wrl_ref[...])) + br_ref[...]
    lane = lax.broadcasted_iota(jnp.int32, logits.shape, 1)
    is_grp = (lane >= N_EXPERTS) & (lane < N_EXPERTS + N_GROUPS)
    gl = jnp.where(is_grp, logits, NEG_INF)
    gmax = jnp.max(gl, axis=1, keepdims=True)
    grp = jnp.min(jnp.where(gl == gmax, lane, 4 * LANES), axis=1, keepdims=True) - N_EXPERTS
    p_grp = 1.0 / jnp.sum(jnp.exp(gl - gmax), axis=1, keepdims=True)
    in_grp = (lane < N_EXPERTS) & ((lane >> 3) == grp)
    el = jnp.where(in_grp, logits, NEG_INF)
    emax = jnp.max(el, axis=1, keepdims=True)
    ee = jnp.exp(el - emax)
    pe = ee / jnp.sum(ee, axis=1, keepdims=True)
    pe = jnp.where(in_grp, pe, -1.0)
    p1 = jnp.max(pe, axis=1, keepdims=True)
    i1 = jnp.min(jnp.where(pe == p1, lane, 4 * LANES), axis=1, keepdims=True)
    pe2 = jnp.where(lane == i1, -1.0, pe)
    p2 = jnp.max(pe2, axis=1, keepdims=True)
    i2 = jnp.min(jnp.where(pe2 == p2, lane, 4 * LANES), axis=1, keepdims=True)
    tot = p1 + p2
    comb_ref[...] = (jnp.where(lane == i1, p_grp * (p1 / tot), 0.0)
                     + jnp.where(lane == i2, p_grp * (p2 / tot), 0.0)
                     + jnp.where(lane == GROUP_LANE, grp.astype(F32), 0.0))


def _mix(x, c, o, g0, b0, wg, bg, wco, wao, wo, g1, b1, wrh, wrl, br, alpha, tm):
    n = x.shape[0]
    row = lambda i: (i, 0)
    full = lambda i: (0, 0)
    fs = lambda a: pl.BlockSpec(a.shape, full)
    return pl.pallas_call(
        functools.partial(_mix_kernel, alpha=alpha),
        grid=(n // tm,),
        in_specs=[pl.BlockSpec((tm, D_MODEL), row), pl.BlockSpec((tm, CONV_CH), row),
                  pl.BlockSpec((tm, ATT_W), row)] + [fs(a) for a in (g0, b0, wg, bg, wco, wao, wo, g1, b1, wrh, wrl, br)],
        out_specs=[pl.BlockSpec((tm, D_MODEL), row), pl.BlockSpec((tm, LANES), row)],
        out_shape=[jax.ShapeDtypeStruct((n, D_MODEL), F32), jax.ShapeDtypeStruct((n, LANES), F32)],
        compiler_params=_cparams(("parallel",)),
        name="mix",
    )(x, c, o, g0, b0, wg, bg, wco, wao, wo, g1, b1, wrh, wrl, br)


def _split3(a):
    hi = a.astype(BF16)
    r1 = a - hi.astype(F32)
    mid = r1.astype(BF16)
    lo = (r1 - mid.astype(F32)).astype(BF16)
    return hi, mid, lo


def _permute_rows(perm, a):
    hi, mid, lo = _split3(a)
    return _dot(perm, hi) + _dot(perm, mid) + _dot(perm, lo)


def _ffn_kernel(x1_ref, comb_ref, p_ref, wpg_ref, wpp_ref, weg_ref, weu_ref, wed_ref, g2_ref, b2_ref,
                y_ref, to_slot_ref, to_token_ref, xs_ref, cs_ref, acc_ref, meta_ref,
                *, alpha, tm, rb, eps, slots):
    s = pl.program_id(1)

    @pl.when(s == 0)
    def _():
        comb = comb_ref[...]
        lane = lax.broadcasted_iota(jnp.int32, (tm, LANES), 1)
        lane_row = lax.broadcasted_iota(jnp.int32, (1, LANES), 1)
        grp = comb[:, GROUP_LANE:GROUP_LANE + 1]
        onehot = jnp.where((lane.astype(F32) == grp) & (lane < N_GROUPS), 1.0, 0.0)
        r = lax.broadcasted_iota(jnp.int32, (tm, tm), 0)
        c = lax.broadcasted_iota(jnp.int32, (tm, tm), 1)
        earlier = jnp.where(c < r, 1.0, 0.0).astype(BF16)
        before = _dot(earlier, onehot.astype(BF16))
        cnt = jnp.sum(onehot, axis=0, keepdims=True)
        base = jnp.zeros((1, LANES), F32)
        run = jnp.zeros((1, 1), F32)
        for g in range(N_GROUPS):
            base = base + jnp.where(lane_row == g, run, 0.0)
            meta_ref[g] = run[0, 0].astype(jnp.int32)
            meta_ref[N_GROUPS + g] = cnt[0, g].astype(jnp.int32)
            run = jnp.floor((run + cnt[:, g:g + 1] + (GROUP_ALIGN - 1.0)) * (1.0 / GROUP_ALIGN)) * GROUP_ALIGN
        slot = jnp.sum(onehot * (before + base), axis=1, keepdims=True)
        slot_row = jnp.transpose(jnp.broadcast_to(slot, (tm, LANES)))[0:1, :]
        slot_c = lax.broadcasted_iota(jnp.int32, (tm, slots), 1).astype(F32)
        slot_r = lax.broadcasted_iota(jnp.int32, (slots, tm), 0).astype(F32)
        to_slot_ref[...] = jnp.where(slot == slot_c, 1.0, 0.0).astype(BF16)
        to_token = jnp.where(slot_row == slot_r, 1.0, 0.0).astype(BF16)
        to_token_ref[...] = to_token
        xs_ref[0:slots, :] = _dot(to_token, x1_ref[...].astype(BF16)).astype(BF16)
        cs_ref[0:slots, :] = _permute_rows(to_token, comb)
        xs_ref[slots:, :] = jnp.zeros((rb, D_MODEL), BF16)
        cs_ref[slots:, :] = jnp.zeros((rb, LANES), F32)
        acc_ref[...] = jnp.zeros((slots + rb, D_MODEL), F32)

    g = (s * eps) // EXPERTS_PER_GROUP
    first = meta_ref[g]
    count = meta_ref[N_GROUPS + g]
    for blk in range(-(-tm // rb)):
        @pl.when(blk * rb < count)
        def _():
            rows = pl.ds(pl.multiple_of(first + blk * rb, GROUP_ALIGN), rb)
            xb = xs_ref[rows, :]
            cs = cs_ref[rows, :]
            lane = lax.broadcasted_iota(jnp.int32, (rb, LANES), 1)
            out = jnp.zeros((rb, D_MODEL), F32)
            for j in range(eps):
                ce = jnp.sum(jnp.where(lane == s * eps + j, cs, 0.0), axis=1, keepdims=True)
                hg = _dot(xb, weg_ref[j])
                hu = _dot(xb, weu_ref[j])
                hidden = (hg * _sigmoid(hg)) * hu * ce
                out = out + _dot(hidden.astype(BF16), wed_ref[j])
            acc_ref[rows, :] += out

    @pl.when(s == N_EXPERTS // eps - 1)
    def _():
        x1 = x1_ref[...]
        ffn = _permute_rows(to_slot_ref[...], acc_ref[0:slots, :])
        ple = _sigmoid(_dot(x1.astype(BF16), wpg_ref[...])) * _dot(p_ref[...].astype(BF16), wpp_ref[...])
        y_ref[...] = _layer_norm(alpha * x1 + ffn + ple, g2_ref[...], b2_ref[...])


def _ffn(x1, comb, p, wpg, wpp, weg, weu, wed, g2, b2, alpha, tm, rb, eps):
    n = x1.shape[0]
    assert rb % GROUP_ALIGN == 0 and EXPERTS_PER_GROUP % eps == 0
    slots = -(-(tm + N_GROUPS * GROUP_ALIGN) // LANES) * LANES
    row = lambda i, e: (i, 0)
    full = lambda i, e: (0, 0)
    exp = lambda i, e: (e, 0, 0)
    return pl.pallas_call(
        functools.partial(_ffn_kernel, alpha=alpha, tm=tm, rb=rb, eps=eps, slots=slots),
        grid=(n // tm, N_EXPERTS // eps),
        in_specs=[pl.BlockSpec((tm, D_MODEL), row), pl.BlockSpec((tm, LANES), row),
                  pl.BlockSpec((tm, PLE_DIM), row),
                  pl.BlockSpec((D_MODEL, D_MODEL), full), pl.BlockSpec((PLE_DIM, D_MODEL), full),
                  pl.BlockSpec((eps, D_MODEL, D_EXPERT), exp), pl.BlockSpec((eps, D_MODEL, D_EXPERT), exp),
                  pl.BlockSpec((eps, D_EXPERT, D_MODEL), exp),
                  pl.BlockSpec((1, D_MODEL), full), pl.BlockSpec((1, D_MODEL), full)],
        out_specs=pl.BlockSpec((tm, D_MODEL), row),
        out_shape=jax.ShapeDtypeStruct((n, D_MODEL), F32),
        scratch_shapes=[pltpu.VMEM((tm, slots), BF16), pltpu.VMEM((slots, tm), BF16),
                        pltpu.VMEM((slots + rb, D_MODEL), BF16), pltpu.VMEM((slots + rb, LANES), F32),
                        pltpu.VMEM((slots + rb, D_MODEL), F32), pltpu.SMEM((2 * N_GROUPS,), jnp.int32)],
        compiler_params=pltpu.CompilerParams(dimension_semantics=("parallel", "arbitrary"),
                                             vmem_limit_bytes=VMEM_LIMIT_FFN),
        name="ffn",
    )(x1, comb, p, wpg, wpp, weg, weu, wed, g2, b2)


def _tile(n, target):
    t = min(n, target)
    assert n % t == 0, (n, t)
    return t


def _row(v):
    return v.reshape(1, -1).astype(F32)


def kernel(x_prompt, x_sample, cache_k, cache_v, cache_kidx, state_conv, page_table, p_prompt, p_sample,
           ln0_g, ln0_b, w_in, b_gate, conv_w, conv_b, lnc_g, lnc_b, w_conv_out, w_attn_out, w_o,
           ln1_g, ln1_b, w_rg, b_rg, w_re, b_re, w_eg, w_eu, w_ed, w_pg, w_pp, ln2_g, ln2_b):
    depth = w_in.shape[0]
    assert depth == 1, "single-layer step"
    assert w_in.shape[1:] == (D_MODEL, COL_END)
    layer = 0
    alpha = (2.0 * depth) ** 0.25
    bp, tp, _ = x_prompt.shape
    bs, ts, _ = x_sample.shape
    assert ts == SUBLANES, "sample step length must fill one sublane tile"
    page = cache_k.shape[2]
    n_pages = page_table.shape[1]
    assert page == LANES

    g0, b0 = _row(ln0_g), _row(ln0_b)
    w = w_in[layer]
    wi_cols = jnp.pad(w[:, COL_WI:COL_GC], ((0, 0), (0, LANES - N_IDX_HEADS)))
    ws = jnp.concatenate([w[:, COL_GLU_A:COL_K], w[:, COL_QI:COL_KI], wi_cols], axis=1).astype(BF16)
    wt = w[:, COL_K:COL_QI].T
    wt = jnp.concatenate([wt, w[:, COL_KI:COL_WI].T], axis=0).astype(BF16)
    wg = w[:, COL_GC:].astype(BF16)
    bg = _row(b_gate[layer])
    wr = jnp.pad(jnp.concatenate([w_re[layer], w_rg[layer]], axis=1), ((0, 0), (0, LANES - N_EXPERTS - N_GROUPS)))
    wrh = wr.astype(BF16)
    wrl = (wr - wrh.astype(F32)).astype(BF16)
    br = jnp.pad(jnp.concatenate([b_re[layer], b_rg[layer]]), (0, LANES - N_EXPERTS - N_GROUPS)).reshape(1, LANES)
    wco, wao, wo = w_conv_out[layer].astype(BF16), w_attn_out[layer].astype(BF16), w_o[layer].astype(BF16)
    wpg, wpp = w_pg[layer].astype(BF16), w_pp[layer].astype(BF16)
    weg, weu, wed = w_eg[layer].astype(BF16), w_eu[layer].astype(BF16), w_ed[layer].astype(BF16)
    cw, cb = conv_w[layer], _row(conv_b[layer])
    lcg, lcb = _row(lnc_g[layer]), _row(lnc_b[layer])
    g1, b1, g2, b2 = _row(ln1_g[layer]), _row(ln1_b[layer]), _row(ln2_g[layer]), _row(ln2_b[layer])
    cache_kt = jnp.transpose(cache_k, (0, 1, 3, 4, 2))
    cache_vt = jnp.transpose(cache_v, (0, 1, 3, 4, 2))
    cache_kidx_t = jnp.transpose(cache_kidx, (0, 1, 3, 2))

    def tail(x1, comb, p):
        n = x1.shape[0]
        tm = _tile(n, 1024)
        rb = min(tm, tm // N_GROUPS + 4 * GROUP_ALIGN)
        return _ffn(x1, comb, p.reshape(n, PLE_DIM), wpg, wpp, weg, weu, wed, g2, b2, alpha, tm, rb, 4)

    def mix(x, c, o):
        n = x.shape[0]
        return _mix(x, c, o, g0, b0, wg, bg, wco, wao, wo, g1, b1, wrh, wrl, br, alpha, _tile(n, 512))

    np_ = bp * tp
    xp = x_prompt.reshape(np_, D_MODEL)
    glu, q, qi, wi, kt, vt, ktb, vtb, kit, kitb = _proj(xp, g0, b0, ws, wt, bp, tp, _tile(tp, 512))
    c, newconv_p = _conv(glu.reshape(bp, tp, CONV_CH), None, cw, cb, lcg, lcb, _tile(tp, 256))
    tq = _tile(tp, 256)
    o = jnp.concatenate(
        [_attn_prompt(qi, wi, q, kitb, ktb, vtb, bp, tp, tq, i, 1, (i + 1) * tq) for i in range(tp // tq)],
        axis=1).reshape(np_, ATT_W)
    x1, comb = mix(xp, c.reshape(np_, CONV_CH), o)
    y_prompt = tail(x1, comb, p_prompt[layer]).reshape(bp, tp, D_MODEL)
    new_k_p = jnp.transpose(kt, (0, 3, 1, 2))[None]
    new_v_p = jnp.transpose(vt, (0, 3, 1, 2))[None]
    new_ki_p = jnp.transpose(kit, (0, 2, 1))[None]

    ns = bs * ts
    xs = x_sample.reshape(ns, D_MODEL)
    glu, q, qi, wi, kt, vt, _, _, kit, kitb = _proj(xs, g0, b0, ws, wt, 1, ns, ns)
    c, newconv_s = _conv(glu.reshape(bs, ts, CONV_CH), state_conv[layer], cw, cb, lcg, lcb, ts)
    to_rows = lambda a: a.reshape(N_HEADS, bs, ts, HEAD_DIM).transpose(1, 0, 2, 3).reshape(bs, N_HEADS * ts, HEAD_DIM)
    qi_s, q_s = to_rows(qi), to_rows(q)
    wi_col = wi.reshape(bs, ts, N_IDX_HEADS).transpose(0, 2, 1).reshape(bs, N_IDX_HEADS * ts, 1)
    kt_s = kt.reshape(N_HEADS, HEAD_DIM, bs, ts)
    vt_s = vt.reshape(N_HEADS, HEAD_DIM, bs, ts)
    kit_s = kit.reshape(IDX_DIM, bs, ts)
    pad_keys = lambda a: jnp.pad(a, ((0, 0),) * (a.ndim - 1) + ((0, page - ts),))
    pt_flat = page_table.reshape(-1).astype(jnp.int32)
    pages_per = lambda target: max(d for d in range(1, target + 1) if n_pages % d == 0)
    scores_past, scores_new = _sample_scores(
        pt_flat, qi_s, wi_col, pad_keys(kitb.reshape(IDX_DIM, bs, ts).transpose(1, 0, 2)),
        cache_kidx_t, layer, pages_per(64))
    selp, seln = _sample_select(scores_past, scores_new, ts, _tile(ns, 64))
    o = _sample_attend(pt_flat, q_s, selp.reshape(bs, ts, -1), seln.reshape(bs, ts, LANES), cache_kt, cache_vt,
                       pad_keys(kt_s.transpose(2, 0, 1, 3)), pad_keys(vt_s.transpose(2, 0, 1, 3)),
                       layer, pages_per(16))
    x1, comb = mix(xs, c.reshape(ns, CONV_CH), o.reshape(ns, ATT_W).astype(BF16))
    y_sample = tail(x1, comb, p_sample[layer]).reshape(bs, ts, D_MODEL)
    new_k_s = kt_s.transpose(2, 3, 0, 1)[None]
    new_v_s = vt_s.transpose(2, 3, 0, 1)[None]
    new_ki_s = kit_s.transpose(1, 2, 0)[None]

    return (y_prompt, y_sample, new_k_p, new_v_p, new_ki_p, newconv_p[None],
            new_k_s, new_v_s, new_ki_s, newconv_s[None])
```

```python
import functools

import jax
import jax.numpy as jnp
from jax import lax
from jax.experimental import pallas as pl
from jax.experimental.pallas import tpu as pltpu

D_MODEL = 1024
CONV_CH = 512
CONV_WIDTH = 31
CONV_HIST = CONV_WIDTH - 1
N_HEADS = 8
HEAD_DIM = 64
ATT_W = N_HEADS * HEAD_DIM
N_IDX_HEADS = 8
IDX_DIM = 64
IDX_TOPK_MAX = 256
N_GROUPS = 4
EXPERTS_PER_GROUP = 8
N_EXPERTS = N_GROUPS * EXPERTS_PER_GROUP
D_EXPERT = 256
PLE_DIM = 256
LN_EPS = 1e-5
IDX_SCALE = (N_IDX_HEADS * IDX_DIM) ** -0.5
ATT_SCALE = HEAD_DIM ** -0.5
COL_GLU_A, COL_GLU_B, COL_Q, COL_K, COL_V, COL_QI, COL_KI, COL_WI, COL_GC = (
    0, 512, 1024, 1536, 2048, 2560, 3072, 3136, 3144)
COL_END = COL_GC + 2 * D_MODEL

LANES = 128
SUBLANES = 8
PREFIX_ROWS = 32
GROUP_LANE = LANES - 1
GROUP_ALIGN = 16
SEARCH_PARTS = 4
VMEM_LIMIT = 48 * 1024 * 1024
VMEM_LIMIT_FFN = 60 * 1024 * 1024

F32 = jnp.float32
BF16 = jnp.bfloat16
NEG_INF = float("-inf")
INT_MIN = -2 ** 31


def _cparams(sem):
    return pltpu.CompilerParams(dimension_semantics=sem, vmem_limit_bytes=VMEM_LIMIT)


def _layer_norm(x, g, b):
    mu = jnp.mean(x, axis=-1, keepdims=True)
    xc = x - mu
    var = jnp.mean(xc * xc, axis=-1, keepdims=True)
    return xc * lax.rsqrt(var + LN_EPS) * g + b


def _sigmoid(x):
    return 1.0 / (1.0 + jnp.exp(-x))


def _dot(a, b):
    return jnp.dot(a, b, preferred_element_type=F32)


def _dot_nt(a, b):
    return lax.dot_general(a, b, (((1,), (1,)), ((), ())), preferred_element_type=F32)


def _proj_kernel(x_ref, g0_ref, b0_ref, ws_ref, wt_ref,
                 glu_ref, q_ref, qi_ref, wi_ref, kt_ref, vt_ref, ktb_ref, vtb_ref, kit_ref, kitb_ref):
    h = _layer_norm(x_ref[...], g0_ref[...], b0_ref[...])
    hb = h.astype(BF16)

    def seg(col, width=512):
        return _dot(hb, ws_ref[:, col:col + width])

    glu_ref[...] = seg(0) * _sigmoid(seg(512))
    q = seg(1024) * ATT_SCALE
    qi = seg(1536)
    for hd in range(N_HEADS):
        sl = slice(hd * HEAD_DIM, (hd + 1) * HEAD_DIM)
        q_ref[hd] = q[:, sl].astype(BF16)
        qi_ref[hd] = qi[:, sl].astype(BF16)
    wi_ref[...] = seg(2048, LANES)[:, :N_IDX_HEADS]

    tm = hb.shape[0]
    kt = _dot_nt(wt_ref[0:ATT_W, :], hb).reshape(N_HEADS, HEAD_DIM, tm)
    kt_ref[...] = kt
    ktb_ref[...] = kt.astype(BF16)
    vt = _dot_nt(wt_ref[ATT_W:2 * ATT_W, :], hb).reshape(N_HEADS, HEAD_DIM, tm)
    vt_ref[...] = vt
    vtb_ref[...] = vt.astype(BF16)
    kit = _dot_nt(wt_ref[2 * ATT_W:2 * ATT_W + IDX_DIM, :], hb)
    kit_ref[...] = kit
    kitb_ref[...] = kit.astype(BF16)


def _proj(x, g0, b0, ws, wt, b, t, tm):
    n = b * t
    nt = t // tm
    row = lambda i: (i, 0)
    head = lambda i: (0, i, 0)
    full = lambda i: (0, 0)
    hm = jax.ShapeDtypeStruct((N_HEADS, n, HEAD_DIM), BF16)
    hm_spec = pl.BlockSpec((N_HEADS, tm, HEAD_DIM), head)
    kv_spec = pl.BlockSpec((None, N_HEADS, HEAD_DIM, tm), lambda i: (i // nt, 0, 0, i % nt))
    ki_spec = pl.BlockSpec((None, IDX_DIM, tm), lambda i: (i // nt, 0, i % nt))
    kv_shape = lambda dt: jax.ShapeDtypeStruct((b, N_HEADS, HEAD_DIM, t), dt)
    ki_shape = lambda dt: jax.ShapeDtypeStruct((b, IDX_DIM, t), dt)
    return pl.pallas_call(
        _proj_kernel,
        grid=(n // tm,),
        in_specs=[pl.BlockSpec((tm, D_MODEL), row),
                  pl.BlockSpec((1, D_MODEL), full), pl.BlockSpec((1, D_MODEL), full),
                  pl.BlockSpec(ws.shape, full), pl.BlockSpec(wt.shape, full)],
        out_specs=[pl.BlockSpec((tm, CONV_CH), row), hm_spec, hm_spec,
                   pl.BlockSpec((tm, N_IDX_HEADS), row),
                   kv_spec, kv_spec, kv_spec, kv_spec, ki_spec, ki_spec],
        out_shape=[jax.ShapeDtypeStruct((n, CONV_CH), F32), hm, hm,
                   jax.ShapeDtypeStruct((n, N_IDX_HEADS), F32),
                   kv_shape(F32), kv_shape(F32), kv_shape(BF16), kv_shape(BF16), ki_shape(F32), ki_shape(BF16)],
        compiler_params=_cparams(("parallel",)),
        name="proj",
    )(x, g0, b0, ws, wt)


def _conv_kernel(*refs, tc, nt, has_state):
    it = iter(refs)
    glu_ref = next(it)
    prev_ref = next(it) if nt > 1 else None
    state_ref = next(it) if has_state else None
    w_ref, cb_ref, g_ref, b_ref, c_ref, newconv_ref, buf_ref, shift_ref, acc_ref = it
    t = pl.program_id(1)

    @pl.when(t == 0)
    def _():
        buf_ref[0:PREFIX_ROWS, :] = jnp.zeros((PREFIX_ROWS, CONV_CH), F32)
        if has_state:
            buf_ref[2:PREFIX_ROWS, :] = state_ref[...]

    if nt > 1:
        @pl.when(t > 0)
        def _():
            buf_ref[0:PREFIX_ROWS, :] = prev_ref[...]

    buf_ref[PREFIX_ROWS:PREFIX_ROWS + tc, :] = glu_ref[...]

    shifted_rows = tc + PREFIX_ROWS - SUBLANES
    for s in range(1, SUBLANES):
        shift_ref[s - 1] = buf_ref[s:s + shifted_rows, :]

    rb = min(tc, 32)
    for c0 in range(0, CONV_CH, LANES):
        cs = slice(c0, c0 + LANES)
        bias = cb_ref[:, cs]
        for r0 in range(0, tc, rb):
            acc = jnp.zeros((rb, LANES), F32) + bias
            for j in range(CONV_WIDTH):
                a, s = divmod(2 + j, SUBLANES)
                lo = r0 + a * SUBLANES
                rows = buf_ref[lo:lo + rb, cs] if s == 0 else shift_ref[s - 1, lo:lo + rb, cs]
                acc = acc + w_ref[j:j + 1, cs] * rows
            acc_ref[r0:r0 + rb, cs] = acc

    y = _layer_norm(acc_ref[...], g_ref[...], b_ref[...])
    c_ref[...] = (y * _sigmoid(y)).astype(BF16)

    @pl.when(t == nt - 1)
    def _():
        newconv_ref[...] = buf_ref[tc + 2:tc + PREFIX_ROWS, :]


def _conv(glu, state, conv_w, conv_b, lnc_g, lnc_b, tc):
    b, t, _ = glu.shape
    nt = t // tc
    blocks_per_tile = tc // PREFIX_ROWS
    in_specs = [pl.BlockSpec((None, tc, CONV_CH), lambda i, j: (i, j, 0))]
    args = [glu]
    if nt > 1:
        in_specs.append(pl.BlockSpec((None, PREFIX_ROWS, CONV_CH),
                                     lambda i, j: (i, jnp.maximum(j * blocks_per_tile - 1, 0), 0)))
        args.append(glu)
    if state is not None:
        in_specs.append(pl.BlockSpec((None, CONV_HIST, CONV_CH), lambda i, j: (i, 0, 0)))
        args.append(state)
    full = lambda i, j: (0, 0)
    in_specs += [pl.BlockSpec((CONV_WIDTH, CONV_CH), full), pl.BlockSpec((1, CONV_CH), full),
                 pl.BlockSpec((1, CONV_CH), full), pl.BlockSpec((1, CONV_CH), full)]
    args += [conv_w, conv_b, lnc_g, lnc_b]
    return pl.pallas_call(
        functools.partial(_conv_kernel, tc=tc, nt=nt, has_state=state is not None),
        grid=(b, nt),
        in_specs=in_specs,
        out_specs=[pl.BlockSpec((None, tc, CONV_CH), lambda i, j: (i, j, 0)),
                   pl.BlockSpec((None, CONV_HIST, CONV_CH), lambda i, j: (i, 0, 0))],
        out_shape=[jax.ShapeDtypeStruct((b, t, CONV_CH), BF16),
                   jax.ShapeDtypeStruct((b, CONV_HIST, CONV_CH), F32)],
        scratch_shapes=[pltpu.VMEM((PREFIX_ROWS + tc, CONV_CH), F32),
                        pltpu.VMEM((SUBLANES - 1, tc + PREFIX_ROWS - SUBLANES, CONV_CH), F32),
                        pltpu.VMEM((tc, CONV_CH), F32)],
        compiler_params=_cparams(("parallel", "arbitrary")),
        name="conv",
    )(*args)


def _ordered_to_f32(c):
    return pltpu.bitcast(c ^ ((c >> 31) & jnp.int32(0x7FFFFFFF)), F32)


def _kth_largest(count_ge_parts, rows, topk):
    def body(i, taus):
        bit = lax.shift_left(jnp.int32(1), jnp.int32(31) - i)
        out = []
        for count_ge, tau in zip(count_ge_parts, taus):
            cand = tau ^ bit
            out.append(jnp.where(count_ge(_ordered_to_f32(cand)) >= topk, cand, tau))
        return tuple(out)
    init = tuple(jnp.full((rows, 1), INT_MIN, jnp.int32) for _ in count_ge_parts)
    return [_ordered_to_f32(t) for t in lax.fori_loop(0, 32, body, init, unroll=8)]


def _count(mask):
    return jnp.sum(mask.astype(jnp.int32), axis=1, keepdims=True)


def _tie_rank(eq_chunks, rows):
    r = lax.broadcasted_iota(jnp.int32, (LANES, LANES), 0)
    c = lax.broadcasted_iota(jnp.int32, (LANES, LANES), 1)
    upper = jnp.where(r < c, 1.0, 0.0).astype(BF16)
    prefix = jnp.zeros((rows, 1), F32)
    for idx, eq in eq_chunks:
        eqf = jnp.where(eq, 1.0, 0.0)
        yield idx, _dot(eqf.astype(BF16), upper) + prefix
        prefix = prefix + jnp.sum(eqf, axis=1, keepdims=True)


def _attn_prompt_kernel(qi_ref, wi_ref, q_ref, kitb_ref, ktb_ref, vtb_ref, o_ref, sc_ref, bias_ref,
                        *, tq, t_eff, topk, first_block):
    qb = first_block + pl.program_id(1)
    wi = wi_ref[...]
    kitb = kitb_ref[...]
    acc = jnp.zeros((tq, t_eff), F32)
    for hd in range(N_IDX_HEADS):
        acc = acc + jnp.maximum(_dot(qi_ref[hd], kitb), 0.0) * wi[:, hd:hd + 1]
    q_pos = qb * tq + lax.broadcasted_iota(jnp.int32, (tq, 1), 0)
    k_pos = lax.broadcasted_iota(jnp.int32, (1, t_eff), 1)
    adm = k_pos <= q_pos
    sc_ref[...] = jnp.where(adm, acc * IDX_SCALE, NEG_INF)

    part = tq // SEARCH_PARTS
    tau = jnp.concatenate(
        _kth_largest([functools.partial(lambda lo, cand: _count(sc_ref[lo:lo + part, :] >= cand), i * part)
                      for i in range(SEARCH_PARTS)], part, topk), axis=0)
    take_all = (q_pos + 1) <= topk
    sc = sc_ref[...]
    eq = (sc == tau) & adm
    need = topk - _count(sc > tau)
    excess = jnp.max(jnp.where(take_all, 0, _count(eq) - need)) > 0

    @pl.when(jnp.logical_not(excess))
    def _():
        bias_ref[...] = jnp.where(adm & ((sc >= tau) | take_all), 0.0, NEG_INF)

    @pl.when(excess)
    def _():
        needf = need.astype(F32)
        chunks = [(i, (sc_ref[:, i * LANES:(i + 1) * LANES] == tau) & adm[:, i * LANES:(i + 1) * LANES])
                  for i in range(t_eff // LANES)]
        for i, rank in _tie_rank(chunks, tq):
            cs = slice(i * LANES, (i + 1) * LANES)
            kc = sc_ref[:, cs]
            sel = ((kc > tau) | ((kc == tau) & (rank < needf)) | take_all) & adm[:, cs]
            bias_ref[:, cs] = jnp.where(sel, 0.0, NEG_INF)

    bias = bias_ref[...]
    for hd in range(N_HEADS):
        logit = _dot(q_ref[hd], ktb_ref[hd]) + bias
        m = jnp.max(logit, axis=1, keepdims=True)
        p = jnp.exp(logit - m)
        denom = jnp.sum(p, axis=1, keepdims=True)
        o = _dot_nt(p.astype(BF16), vtb_ref[hd]) / denom
        o_ref[:, hd * HEAD_DIM:(hd + 1) * HEAD_DIM] = o.astype(BF16)


def _attn_prompt(qi, wi, q, kitb, ktb, vtb, b, t, tq, first_block, n_blocks, t_eff):
    nq = t // tq
    topk = min(IDX_TOPK_MAX, t // 4)
    assert (first_block + n_blocks) * tq <= t_eff
    qrow = lambda i, j: i * nq + first_block + j
    qspec = pl.BlockSpec((N_HEADS, tq, HEAD_DIM), lambda i, j: (0, qrow(i, j), 0))
    kspec = pl.BlockSpec((None, N_HEADS, HEAD_DIM, t_eff), lambda i, j: (i, 0, 0, 0))
    return pl.pallas_call(
        functools.partial(_attn_prompt_kernel, tq=tq, t_eff=t_eff, topk=topk, first_block=first_block),
        grid=(b, n_blocks),
        in_specs=[qspec, pl.BlockSpec((tq, N_IDX_HEADS), lambda i, j: (qrow(i, j), 0)), qspec,
                  pl.BlockSpec((None, IDX_DIM, t_eff), lambda i, j: (i, 0, 0)), kspec, kspec],
        out_specs=pl.BlockSpec((None, tq, ATT_W), lambda i, j: (i, j, 0)),
        out_shape=jax.ShapeDtypeStruct((b, n_blocks * tq, ATT_W), BF16),
        scratch_shapes=[pltpu.VMEM((tq, t_eff), F32), pltpu.VMEM((tq, t_eff), F32)],
        compiler_params=_cparams(("parallel", "arbitrary")),
        name="attn_prompt",
    )(qi, wi, q, kitb, ktb, vtb)


def _indexer_rows(qi, keys_t, wi_col):
    s = jnp.maximum(_dot(qi, keys_t), 0.0) * wi_col
    n = s.shape[1]
    return jnp.sum(s.reshape(N_IDX_HEADS, SUBLANES, n), axis=0) * IDX_SCALE


def _sample_scores_kernel(pt_ref, qi_ref, wi_ref, kin_ref, *refs, pages_per_step, page):
    del pt_ref
    page_refs, out_ref, new_ref = refs[:pages_per_step], refs[pages_per_step], refs[pages_per_step + 1]
    qi = qi_ref[...]
    wi_col = wi_ref[...]
    for i in range(pages_per_step):
        out_ref[:, i * page:(i + 1) * page] = _indexer_rows(qi, page_refs[i][...].astype(BF16), wi_col)

    @pl.when(pl.program_id(1) == 0)
    def _():
        new_ref[...] = _indexer_rows(qi, kin_ref[...], wi_col)


def _sample_scores(page_table, qi_s, wi_col, kitb_new_pad, cache_kidx_t, layer, pages_per_step):
    bsz = qi_s.shape[0]
    n_pages = page_table.shape[0] // bsz
    page = cache_kidx_t.shape[-1]
    steps = n_pages // pages_per_step

    def page_spec(i):
        return pl.BlockSpec((None, None, IDX_DIM, page),
                            lambda b, g, pt: (layer, pt[b * n_pages + g * pages_per_step + i], 0, 0))

    grid_spec = pltpu.PrefetchScalarGridSpec(
        num_scalar_prefetch=1,
        grid=(bsz, steps),
        in_specs=[pl.BlockSpec((None, N_IDX_HEADS * SUBLANES, IDX_DIM), lambda b, g, pt: (b, 0, 0)),
                  pl.BlockSpec((None, N_IDX_HEADS * SUBLANES, 1), lambda b, g, pt: (b, 0, 0)),
                  pl.BlockSpec((None, IDX_DIM, LANES), lambda b, g, pt: (b, 0, 0))]
                 + [page_spec(i) for i in range(pages_per_step)],
        out_specs=[pl.BlockSpec((SUBLANES, pages_per_step * page), lambda b, g, pt: (b, g)),
                   pl.BlockSpec((SUBLANES, LANES), lambda b, g, pt: (b, 0))],
    )
    return pl.pallas_call(
        functools.partial(_sample_scores_kernel, pages_per_step=pages_per_step, page=page),
        grid_spec=grid_spec,
        out_shape=[jax.ShapeDtypeStruct((bsz * SUBLANES, n_pages * page), F32),
                   jax.ShapeDtypeStruct((bsz * SUBLANES, LANES), F32)],
        compiler_params=_cparams(("parallel", "arbitrary")),
        name="sample_scores",
    )(page_table, qi_s, wi_col, kitb_new_pad, *([cache_kidx_t] * pages_per_step))


def _sample_select_kernel(sp_ref, sn_ref, selp_ref, seln_ref, *, rows, t_new, past, topk):
    jn = lax.broadcasted_iota(jnp.int32, (rows, LANES), 1)
    tn = lax.broadcasted_iota(jnp.int32, (rows, LANES), 0) & (SUBLANES - 1)
    adm_n = (jn <= tn) & (jn < t_new)
    sn = jnp.where(adm_n, sn_ref[...], NEG_INF)

    half = rows // 2
    tau = jnp.concatenate(
        _kth_largest([lambda cand: _count(sp_ref[0:half, :] >= cand) + _count(sn[0:half] >= cand),
                      lambda cand: _count(sp_ref[half:rows, :] >= cand) + _count(sn[half:rows] >= cand)],
                     half, topk), axis=0)
    sp = sp_ref[...]
    need = topk - (_count(sp > tau) + _count(sn > tau))
    eqn = (sn == tau) & adm_n
    excess = jnp.max(_count(sp == tau) + _count(eqn) - need) > 0

    @pl.when(jnp.logical_not(excess))
    def _():
        selp_ref[...] = jnp.where(sp >= tau, 1.0, 0.0)
        seln_ref[...] = jnp.where((sn >= tau) & adm_n, 1.0, 0.0)

    @pl.when(excess)
    def _():
        needf = need.astype(F32)
        n_chunks = past // LANES
        chunks = [(i, sp_ref[:, i * LANES:(i + 1) * LANES] == tau) for i in range(n_chunks)]
        chunks.append((n_chunks, eqn))
        for i, rank in _tie_rank(chunks, rows):
            if i < n_chunks:
                kc = sp_ref[:, i * LANES:(i + 1) * LANES]
                sel = (kc > tau) | ((kc == tau) & (rank < needf))
                selp_ref[:, i * LANES:(i + 1) * LANES] = jnp.where(sel, 1.0, 0.0)
            else:
                sel = ((sn > tau) | ((sn == tau) & (rank < needf))) & adm_n
                seln_ref[...] = jnp.where(sel, 1.0, 0.0)


def _sample_select(scores_past, scores_new, t_new, rows):
    n, past = scores_past.shape
    assert past >= IDX_TOPK_MAX, "every query must see at least top-k admissible keys"
    assert t_new == SUBLANES and n % rows == 0
    topk = min(IDX_TOPK_MAX, (past + t_new) // 4)
    blk = lambda w: pl.BlockSpec((rows, w), lambda i: (i, 0))
    return pl.pallas_call(
        functools.partial(_sample_select_kernel, rows=rows, t_new=t_new, past=past, topk=topk),
        grid=(n // rows,),
        in_specs=[blk(past), blk(LANES)],
        out_specs=[blk(past), blk(LANES)],
        out_shape=[jax.ShapeDtypeStruct((n, past), F32), jax.ShapeDtypeStruct((n, LANES), F32)],
        compiler_params=_cparams(("parallel",)),
        name="sample_select",
    )(scores_past, scores_new)


def _sample_attend_kernel(pt_ref, q_ref, selp_ref, seln_ref, *refs, n_groups, pages_per_step, page):
    del pt_ref
    g = pages_per_step
    k_refs, v_refs = refs[:g], refs[g:2 * g]
    kn_ref, vn_ref, o_ref, m_ref, l_ref, acc_ref = refs[2 * g:]
    step = pl.program_id(1)
    rows = N_HEADS * SUBLANES

    @pl.when(step == 0)
    def _():
        m_ref[...] = jnp.full((rows, 1), NEG_INF, F32)
        l_ref[...] = jnp.zeros((rows, 1), F32)
        acc_ref[...] = jnp.zeros((rows, ATT_W), F32)

    qrow = lax.broadcasted_iota(jnp.int32, (rows, ATT_W), 0)
    qcol = lax.broadcasted_iota(jnp.int32, (rows, ATT_W), 1)
    q_bd = jnp.where((qcol >> 6) == (qrow >> 3), jnp.concatenate([q_ref[...]] * N_HEADS, axis=1), 0.0).astype(BF16)

    def process(k_pages, v_pages, sel):
        logit = jnp.concatenate([_dot(q_bd, kp.reshape(ATT_W, page).astype(BF16)) for kp in k_pages], axis=1)
        valid = jnp.concatenate([sel] * N_HEADS, axis=0) > 0.5
        logit = jnp.where(valid, logit, NEG_INF)
        m_old = m_ref[...]
        m_new = jnp.maximum(m_old, jnp.max(logit, axis=1, keepdims=True))
        m_safe = jnp.where(m_new == NEG_INF, 0.0, m_new)
        pr = jnp.exp(logit - m_safe)
        alpha = jnp.exp(m_old - m_safe)
        l_ref[...] = alpha * l_ref[...] + jnp.sum(pr, axis=1, keepdims=True)
        prb = pr.astype(BF16)
        pv = jnp.zeros((rows, ATT_W), F32)
        for i, vp in enumerate(v_pages):
            pv = pv + _dot_nt(prb[:, i * page:(i + 1) * page], vp.reshape(ATT_W, page).astype(BF16))
        acc_ref[...] = alpha * acc_ref[...] + pv
        m_ref[...] = m_new

    process([r[...] for r in k_refs], [r[...] for r in v_refs], selp_ref[...])

    @pl.when(step == n_groups - 1)
    def _():
        process([kn_ref[...]], [vn_ref[...]], seln_ref[...])
        out = acc_ref[...] / l_ref[...]
        for hd in range(N_HEADS):
            o_ref[:, hd * HEAD_DIM:(hd + 1) * HEAD_DIM] = out[hd * SUBLANES:(hd + 1) * SUBLANES,
                                                              hd * HEAD_DIM:(hd + 1) * HEAD_DIM]


def _sample_attend(page_table, q_s, selp, seln, cache_kt, cache_vt, kt_new_pad, vt_new_pad, layer, pages_per_step):
    bsz = q_s.shape[0]
    n_pages = page_table.shape[0] // bsz
    page = cache_kt.shape[-1]
    g = pages_per_step
    n_groups = n_pages // g

    def page_spec(i):
        return pl.BlockSpec(
            (None, None, N_HEADS, HEAD_DIM, page),
            lambda b, s, pt: (layer, pt[b * n_pages + s * g + i], 0, 0, 0))

    new_spec = pl.BlockSpec((None, N_HEADS, HEAD_DIM, page), lambda b, s, pt: (b, 0, 0, 0))
    rows = N_HEADS * SUBLANES
    grid_spec = pltpu.PrefetchScalarGridSpec(
        num_scalar_prefetch=1,
        grid=(bsz, n_groups),
        in_specs=[pl.BlockSpec((None, rows, HEAD_DIM), lambda b, s, pt: (b, 0, 0)),
                  pl.BlockSpec((None, SUBLANES, g * page), lambda b, s, pt: (b, 0, s)),
                  pl.BlockSpec((None, SUBLANES, LANES), lambda b, s, pt: (b, 0, 0))]
                 + [page_spec(i) for i in range(g)] + [page_spec(i) for i in range(g)]
                 + [new_spec, new_spec],
        out_specs=pl.BlockSpec((None, SUBLANES, ATT_W), lambda b, s, pt: (b, 0, 0)),
        scratch_shapes=[pltpu.VMEM((rows, 1), F32), pltpu.VMEM((rows, 1), F32), pltpu.VMEM((rows, ATT_W), F32)],
    )
    return pl.pallas_call(
        functools.partial(_sample_attend_kernel, n_groups=n_groups, pages_per_step=g, page=page),
        grid_spec=grid_spec,
        out_shape=jax.ShapeDtypeStruct((bsz, SUBLANES, ATT_W), F32),
        compiler_params=_cparams(("parallel", "arbitrary")),
        name="sample_attend",
    )(page_table, q_s, selp, seln, *([cache_kt] * g), *([cache_vt] * g), kt_new_pad, vt_new_pad)


def _mix_kernel(x_ref, c_ref, o_ref, g0_ref, b0_ref, wg_ref, bg_ref, wco_ref, wao_ref, wo_ref,
                g1_ref, b1_ref, wrh_ref, wrl_ref, br_ref, x1_ref, comb_ref, *, alpha):
    h = _layer_norm(x_ref[...], g0_ref[...], b0_ref[...])
    hb = h.astype(BF16)
    gc = _sigmoid(_dot(hb, wg_ref[:, :D_MODEL]) + bg_ref[:, :D_MODEL])
    ga = _sigmoid(_dot(hb, wg_ref[:, D_MODEL:]) + bg_ref[:, D_MODEL:])
    merged = gc * _dot(c_ref[...], wco_ref[...]) + ga * _dot(o_ref[...], wao_ref[...])
    mix = _dot(merged.astype(BF16), wo_ref[...])
    x1 = _layer_norm(alpha * h + mix, g1_ref[...], b1_ref[...])
    x1_ref[...] = x1

    xh = x1.astype(BF16)
    xl = (x1 - xh.astype(F32)).astype(BF16)
    logits = _dot(xh, wrh_ref[...]) + (_dot(xl, wrh_ref[...]) + _dot(xh, wrl_ref[...])) + br_ref[...]
    lane = lax.broadcasted_iota(jnp.int32, logits.shape, 1)
    is_grp = (lane >= N_EXPERTS) & (lane < N_EXPERTS + N_GROUPS)
    gl = jnp.where(is_grp, logits, NEG_INF)
    gmax = jnp.max(gl, axis=1, keepdims=True)
    grp = jnp.min(jnp.where(gl == gmax, lane, 4 * LANES), axis=1, keepdims=True) - N_EXPERTS
    p_grp = 1.0 / jnp.sum(jnp.exp(gl - gmax), axis=1, keepdims=True)
    in_grp = (lane < N_EXPERTS) & ((lane >> 3) == grp)
    el = jnp.where(in_grp, logits, NEG_INF)
    emax = jnp.max(el, axis=1, keepdims=True)
    ee = jnp.exp(el - emax)
    pe = ee / jnp.sum(ee, axis=1, keepdims=True)
    pe = jnp.where(in_grp, pe, -1.0)
    p1 = jnp.max(pe, axis=1, keepdims=True)
    i1 = jnp.min(jnp.where(pe == p1, lane, 4 * LANES), axis=1, keepdims=True)
    pe2 = jnp.where(lane == i1, -1.0, pe)
    p2 = jnp.max(pe2, axis=1, keepdims=True)
    i2 = jnp.min(jnp.where(pe2 == p2, lane, 4 * LANES), axis=1, keepdims=True)
    tot = p1 + p2
    comb_ref[...] = (jnp.where(lane == i1, p_grp * (p1 / tot), 0.0)
                     + jnp.where(lane == i2, p_grp * (p2 / tot), 0.0)
                     + jnp.where(lane == GROUP_LANE, grp.astype(F32), 0.0))


def _mix(x, c, o, g0, b0, wg, bg, wco, wao, wo, g1, b1, wrh, wrl, br, alpha, tm):
    n = x.shape[0]
    row = lambda i: (i, 0)
    full = lambda i: (0, 0)
    fs = lambda a: pl.BlockSpec(a.shape, full)
    return pl.pallas_call(
        functools.partial(_mix_kernel, alpha=alpha),
        grid=(n // tm,),
        in_specs=[pl.BlockSpec((tm, D_MODEL), row), pl.BlockSpec((tm, CONV_CH), row),
                  pl.BlockSpec((tm, ATT_W), row)] + [fs(a) for a in (g0, b0, wg, bg, wco, wao, wo, g1, b1, wrh, wrl, br)],
        out_specs=[pl.BlockSpec((tm, D_MODEL), row), pl.BlockSpec((tm, LANES), row)],
        out_shape=[jax.ShapeDtypeStruct((n, D_MODEL), F32), jax.ShapeDtypeStruct((n, LANES), F32)],
        compiler_params=_cparams(("parallel",)),
        name="mix",
    )(x, c, o, g0, b0, wg, bg, wco, wao, wo, g1, b1, wrh, wrl, br)


def _split3(a):
    hi = a.astype(BF16)
    r1 = a - hi.astype(F32)
    mid = r1.astype(BF16)
    lo = (r1 - mid.astype(F32)).astype(BF16)
    return hi, mid, lo


def _permute_rows(perm, a):
    hi, mid, lo = _split3(a)
    return _dot(perm, hi) + _dot(perm, mid) + _dot(perm, lo)


def _ffn_kernel(x1_ref, comb_ref, p_ref, wpg_ref, wpp_ref, weg_ref, weu_ref, wed_ref, g2_ref, b2_ref,
                y_ref, to_slot_ref, to_token_ref, xs_ref, cs_ref, acc_ref, meta_ref,
                *, alpha, tm, rb, eps, slots):
    s = pl.program_id(1)

    @pl.when(s == 0)
    def _():
        comb = comb_ref[...]
        lane = lax.broadcasted_iota(jnp.int32, (tm, LANES), 1)
        lane_row = lax.broadcasted_iota(jnp.int32, (1, LANES), 1)
        grp = comb[:, GROUP_LANE:GROUP_LANE + 1]
        onehot = jnp.where((lane.astype(F32) == grp) & (lane < N_GROUPS), 1.0, 0.0)
        r = lax.broadcasted_iota(jnp.int32, (tm, tm), 0)
        c = lax.broadcasted_iota(jnp.int32, (tm, tm), 1)
        earlier = jnp.where(c < r, 1.0, 0.0).astype(BF16)
        before = _dot(earlier, onehot.astype(BF16))
        cnt = jnp.sum(onehot, axis=0, keepdims=True)
        base = jnp.zeros((1, LANES), F32)
        run = jnp.zeros((1, 1), F32)
        for g in range(N_GROUPS):
            base = base + jnp.where(lane_row == g, run, 0.0)
            meta_ref[g] = run[0, 0].astype(jnp.int32)
            meta_ref[N_GROUPS + g] = cnt[0, g].astype(jnp.int32)
            run = jnp.floor((run + cnt[:, g:g + 1] + (GROUP_ALIGN - 1.0)) * (1.0 / GROUP_ALIGN)) * GROUP_ALIGN
        slot = jnp.sum(onehot * (before + base), axis=1, keepdims=True)
        slot_row = jnp.transpose(jnp.broadcast_to(slot, (tm, LANES)))[0:1, :]
        slot_c = lax.broadcasted_iota(jnp.int32, (tm, slots), 1).astype(F32)
        slot_r = lax.broadcasted_iota(jnp.int32, (slots, tm), 0).astype(F32)
        to_slot_ref[...] = jnp.where(slot == slot_c, 1.0, 0.0).astype(BF16)
        to_token = jnp.where(slot_row == slot_r, 1.0, 0.0).astype(BF16)
        to_token_ref[...] = to_token
        xs_ref[0:slots, :] = _dot(to_token, x1_ref[...].astype(BF16)).astype(BF16)
        cs_ref[0:slots, :] = _permute_rows(to_token, comb)
        xs_ref[slots:, :] = jnp.zeros((rb, D_MODEL), BF16)
        cs_ref[slots:, :] = jnp.zeros((rb, LANES), F32)
        acc_ref[...] = jnp.zeros((slots + rb, D_MODEL), F32)

    g = (s * eps) // EXPERTS_PER_GROUP
    first = meta_ref[g]
    count = meta_ref[N_GROUPS + g]
    for blk in range(-(-tm // rb)):
        @pl.when(blk * rb < count)
        def _():
            rows = pl.ds(pl.multiple_of(first + blk * rb, GROUP_ALIGN), rb)
            xb = xs_ref[rows, :]
            cs = cs_ref[rows, :]
            lane = lax.broadcasted_iota(jnp.int32, (rb, LANES), 1)
            out = jnp.zeros((rb, D_MODEL), F32)
            for j in range(eps):
                ce = jnp.sum(jnp.where(lane == s * eps + j, cs, 0.0), axis=1, keepdims=True)
                hg = _dot(xb, weg_ref[j])
                hu = _dot(xb, weu_ref[j])
                hidden = (hg * _sigmoid(hg)) * hu * ce
                out = out + _dot(hidden.astype(BF16), wed_ref[j])
            acc_ref[rows, :] += out

    @pl.when(s == N_EXPERTS // eps - 1)
    def _():
        x1 = x1_ref[...]
        ffn = _permute_rows(to_slot_ref[...], acc_ref[0:slots, :])
        ple = _sigmoid(_dot(x1.astype(BF16), wpg_ref[...])) * _dot(p_ref[...].astype(BF16), wpp_ref[...])
        y_ref[...] = _layer_norm(alpha * x1 + ffn + ple, g2_ref[...], b2_ref[...])


def _ffn(x1, comb, p, wpg, wpp, weg, weu, wed, g2, b2, alpha, tm, rb, eps):
    n = x1.shape[0]
    assert rb % GROUP_ALIGN == 0 and EXPERTS_PER_GROUP % eps == 0
    slots = -(-(tm + N_GROUPS * GROUP_ALIGN) // LANES) * LANES
    row = lambda i, e: (i, 0)
    full = lambda i, e: (0, 0)
    exp = lambda i, e: (e, 0, 0)
    return pl.pallas_call(
        functools.partial(_ffn_kernel, alpha=alpha, tm=tm, rb=rb, eps=eps, slots=slots),
        grid=(n // tm, N_EXPERTS // eps),
        in_specs=[pl.BlockSpec((tm, D_MODEL), row), pl.BlockSpec((tm, LANES), row),
                  pl.BlockSpec((tm, PLE_DIM), row),
                  pl.BlockSpec((D_MODEL, D_MODEL), full), pl.BlockSpec((PLE_DIM, D_MODEL), full),
                  pl.BlockSpec((eps, D_MODEL, D_EXPERT), exp), pl.BlockSpec((eps, D_MODEL, D_EXPERT), exp),
                  pl.BlockSpec((eps, D_EXPERT, D_MODEL), exp),
                  pl.BlockSpec((1, D_MODEL), full), pl.BlockSpec((1, D_MODEL), full)],
        out_specs=pl.BlockSpec((tm, D_MODEL), row),
        out_shape=jax.ShapeDtypeStruct((n, D_MODEL), F32),
        scratch_shapes=[pltpu.VMEM((tm, slots), BF16), pltpu.VMEM((slots, tm), BF16),
                        pltpu.VMEM((slots + rb, D_MODEL), BF16), pltpu.VMEM((slots + rb, LANES), F32),
                        pltpu.VMEM((slots + rb, D_MODEL), F32), pltpu.SMEM((2 * N_GROUPS,), jnp.int32)],
        compiler_params=pltpu.CompilerParams(dimension_semantics=("parallel", "arbitrary"),
                                             vmem_limit_bytes=VMEM_LIMIT_FFN),
        name="ffn",
    )(x1, comb, p, wpg, wpp, weg, weu, wed, g2, b2)


def _tile(n, target):
    t = min(n, target)
    assert n % t == 0, (n, t)
    return t


def _row(v):
    return v.reshape(1, -1).astype(F32)


def kernel(x_prompt, x_sample, cache_k, cache_v, cache_kidx, state_conv, page_table, p_prompt, p_sample,
           ln0_g, ln0_b, w_in, b_gate, conv_w, conv_b, lnc_g, lnc_b, w_conv_out, w_attn_out, w_o,
           ln1_g, ln1_b, w_rg, b_rg, w_re, b_re, w_eg, w_eu, w_ed, w_pg, w_pp, ln2_g, ln2_b):
    depth = w_in.shape[0]
    assert depth == 1, "single-layer step"
    assert w_in.shape[1:] == (D_MODEL, COL_END)
    layer = 0
    alpha = (2.0 * depth) ** 0.25
    bp, tp, _ = x_prompt.shape
    bs, ts, _ = x_sample.shape
    assert ts == SUBLANES, "sample step length must fill one sublane tile"
    page = cache_k.shape[2]
    n_pages = page_table.shape[1]
    assert page == LANES

    g0, b0 = _row(ln0_g), _row(ln0_b)
    w = w_in[layer]
    wi_cols = jnp.pad(w[:, COL_WI:COL_GC], ((0, 0), (0, LANES - N_IDX_HEADS)))
    ws = jnp.concatenate([w[:, COL_GLU_A:COL_K], w[:, COL_QI:COL_KI], wi_cols], axis=1).astype(BF16)
    wt = w[:, COL_K:COL_QI].T
    wt = jnp.concatenate([wt, w[:, COL_KI:COL_WI].T], axis=0).astype(BF16)
    wg = w[:, COL_GC:].astype(BF16)
    bg = _row(b_gate[layer])
    wr = jnp.pad(jnp.concatenate([w_re[layer], w_rg[layer]], axis=1), ((0, 0), (0, LANES - N_EXPERTS - N_GROUPS)))
    wrh = wr.astype(BF16)
    wrl = (wr - wrh.astype(F32)).astype(BF16)
    br = jnp.pad(jnp.concatenate([b_re[layer], b_rg[layer]]), (0, LANES - N_EXPERTS - N_GROUPS)).reshape(1, LANES)
    wco, wao, wo = w_conv_out[layer].astype(BF16), w_attn_out[layer].astype(BF16), w_o[layer].astype(BF16)
    wpg, wpp = w_pg[layer].astype(BF16), w_pp[layer].astype(BF16)
    weg, weu, wed = w_eg[layer].astype(BF16), w_eu[layer].astype(BF16), w_ed[layer].astype(BF16)
    cw, cb = conv_w[layer], _row(conv_b[layer])
    lcg, lcb = _row(lnc_g[layer]), _row(lnc_b[layer])
    g1, b1, g2, b2 = _row(ln1_g[layer]), _row(ln1_b[layer]), _row(ln2_g[layer]), _row(ln2_b[layer])
    cache_kt = jnp.transpose(cache_k, (0, 1, 3, 4, 2))
    cache_vt = jnp.transpose(cache_v, (0, 1, 3, 4, 2))
    cache_kidx_t = jnp.transpose(cache_kidx, (0, 1, 3, 2))

    def tail(x1, comb, p):
        n = x1.shape[0]
        tm = _tile(n, 1024)
        rb = min(tm, tm // N_GROUPS + 4 * GROUP_ALIGN)
        return _ffn(x1, comb, p.reshape(n, PLE_DIM), wpg, wpp, weg, weu, wed, g2, b2, alpha, tm, rb, 4)

    def mix(x, c, o):
        n = x.shape[0]
        return _mix(x, c, o, g0, b0, wg, bg, wco, wao, wo, g1, b1, wrh, wrl, br, alpha, _tile(n, 512))

    np_ = bp * tp
    xp = x_prompt.reshape(np_, D_MODEL)
    glu, q, qi, wi, kt, vt, ktb, vtb, kit, kitb = _proj(xp, g0, b0, ws, wt, bp, tp, _tile(tp, 512))
    c, newconv_p = _conv(glu.reshape(bp, tp, CONV_CH), None, cw, cb, lcg, lcb, _tile(tp, 256))
    tq = _tile(tp, 256)
    o = jnp.concatenate(
        [_attn_prompt(qi, wi, q, kitb, ktb, vtb, bp, tp, tq, i, 1, (i + 1) * tq) for i in range(tp // tq)],
        axis=1).reshape(np_, ATT_W)
    x1, comb = mix(xp, c.reshape(np_, CONV_CH), o)
    y_prompt = tail(x1, comb, p_prompt[layer]).reshape(bp, tp, D_MODEL)
    new_k_p = jnp.transpose(kt, (0, 3, 1, 2))[None]
    new_v_p = jnp.transpose(vt, (0, 3, 1, 2))[None]
    new_ki_p = jnp.transpose(kit, (0, 2, 1))[None]

    ns = bs * ts
    xs = x_sample.reshape(ns, D_MODEL)
    glu, q, qi, wi, kt, vt, _, _, kit, kitb = _proj(xs, g0, b0, ws, wt, 1, ns, ns)
    c, newconv_s = _conv(glu.reshape(bs, ts, CONV_CH), state_conv[layer], cw, cb, lcg, lcb, ts)
    to_rows = lambda a: a.reshape(N_HEADS, bs, ts, HEAD_DIM).transpose(1, 0, 2, 3).reshape(bs, N_HEADS * ts, HEAD_DIM)
    qi_s, q_s = to_rows(qi), to_rows(q)
    wi_col = wi.reshape(bs, ts, N_IDX_HEADS).transpose(0, 2, 1).reshape(bs, N_IDX_HEADS * ts, 1)
    kt_s = kt.reshape(N_HEADS, HEAD_DIM, bs, ts)
    vt_s = vt.reshape(N_HEADS, HEAD_DIM, bs, ts)
    kit_s = kit.reshape(IDX_DIM, bs, ts)
    pad_keys = lambda a: jnp.pad(a, ((0, 0),) * (a.ndim - 1) + ((0, page - ts),))
    pt_flat = page_table.reshape(-1).astype(jnp.int32)
    pages_per = lambda target: max(d for d in range(1, target + 1) if n_pages % d == 0)
    scores_past, scores_new = _sample_scores(
        pt_flat, qi_s, wi_col, pad_keys(kitb.reshape(IDX_DIM, bs, ts).transpose(1, 0, 2)),
        cache_kidx_t, layer, pages_per(64))
    selp, seln = _sample_select(scores_past, scores_new, ts, _tile(ns, 64))
    o = _sample_attend(pt_flat, q_s, selp.reshape(bs, ts, -1), seln.reshape(bs, ts, LANES), cache_kt, cache_vt,
                       pad_keys(kt_s.transpose(2, 0, 1, 3)), pad_keys(vt_s.transpose(2, 0, 1, 3)),
                       layer, pages_per(32))
    x1, comb = mix(xs, c.reshape(ns, CONV_CH), o.reshape(ns, ATT_W).astype(BF16))
    y_sample = tail(x1, comb, p_sample[layer]).reshape(bs, ts, D_MODEL)
    new_k_s = kt_s.transpose(2, 3, 0, 1)[None]
    new_v_s = vt_s.transpose(2, 3, 0, 1)[None]
    new_ki_s = kit_s.transpose(1, 2, 0)[None]

    return (y_prompt, y_sample, new_k_p, new_v_p, new_ki_p, newconv_p[None],
            new_k_s, new_v_s, new_ki_s, newconv_s[None])
```

```python
import functools

import jax
import jax.numpy as jnp
from jax import lax
from jax.experimental import pallas as pl
from jax.experimental.pallas import tpu as pltpu

D_MODEL = 1024
CONV_CH = 512
CONV_WIDTH = 31
CONV_HIST = CONV_WIDTH - 1
N_HEADS = 8
HEAD_DIM = 64
ATT_W = N_HEADS * HEAD_DIM
N_IDX_HEADS = 8
IDX_DIM = 64
IDX_TOPK_MAX = 256
N_GROUPS = 4
EXPERTS_PER_GROUP = 8
N_EXPERTS = N_GROUPS * EXPERTS_PER_GROUP
D_EXPERT = 256
PLE_DIM = 256
LN_EPS = 1e-5
IDX_SCALE = (N_IDX_HEADS * IDX_DIM) ** -0.5
ATT_SCALE = HEAD_DIM ** -0.5
COL_GLU_A, COL_GLU_B, COL_Q, COL_K, COL_V, COL_QI, COL_KI, COL_WI, COL_GC = (
    0, 512, 1024, 1536, 2048, 2560, 3072, 3136, 3144)
COL_END = COL_GC + 2 * D_MODEL

LANES = 128
SUBLANES = 8
PREFIX_ROWS = 32
GROUP_LANE = LANES - 1
GROUP_ALIGN = 16
SEARCH_PARTS = 4
VMEM_LIMIT = 48 * 1024 * 1024
VMEM_LIMIT_FFN = 60 * 1024 * 1024

F32 = jnp.float32
BF16 = jnp.bfloat16
NEG_INF = float("-inf")
INT_MIN = -2 ** 31


def _cparams(sem):
    return pltpu.CompilerParams(dimension_semantics=sem, vmem_limit_bytes=VMEM_LIMIT)


def _layer_norm(x, g, b):
    mu = jnp.mean(x, axis=-1, keepdims=True)
    xc = x - mu
    var = jnp.mean(xc * xc, axis=-1, keepdims=True)
    return xc * lax.rsqrt(var + LN_EPS) * g + b


def _sigmoid(x):
    return 1.0 / (1.0 + jnp.exp(-x))


def _dot(a, b):
    return jnp.dot(a, b, preferred_element_type=F32)


def _dot_nt(a, b):
    return lax.dot_general(a, b, (((1,), (1,)), ((), ())), preferred_element_type=F32)


def _proj_kernel(x_ref, g0_ref, b0_ref, ws_ref, wt_ref,
                 glu_ref, q_ref, qi_ref, wi_ref, kt_ref, vt_ref, ktb_ref, vtb_ref, kit_ref, kitb_ref):
    h = _layer_norm(x_ref[...], g0_ref[...], b0_ref[...])
    hb = h.astype(BF16)

    def seg(col, width=512):
        return _dot(hb, ws_ref[:, col:col + width])

    glu_ref[...] = seg(0) * _sigmoid(seg(512))
    q = seg(1024) * ATT_SCALE
    qi = seg(1536)
    for hd in range(N_HEADS):
        sl = slice(hd * HEAD_DIM, (hd + 1) * HEAD_DIM)
        q_ref[hd] = q[:, sl].astype(BF16)
        qi_ref[hd] = qi[:, sl].astype(BF16)
    wi_ref[...] = seg(2048, LANES)[:, :N_IDX_HEADS]

    tm = hb.shape[0]
    kt = _dot_nt(wt_ref[0:ATT_W, :], hb).reshape(N_HEADS, HEAD_DIM, tm)
    kt_ref[...] = kt
    ktb_ref[...] = kt.astype(BF16)
    vt = _dot_nt(wt_ref[ATT_W:2 * ATT_W, :], hb).reshape(N_HEADS, HEAD_DIM, tm)
    vt_ref[...] = vt
    vtb_ref[...] = vt.astype(BF16)
    kit = _dot_nt(wt_ref[2 * ATT_W:2 * ATT_W + IDX_DIM, :], hb)
    kit_ref[...] = kit
    kitb_ref[...] = kit.astype(BF16)


def _proj(x, g0, b0, ws, wt, b, t, tm):
    n = b * t
    nt = t // tm
    row = lambda i: (i, 0)
    head = lambda i: (0, i, 0)
    full = lambda i: (0, 0)
    hm = jax.ShapeDtypeStruct((N_HEADS, n, HEAD_DIM), BF16)
    hm_spec = pl.BlockSpec((N_HEADS, tm, HEAD_DIM), head)
    kv_spec = pl.BlockSpec((None, N_HEADS, HEAD_DIM, tm), lambda i: (i // nt, 0, 0, i % nt))
    ki_spec = pl.BlockSpec((None, IDX_DIM, tm), lambda i: (i // nt, 0, i % nt))
    kv_shape = lambda dt: jax.ShapeDtypeStruct((b, N_HEADS, HEAD_DIM, t), dt)
    ki_shape = lambda dt: jax.ShapeDtypeStruct((b, IDX_DIM, t), dt)
    return pl.pallas_call(
        _proj_kernel,
        grid=(n // tm,),
        in_specs=[pl.BlockSpec((tm, D_MODEL), row),
                  pl.BlockSpec((1, D_MODEL), full), pl.BlockSpec((1, D_MODEL), full),
                  pl.BlockSpec(ws.shape, full), pl.BlockSpec(wt.shape, full)],
        out_specs=[pl.BlockSpec((tm, CONV_CH), row), hm_spec, hm_spec,
                   pl.BlockSpec((tm, N_IDX_HEADS), row),
                   kv_spec, kv_spec, kv_spec, kv_spec, ki_spec, ki_spec],
        out_shape=[jax.ShapeDtypeStruct((n, CONV_CH), F32), hm, hm,
                   jax.ShapeDtypeStruct((n, N_IDX_HEADS), F32),
                   kv_shape(F32), kv_shape(F32), kv_shape(BF16), kv_shape(BF16), ki_shape(F32), ki_shape(BF16)],
        compiler_params=_cparams(("parallel",)),
        name="proj",
    )(x, g0, b0, ws, wt)


def _conv_kernel(*refs, tc, nt, has_state):
    it = iter(refs)
    glu_ref = next(it)
    prev_ref = next(it) if nt > 1 else None
    state_ref = next(it) if has_state else None
    w_ref, cb_ref, g_ref, b_ref, c_ref, newconv_ref, buf_ref, shift_ref, acc_ref = it
    t = pl.program_id(1)

    @pl.when(t == 0)
    def _():
        buf_ref[0:PREFIX_ROWS, :] = jnp.zeros((PREFIX_ROWS, CONV_CH), F32)
        if has_state:
            buf_ref[2:PREFIX_ROWS, :] = state_ref[...]

    if nt > 1:
        @pl.when(t > 0)
        def _():
            buf_ref[0:PREFIX_ROWS, :] = prev_ref[...]

    buf_ref[PREFIX_ROWS:PREFIX_ROWS + tc, :] = glu_ref[...]

    shifted_rows = tc + PREFIX_ROWS - SUBLANES
    for s in range(1, SUBLANES):
        shift_ref[s - 1] = buf_ref[s:s + shifted_rows, :]

    rb = min(tc, 32)
    for c0 in range(0, CONV_CH, LANES):
        cs = slice(c0, c0 + LANES)
        bias = cb_ref[:, cs]
        for r0 in range(0, tc, rb):
            acc = jnp.zeros((rb, LANES), F32) + bias
            for j in range(CONV_WIDTH):
                a, s = divmod(2 + j, SUBLANES)
                lo = r0 + a * SUBLANES
                rows = buf_ref[lo:lo + rb, cs] if s == 0 else shift_ref[s - 1, lo:lo + rb, cs]
                acc = acc + w_ref[j:j + 1, cs] * rows
            acc_ref[r0:r0 + rb, cs] = acc

    y = _layer_norm(acc_ref[...], g_ref[...], b_ref[...])
    c_ref[...] = (y * _sigmoid(y)).astype(BF16)

    @pl.when(t == nt - 1)
    def _():
        newconv_ref[...] = buf_ref[tc + 2:tc + PREFIX_ROWS, :]


def _conv(glu, state, conv_w, conv_b, lnc_g, lnc_b, tc):
    b, t, _ = glu.shape
    nt = t // tc
    blocks_per_tile = tc // PREFIX_ROWS
    in_specs = [pl.BlockSpec((None, tc, CONV_CH), lambda i, j: (i, j, 0))]
    args = [glu]
    if nt > 1:
        in_specs.append(pl.BlockSpec((None, PREFIX_ROWS, CONV_CH),
                                     lambda i, j: (i, jnp.maximum(j * blocks_per_tile - 1, 0), 0)))
        args.append(glu)
    if state is not None:
        in_specs.append(pl.BlockSpec((None, CONV_HIST, CONV_CH), lambda i, j: (i, 0, 0)))
        args.append(state)
    full = lambda i, j: (0, 0)
    in_specs += [pl.BlockSpec((CONV_WIDTH, CONV_CH), full), pl.BlockSpec((1, CONV_CH), full),
                 pl.BlockSpec((1, CONV_CH), full), pl.BlockSpec((1, CONV_CH), full)]
    args += [conv_w, conv_b, lnc_g, lnc_b]
    return pl.pallas_call(
        functools.partial(_conv_kernel, tc=tc, nt=nt, has_state=state is not None),
        grid=(b, nt),
        in_specs=in_specs,
        out_specs=[pl.BlockSpec((None, tc, CONV_CH), lambda i, j: (i, j, 0)),
                   pl.BlockSpec((None, CONV_HIST, CONV_CH), lambda i, j: (i, 0, 0))],
        out_shape=[jax.ShapeDtypeStruct((b, t, CONV_CH), BF16),
                   jax.ShapeDtypeStruct((b, CONV_HIST, CONV_CH), F32)],
        scratch_shapes=[pltpu.VMEM((PREFIX_ROWS + tc, CONV_CH), F32),
                        pltpu.VMEM((SUBLANES - 1, tc + PREFIX_ROWS - SUBLANES, CONV_CH), F32),
                        pltpu.VMEM((tc, CONV_CH), F32)],
        compiler_params=_cparams(("parallel", "arbitrary")),
        name="conv",
    )(*args)


def _ordered_to_f32(c):
    return pltpu.bitcast(c ^ ((c >> 31) & jnp.int32(0x7FFFFFFF)), F32)


def _kth_largest(count_ge_parts, rows, topk):
    def body(i, taus):
        bit = lax.shift_left(jnp.int32(1), jnp.int32(31) - i)
        out = []
        for count_ge, tau in zip(count_ge_parts, taus):
            cand = tau ^ bit
            out.append(jnp.where(count_ge(_ordered_to_f32(cand)) >= topk, cand, tau))
        return tuple(out)
    init = tuple(jnp.full((rows, 1), INT_MIN, jnp.int32) for _ in count_ge_parts)
    return [_ordered_to_f32(t) for t in lax.fori_loop(0, 32, body, init, unroll=16)]


def _count(mask):
    return jnp.sum(mask.astype(jnp.int32), axis=1, keepdims=True)


def _tie_rank(eq_chunks, rows):
    r = lax.broadcasted_iota(jnp.int32, (LANES, LANES), 0)
    c = lax.broadcasted_iota(jnp.int32, (LANES, LANES), 1)
    upper = jnp.where(r < c, 1.0, 0.0).astype(BF16)
    prefix = jnp.zeros((rows, 1), F32)
    for idx, eq in eq_chunks:
        eqf = jnp.where(eq, 1.0, 0.0)
        yield idx, _dot(eqf.astype(BF16), upper) + prefix
        prefix = prefix + jnp.sum(eqf, axis=1, keepdims=True)


def _attn_prompt_kernel(qi_ref, wi_ref, q_ref, kitb_ref, ktb_ref, vtb_ref, o_ref, sc_ref, bias_ref,
                        *, tq, t_eff, topk, first_block):
    qb = first_block + pl.program_id(1)
    wi = wi_ref[...]
    kitb = kitb_ref[...]
    acc = jnp.zeros((tq, t_eff), F32)
    for hd in range(N_IDX_HEADS):
        acc = acc + jnp.maximum(_dot(qi_ref[hd], kitb), 0.0) * wi[:, hd:hd + 1]
    q_pos = qb * tq + lax.broadcasted_iota(jnp.int32, (tq, 1), 0)
    k_pos = lax.broadcasted_iota(jnp.int32, (1, t_eff), 1)
    adm = k_pos <= q_pos
    sc_ref[...] = jnp.where(adm, acc * IDX_SCALE, NEG_INF)

    take_all = (q_pos + 1) <= topk
    if t_eff <= topk:
        tau = sc_ref[:, 0:1]
    else:
        part = tq // SEARCH_PARTS
        tau = jnp.concatenate(
            _kth_largest([functools.partial(lambda lo, cand: _count(sc_ref[lo:lo + part, :] >= cand), i * part)
                          for i in range(SEARCH_PARTS)], part, topk), axis=0)
    sc = sc_ref[...]
    eq = (sc == tau) & adm
    need = topk - _count(sc > tau)
    excess = jnp.max(jnp.where(take_all, 0, _count(eq) - need)) > 0

    @pl.when(jnp.logical_not(excess))
    def _():
        bias_ref[...] = jnp.where(adm & ((sc >= tau) | take_all), 0.0, NEG_INF)

    @pl.when(excess)
    def _():
        needf = need.astype(F32)
        chunks = [(i, (sc_ref[:, i * LANES:(i + 1) * LANES] == tau) & adm[:, i * LANES:(i + 1) * LANES])
                  for i in range(t_eff // LANES)]
        for i, rank in _tie_rank(chunks, tq):
            cs = slice(i * LANES, (i + 1) * LANES)
            kc = sc_ref[:, cs]
            sel = ((kc > tau) | ((kc == tau) & (rank < needf)) | take_all) & adm[:, cs]
            bias_ref[:, cs] = jnp.where(sel, 0.0, NEG_INF)

    bias = bias_ref[...]
    for hd in range(N_HEADS):
        logit = _dot(q_ref[hd], ktb_ref[hd]) + bias
        m = jnp.max(logit, axis=1, keepdims=True)
        p = jnp.exp(logit - m)
        denom = jnp.sum(p, axis=1, keepdims=True)
        o = _dot_nt(p.astype(BF16), vtb_ref[hd]) / denom
        o_ref[:, hd * HEAD_DIM:(hd + 1) * HEAD_DIM] = o.astype(BF16)


def _attn_prompt(qi, wi, q, kitb, ktb, vtb, b, t, tq, first_block, n_blocks, t_eff):
    nq = t // tq
    topk = min(IDX_TOPK_MAX, t // 4)
    assert (first_block + n_blocks) * tq <= t_eff
    qrow = lambda i, j: i * nq + first_block + j
    qspec = pl.BlockSpec((N_HEADS, tq, HEAD_DIM), lambda i, j: (0, qrow(i, j), 0))
    kspec = pl.BlockSpec((None, N_HEADS, HEAD_DIM, t_eff), lambda i, j: (i, 0, 0, 0))
    return pl.pallas_call(
        functools.partial(_attn_prompt_kernel, tq=tq, t_eff=t_eff, topk=topk, first_block=first_block),
        grid=(b, n_blocks),
        in_specs=[qspec, pl.BlockSpec((tq, N_IDX_HEADS), lambda i, j: (qrow(i, j), 0)), qspec,
                  pl.BlockSpec((None, IDX_DIM, t_eff), lambda i, j: (i, 0, 0)), kspec, kspec],
        out_specs=pl.BlockSpec((None, tq, ATT_W), lambda i, j: (i, j, 0)),
        out_shape=jax.ShapeDtypeStruct((b, n_blocks * tq, ATT_W), BF16),
        scratch_shapes=[pltpu.VMEM((tq, t_eff), F32), pltpu.VMEM((tq, t_eff), F32)],
        compiler_params=_cparams(("parallel", "arbitrary")),
        name="attn_prompt",
    )(qi, wi, q, kitb, ktb, vtb)


def _indexer_rows(qi, keys_t, wi_col):
    s = jnp.maximum(_dot(qi, keys_t), 0.0) * wi_col
    n = s.shape[1]
    return jnp.sum(s.reshape(N_IDX_HEADS, SUBLANES, n), axis=0) * IDX_SCALE


def _sample_scores_kernel(pt_ref, qi_ref, wi_ref, kin_ref, *refs, pages_per_step, page):
    del pt_ref
    page_refs, out_ref, new_ref = refs[:pages_per_step], refs[pages_per_step], refs[pages_per_step + 1]
    qi = qi_ref[...]
    wi_col = wi_ref[...]
    for i in range(pages_per_step):
        out_ref[:, i * page:(i + 1) * page] = _indexer_rows(qi, page_refs[i][...].astype(BF16), wi_col)

    @pl.when(pl.program_id(1) == 0)
    def _():
        new_ref[...] = _indexer_rows(qi, kin_ref[...], wi_col)


def _sample_scores(page_table, qi_s, wi_col, kitb_new_pad, cache_kidx_t, layer, pages_per_step):
    bsz = qi_s.shape[0]
    n_pages = page_table.shape[0] // bsz
    page = cache_kidx_t.shape[-1]
    steps = n_pages // pages_per_step

    def page_spec(i):
        return pl.BlockSpec((None, None, IDX_DIM, page),
                            lambda b, g, pt: (layer, pt[b * n_pages + g * pages_per_step + i], 0, 0))

    grid_spec = pltpu.PrefetchScalarGridSpec(
        num_scalar_prefetch=1,
        grid=(bsz, steps),
        in_specs=[pl.BlockSpec((None, N_IDX_HEADS * SUBLANES, IDX_DIM), lambda b, g, pt: (b, 0, 0)),
                  pl.BlockSpec((None, N_IDX_HEADS * SUBLANES, 1), lambda b, g, pt: (b, 0, 0)),
                  pl.BlockSpec((None, IDX_DIM, LANES), lambda b, g, pt: (b, 0, 0))]
                 + [page_spec(i) for i in range(pages_per_step)],
        out_specs=[pl.BlockSpec((SUBLANES, pages_per_step * page), lambda b, g, pt: (b, g)),
                   pl.BlockSpec((SUBLANES, LANES), lambda b, g, pt: (b, 0))],
    )
    return pl.pallas_call(
        functools.partial(_sample_scores_kernel, pages_per_step=pages_per_step, page=page),
        grid_spec=grid_spec,
        out_shape=[jax.ShapeDtypeStruct((bsz * SUBLANES, n_pages * page), F32),
                   jax.ShapeDtypeStruct((bsz * SUBLANES, LANES), F32)],
        compiler_params=_cparams(("parallel", "arbitrary")),
        name="sample_scores",
    )(page_table, qi_s, wi_col, kitb_new_pad, *([cache_kidx_t] * pages_per_step))


def _sample_select_kernel(sp_ref, sn_ref, selp_ref, seln_ref, *, rows, t_new, past, topk):
    jn = lax.broadcasted_iota(jnp.int32, (rows, LANES), 1)
    tn = lax.broadcasted_iota(jnp.int32, (rows, LANES), 0) & (SUBLANES - 1)
    adm_n = (jn <= tn) & (jn < t_new)
    sn = jnp.where(adm_n, sn_ref[...], NEG_INF)

    half = rows // 2
    tau = jnp.concatenate(
        _kth_largest([lambda cand: _count(sp_ref[0:half, :] >= cand) + _count(sn[0:half] >= cand),
                      lambda cand: _count(sp_ref[half:rows, :] >= cand) + _count(sn[half:rows] >= cand)],
                     half, topk), axis=0)
    sp = sp_ref[...]
    need = topk - (_count(sp > tau) + _count(sn > tau))
    eqn = (sn == tau) & adm_n
    excess = jnp.max(_count(sp == tau) + _count(eqn) - need) > 0

    @pl.when(jnp.logical_not(excess))
    def _():
        selp_ref[...] = jnp.where(sp >= tau, 1.0, 0.0)
        seln_ref[...] = jnp.where((sn >= tau) & adm_n, 1.0, 0.0)

    @pl.when(excess)
    def _():
        needf = need.astype(F32)
        n_chunks = past // LANES
        chunks = [(i, sp_ref[:, i * LANES:(i + 1) * LANES] == tau) for i in range(n_chunks)]
        chunks.append((n_chunks, eqn))
        for i, rank in _tie_rank(chunks, rows):
            if i < n_chunks:
                kc = sp_ref[:, i * LANES:(i + 1) * LANES]
                sel = (kc > tau) | ((kc == tau) & (rank < needf))
                selp_ref[:, i * LANES:(i + 1) * LANES] = jnp.where(sel, 1.0, 0.0)
            else:
                sel = ((sn > tau) | ((sn == tau) & (rank < needf))) & adm_n
                seln_ref[...] = jnp.where(sel, 1.0, 0.0)


def _sample_select(scores_past, scores_new, t_new, rows):
    n, past = scores_past.shape
    assert past >= IDX_TOPK_MAX, "every query must see at least top-k admissible keys"
    assert t_new == SUBLANES and n % rows == 0
    topk = min(IDX_TOPK_MAX, (past + t_new) // 4)
    blk = lambda w: pl.BlockSpec((rows, w), lambda i: (i, 0))
    return pl.pallas_call(
        functools.partial(_sample_select_kernel, rows=rows, t_new=t_new, past=past, topk=topk),
        grid=(n // rows,),
        in_specs=[blk(past), blk(LANES)],
        out_specs=[blk(past), blk(LANES)],
        out_shape=[jax.ShapeDtypeStruct((n, past), F32), jax.ShapeDtypeStruct((n, LANES), F32)],
        compiler_params=_cparams(("parallel",)),
        name="sample_select",
    )(scores_past, scores_new)


def _sample_attend_kernel(pt_ref, q_ref, selp_ref, seln_ref, *refs, n_groups, pages_per_step, page):
    del pt_ref
    g = pages_per_step
    k_refs, v_refs = refs[:g], refs[g:2 * g]
    kn_ref, vn_ref, o_ref, m_ref, l_ref, acc_ref = refs[2 * g:]
    step = pl.program_id(1)
    rows = N_HEADS * SUBLANES

    @pl.when(step == 0)
    def _():
        m_ref[...] = jnp.full((rows, 1), NEG_INF, F32)
        l_ref[...] = jnp.zeros((rows, 1), F32)
        acc_ref[...] = jnp.zeros((rows, ATT_W), F32)

    qrow = lax.broadcasted_iota(jnp.int32, (rows, ATT_W), 0)
    qcol = lax.broadcasted_iota(jnp.int32, (rows, ATT_W), 1)
    q_bd = jnp.where((qcol >> 6) == (qrow >> 3), jnp.concatenate([q_ref[...]] * N_HEADS, axis=1), 0.0).astype(BF16)

    def process(k_pages, v_pages, sel):
        logit = jnp.concatenate([_dot(q_bd, kp.reshape(ATT_W, page).astype(BF16)) for kp in k_pages], axis=1)
        valid = jnp.concatenate([sel] * N_HEADS, axis=0) > 0.5
        logit = jnp.where(valid, logit, NEG_INF)
        m_old = m_ref[...]
        m_new = jnp.maximum(m_old, jnp.max(logit, axis=1, keepdims=True))
        m_safe = jnp.where(m_new == NEG_INF, 0.0, m_new)
        pr = jnp.exp(logit - m_safe)
        alpha = jnp.exp(m_old - m_safe)
        l_ref[...] = alpha * l_ref[...] + jnp.sum(pr, axis=1, keepdims=True)
        prb = pr.astype(BF16)
        pv = jnp.zeros((rows, ATT_W), F32)
        for i, vp in enumerate(v_pages):
            pv = pv + _dot_nt(prb[:, i * page:(i + 1) * page], vp.reshape(ATT_W, page).astype(BF16))
        acc_ref[...] = alpha * acc_ref[...] + pv
        m_ref[...] = m_new

    process([r[...] for r in k_refs], [r[...] for r in v_refs], selp_ref[...])

    @pl.when(step == n_groups - 1)
    def _():
        process([kn_ref[...]], [vn_ref[...]], seln_ref[...])
        out = acc_ref[...] / l_ref[...]
        for hd in range(N_HEADS):
            o_ref[:, hd * HEAD_DIM:(hd + 1) * HEAD_DIM] = out[hd * SUBLANES:(hd + 1) * SUBLANES,
                                                              hd * HEAD_DIM:(hd + 1) * HEAD_DIM]


def _sample_attend(page_table, q_s, selp, seln, cache_kt, cache_vt, kt_new_pad, vt_new_pad, layer, pages_per_step):
    bsz = q_s.shape[0]
    n_pages = page_table.shape[0] // bsz
    page = cache_kt.shape[-1]
    g = pages_per_step
    n_groups = n_pages // g

    def page_spec(i):
        return pl.BlockSpec(
            (None, None, N_HEADS, HEAD_DIM, page),
            lambda b, s, pt: (layer, pt[b * n_pages + s * g + i], 0, 0, 0))

    new_spec = pl.BlockSpec((None, N_HEADS, HEAD_DIM, page), lambda b, s, pt: (b, 0, 0, 0))
    rows = N_HEADS * SUBLANES
    grid_spec = pltpu.PrefetchScalarGridSpec(
        num_scalar_prefetch=1,
        grid=(bsz, n_groups),
        in_specs=[pl.BlockSpec((None, rows, HEAD_DIM), lambda b, s, pt: (b, 0, 0)),
                  pl.BlockSpec((None, SUBLANES, g * page), lambda b, s, pt: (b, 0, s)),
                  pl.BlockSpec((None, SUBLANES, LANES), lambda b, s, pt: (b, 0, 0))]
                 + [page_spec(i) for i in range(g)] + [page_spec(i) for i in range(g)]
                 + [new_spec, new_spec],
        out_specs=pl.BlockSpec((None, SUBLANES, ATT_W), lambda b, s, pt: (b, 0, 0)),
        scratch_shapes=[pltpu.VMEM((rows, 1), F32), pltpu.VMEM((rows, 1), F32), pltpu.VMEM((rows, ATT_W), F32)],
    )
    return pl.pallas_call(
        functools.partial(_sample_attend_kernel, n_groups=n_groups, pages_per_step=g, page=page),
        grid_spec=grid_spec,
        out_shape=jax.ShapeDtypeStruct((bsz, SUBLANES, ATT_W), F32),
        compiler_params=_cparams(("parallel", "arbitrary")),
        name="sample_attend",
    )(page_table, q_s, selp, seln, *([cache_kt] * g), *([cache_vt] * g), kt_new_pad, vt_new_pad)


def _mix_kernel(x_ref, c_ref, o_ref, g0_ref, b0_ref, wg_ref, bg_ref, wco_ref, wao_ref, wo_ref,
                g1_ref, b1_ref, wrh_ref, wrl_ref, br_ref, x1_ref, comb_ref, *, alpha):
    h = _layer_norm(x_ref[...], g0_ref[...], b0_ref[...])
    hb = h.astype(BF16)
    gc = _sigmoid(_dot(hb, wg_ref[:, :D_MODEL]) + bg_ref[:, :D_MODEL])
    ga = _sigmoid(_dot(hb, wg_ref[:, D_MODEL:]) + bg_ref[:, D_MODEL:])
    merged = gc * _dot(c_ref[...], wco_ref[...]) + ga * _dot(o_ref[...], wao_ref[...])
    mix = _dot(merged.astype(BF16), wo_ref[...])
    x1 = _layer_norm(alpha * h + mix, g1_ref[...], b1_ref[...])
    x1_ref[...] = x1

    xh = x1.astype(BF16)
    xl = (x1 - xh.astype(F32)).astype(BF16)
    logits = _dot(xh, wrh_ref[...]) + (_dot(xl, wrh_ref[...]) + _dot(xh, wrl_ref[...])) + br_ref[...]
    lane = lax.broadcasted_iota(jnp.int32, logits.shape, 1)
    is_grp = (lane >= N_EXPERTS) & (lane < N_EXPERTS + N_GROUPS)
    gl = jnp.where(is_grp, logits, NEG_INF)
    gmax = jnp.max(gl, axis=1, keepdims=True)
    grp = jnp.min(jnp.where(gl == gmax, lane, 4 * LANES), axis=1, keepdims=True) - N_EXPERTS
    p_grp = 1.0 / jnp.sum(jnp.exp(gl - gmax), axis=1, keepdims=True)
    in_grp = (lane < N_EXPERTS) & ((lane >> 3) == grp)
    el = jnp.where(in_grp, logits, NEG_INF)
    emax = jnp.max(el, axis=1, keepdims=True)
    ee = jnp.exp(el - emax)
    pe = ee / jnp.sum(ee, axis=1, keepdims=True)
    pe = jnp.where(in_grp, pe, -1.0)
    p1 = jnp.max(pe, axis=1, keepdims=True)
    i1 = jnp.min(jnp.where(pe == p1, lane, 4 * LANES), axis=1, keepdims=True)
    pe2 = jnp.where(lane == i1, -1.0, pe)
    p2 = jnp.max(pe2, axis=1, keepdims=True)
    i2 = jnp.min(jnp.where(pe2 == p2, lane, 4 * LANES), axis=1, keepdims=True)
    tot = p1 + p2
    comb_ref[...] = (jnp.where(lane == i1, p_grp * (p1 / tot), 0.0)
                     + jnp.where(lane == i2, p_grp * (p2 / tot), 0.0)
                     + jnp.where(lane == GROUP_LANE, grp.astype(F32), 0.0))


def _mix(x, c, o, g0, b0, wg, bg, wco, wao, wo, g1, b1, wrh, wrl, br, alpha, tm):
    n = x.shape[0]
    row = lambda i: (i, 0)
    full = lambda i: (0, 0)
    fs = lambda a: pl.BlockSpec(a.shape, full)
    return pl.pallas_call(
        functools.partial(_mix_kernel, alpha=alpha),
        grid=(n // tm,),
        in_specs=[pl.BlockSpec((tm, D_MODEL), row), pl.BlockSpec((tm, CONV_CH), row),
                  pl.BlockSpec((tm, ATT_W), row)] + [fs(a) for a in (g0, b0, wg, bg, wco, wao, wo, g1, b1, wrh, wrl, br)],
        out_specs=[pl.BlockSpec((tm, D_MODEL), row), pl.BlockSpec((tm, LANES), row)],
        out_shape=[jax.ShapeDtypeStruct((n, D_MODEL), F32), jax.ShapeDtypeStruct((n, LANES), F32)],
        compiler_params=_cparams(("parallel",)),
        name="mix",
    )(x, c, o, g0, b0, wg, bg, wco, wao, wo, g1, b1, wrh, wrl, br)


def _split3(a):
    hi = a.astype(BF16)
    r1 = a - hi.astype(F32)
    mid = r1.astype(BF16)
    lo = (r1 - mid.astype(F32)).astype(BF16)
    return hi, mid, lo


def _permute_rows(perm, a):
    hi, mid, lo = _split3(a)
    return _dot(perm, hi) + _dot(perm, mid) + _dot(perm, lo)


def _ffn_kernel(x1_ref, comb_ref, p_ref, wpg_ref, wpp_ref, weg_ref, weu_ref, wed_ref, g2_ref, b2_ref,
                y_ref, to_slot_ref, to_token_ref, xs_ref, cs_ref, acc_ref, meta_ref,
                *, alpha, tm, rb, eps, slots):
    s = pl.program_id(1)

    @pl.when(s == 0)
    def _():
        comb = comb_ref[...]
        lane = lax.broadcasted_iota(jnp.int32, (tm, LANES), 1)
        lane_row = lax.broadcasted_iota(jnp.int32, (1, LANES), 1)
        grp = comb[:, GROUP_LANE:GROUP_LANE + 1]
        onehot = jnp.where((lane.astype(F32) == grp) & (lane < N_GROUPS), 1.0, 0.0)
        r = lax.broadcasted_iota(jnp.int32, (tm, tm), 0)
        c = lax.broadcasted_iota(jnp.int32, (tm, tm), 1)
        earlier = jnp.where(c < r, 1.0, 0.0).astype(BF16)
        before = _dot(earlier, onehot.astype(BF16))
        cnt = jnp.sum(onehot, axis=0, keepdims=True)
        base = jnp.zeros((1, LANES), F32)
        run = jnp.zeros((1, 1), F32)
        for g in range(N_GROUPS):
            base = base + jnp.where(lane_row == g, run, 0.0)
            meta_ref[g] = run[0, 0].astype(jnp.int32)
            meta_ref[N_GROUPS + g] = cnt[0, g].astype(jnp.int32)
            run = jnp.floor((run + cnt[:, g:g + 1] + (GROUP_ALIGN - 1.0)) * (1.0 / GROUP_ALIGN)) * GROUP_ALIGN
        slot = jnp.sum(onehot * (before + base), axis=1, keepdims=True)
        slot_row = jnp.transpose(jnp.broadcast_to(slot, (tm, LANES)))[0:1, :]
        slot_c = lax.broadcasted_iota(jnp.int32, (tm, slots), 1).astype(F32)
        slot_r = lax.broadcasted_iota(jnp.int32, (slots, tm), 0).astype(F32)
        to_slot_ref[...] = jnp.where(slot == slot_c, 1.0, 0.0).astype(BF16)
        to_token = jnp.where(slot_row == slot_r, 1.0, 0.0).astype(BF16)
        to_token_ref[...] = to_token
        xs_ref[0:slots, :] = _dot(to_token, x1_ref[...].astype(BF16)).astype(BF16)
        cs_ref[0:slots, :] = _permute_rows(to_token, comb)
        xs_ref[slots:, :] = jnp.zeros((rb, D_MODEL), BF16)
        cs_ref[slots:, :] = jnp.zeros((rb, LANES), F32)
        acc_ref[...] = jnp.zeros((slots + rb, D_MODEL), F32)

    g = (s * eps) // EXPERTS_PER_GROUP
    first = meta_ref[g]
    count = meta_ref[N_GROUPS + g]
    for blk in range(-(-tm // rb)):
        @pl.when(blk * rb < count)
        def _():
            rows = pl.ds(pl.multiple_of(first + blk * rb, GROUP_ALIGN), rb)
            xb = xs_ref[rows, :]
            cs = cs_ref[rows, :]
            lane = lax.broadcasted_iota(jnp.int32, (rb, LANES), 1)
            out = jnp.zeros((rb, D_MODEL), F32)
            for j in range(eps):
                ce = jnp.sum(jnp.where(lane == s * eps + j, cs, 0.0), axis=1, keepdims=True)
                hg = _dot(xb, weg_ref[j])
                hu = _dot(xb, weu_ref[j])
                hidden = (hg * _sigmoid(hg)) * hu * ce
                out = out + _dot(hidden.astype(BF16), wed_ref[j])
            acc_ref[rows, :] += out

    @pl.when(s == N_EXPERTS // eps - 1)
    def _():
        x1 = x1_ref[...]
        ffn = _permute_rows(to_slot_ref[...], acc_ref[0:slots, :])
        ple = _sigmoid(_dot(x1.astype(BF16), wpg_ref[...])) * _dot(p_ref[...].astype(BF16), wpp_ref[...])
        y_ref[...] = _layer_norm(alpha * x1 + ffn + ple, g2_ref[...], b2_ref[...])


def _ffn(x1, comb, p, wpg, wpp, weg, weu, wed, g2, b2, alpha, tm, rb, eps):
    n = x1.shape[0]
    assert rb % GROUP_ALIGN == 0 and EXPERTS_PER_GROUP % eps == 0
    slots = -(-(tm + N_GROUPS * GROUP_ALIGN) // LANES) * LANES
    row = lambda i, e: (i, 0)
    full = lambda i, e: (0, 0)
    exp = lambda i, e: (e, 0, 0)
    return pl.pallas_call(
        functools.partial(_ffn_kernel, alpha=alpha, tm=tm, rb=rb, eps=eps, slots=slots),
        grid=(n // tm, N_EXPERTS // eps),
        in_specs=[pl.BlockSpec((tm, D_MODEL), row), pl.BlockSpec((tm, LANES), row),
                  pl.BlockSpec((tm, PLE_DIM), row),
                  pl.BlockSpec((D_MODEL, D_MODEL), full), pl.BlockSpec((PLE_DIM, D_MODEL), full),
                  pl.BlockSpec((eps, D_MODEL, D_EXPERT), exp), pl.BlockSpec((eps, D_MODEL, D_EXPERT), exp),
                  pl.BlockSpec((eps, D_EXPERT, D_MODEL), exp),
                  pl.BlockSpec((1, D_MODEL), full), pl.BlockSpec((1, D_MODEL), full)],
        out_specs=pl.BlockSpec((tm, D_MODEL), row),
        out_shape=jax.ShapeDtypeStruct((n, D_MODEL), F32),
        scratch_shapes=[pltpu.VMEM((tm, slots), BF16), pltpu.VMEM((slots, tm), BF16),
                        pltpu.VMEM((slots + rb, D_MODEL), BF16), pltpu.VMEM((slots + rb, LANES), F32),
                        pltpu.VMEM((slots + rb, D_MODEL), F32), pltpu.SMEM((2 * N_GROUPS,), jnp.int32)],
        compiler_params=pltpu.CompilerParams(dimension_semantics=("parallel", "arbitrary"),
                                             vmem_limit_bytes=VMEM_LIMIT_FFN),
        name="ffn",
    )(x1, comb, p, wpg, wpp, weg, weu, wed, g2, b2)


def _tile(n, target):
    t = min(n, target)
    assert n % t == 0, (n, t)
    return t


def _row(v):
    return v.reshape(1, -1).astype(F32)


def kernel(x_prompt, x_sample, cache_k, cache_v, cache_kidx, state_conv, page_table, p_prompt, p_sample,
           ln0_g, ln0_b, w_in, b_gate, conv_w, conv_b, lnc_g, lnc_b, w_conv_out, w_attn_out, w_o,
           ln1_g, ln1_b, w_rg, b_rg, w_re, b_re, w_eg, w_eu, w_ed, w_pg, w_pp, ln2_g, ln2_b):
    depth = w_in.shape[0]
    assert depth == 1, "single-layer step"
    assert w_in.shape[1:] == (D_MODEL, COL_END)
    layer = 0
    alpha = (2.0 * depth) ** 0.25
    bp, tp, _ = x_prompt.shape
    bs, ts, _ = x_sample.shape
    assert ts == SUBLANES, "sample step length must fill one sublane tile"
    page = cache_k.shape[2]
    n_pages = page_table.shape[1]
    assert page == LANES

    g0, b0 = _row(ln0_g), _row(ln0_b)
    w = w_in[layer]
    wi_cols = jnp.pad(w[:, COL_WI:COL_GC], ((0, 0), (0, LANES - N_IDX_HEADS)))
    ws = jnp.concatenate([w[:, COL_GLU_A:COL_K], w[:, COL_QI:COL_KI], wi_cols], axis=1).astype(BF16)
    wt = w[:, COL_K:COL_QI].T
    wt = jnp.concatenate([wt, w[:, COL_KI:COL_WI].T], axis=0).astype(BF16)
    wg = w[:, COL_GC:].astype(BF16)
    bg = _row(b_gate[layer])
    wr = jnp.pad(jnp.concatenate([w_re[layer], w_rg[layer]], axis=1), ((0, 0), (0, LANES - N_EXPERTS - N_GROUPS)))
    wrh = wr.astype(BF16)
    wrl = (wr - wrh.astype(F32)).astype(BF16)
    br = jnp.pad(jnp.concatenate([b_re[layer], b_rg[layer]]), (0, LANES - N_EXPERTS - N_GROUPS)).reshape(1, LANES)
    wco, wao, wo = w_conv_out[layer].astype(BF16), w_attn_out[layer].astype(BF16), w_o[layer].astype(BF16)
    wpg, wpp = w_pg[layer].astype(BF16), w_pp[layer].astype(BF16)
    weg, weu, wed = w_eg[layer].astype(BF16), w_eu[layer].astype(BF16), w_ed[layer].astype(BF16)
    cw, cb = conv_w[layer], _row(conv_b[layer])
    lcg, lcb = _row(lnc_g[layer]), _row(lnc_b[layer])
    g1, b1, g2, b2 = _row(ln1_g[layer]), _row(ln1_b[layer]), _row(ln2_g[layer]), _row(ln2_b[layer])
    cache_kt = jnp.transpose(cache_k, (0, 1, 3, 4, 2))
    cache_vt = jnp.transpose(cache_v, (0, 1, 3, 4, 2))
    cache_kidx_t = jnp.transpose(cache_kidx, (0, 1, 3, 2))

    def tail(x1, comb, p):
        n = x1.shape[0]
        tm = _tile(n, 1024)
        rb = min(tm, tm // N_GROUPS + 2 * GROUP_ALIGN)
        return _ffn(x1, comb, p.reshape(n, PLE_DIM), wpg, wpp, weg, weu, wed, g2, b2, alpha, tm, rb, 4)

    def mix(x, c, o):
        n = x.shape[0]
        return _mix(x, c, o, g0, b0, wg, bg, wco, wao, wo, g1, b1, wrh, wrl, br, alpha, _tile(n, 512))

    np_ = bp * tp
    xp = x_prompt.reshape(np_, D_MODEL)
    glu, q, qi, wi, kt, vt, ktb, vtb, kit, kitb = _proj(xp, g0, b0, ws, wt, bp, tp, _tile(tp, 512))
    c, newconv_p = _conv(glu.reshape(bp, tp, CONV_CH), None, cw, cb, lcg, lcb, _tile(tp, 256))
    tq = _tile(tp, 256)
    o = jnp.concatenate(
        [_attn_prompt(qi, wi, q, kitb, ktb, vtb, bp, tp, tq, i, 1, (i + 1) * tq) for i in range(tp // tq)],
        axis=1).reshape(np_, ATT_W)
    x1, comb = mix(xp, c.reshape(np_, CONV_CH), o)
    y_prompt = tail(x1, comb, p_prompt[layer]).reshape(bp, tp, D_MODEL)
    new_k_p = jnp.transpose(kt, (0, 3, 1, 2))[None]
    new_v_p = jnp.transpose(vt, (0, 3, 1, 2))[None]
    new_ki_p = jnp.transpose(kit, (0, 2, 1))[None]

    ns = bs * ts
    xs = x_sample.reshape(ns, D_MODEL)
    glu, q, qi, wi, kt, vt, _, _, kit, kitb = _proj(xs, g0, b0, ws, wt, 1, ns, ns)
    c, newconv_s = _conv(glu.reshape(bs, ts, CONV_CH), state_conv[layer], cw, cb, lcg, lcb, ts)
    to_rows = lambda a: a.reshape(N_HEADS, bs, ts, HEAD_DIM).transpose(1, 0, 2, 3).reshape(bs, N_HEADS * ts, HEAD_DIM)
    qi_s, q_s = to_rows(qi), to_rows(q)
    wi_col = wi.reshape(bs, ts, N_IDX_HEADS).transpose(0, 2, 1).reshape(bs, N_IDX_HEADS * ts, 1)
    kt_s = kt.reshape(N_HEADS, HEAD_DIM, bs, ts)
    vt_s = vt.reshape(N_HEADS, HEAD_DIM, bs, ts)
    kit_s = kit.reshape(IDX_DIM, bs, ts)
    pad_keys = lambda a: jnp.pad(a, ((0, 0),) * (a.ndim - 1) + ((0, page - ts),))
    pt_flat = page_table.reshape(-1).astype(jnp.int32)
    pages_per = lambda target: max(d for d in range(1, target + 1) if n_pages % d == 0)
    scores_past, scores_new = _sample_scores(
        pt_flat, qi_s, wi_col, pad_keys(kitb.reshape(IDX_DIM, bs, ts).transpose(1, 0, 2)),
        cache_kidx_t, layer, pages_per(64))
    selp, seln = _sample_select(scores_past, scores_new, ts, _tile(ns, 64))
    o = _sample_attend(pt_flat, q_s, selp.reshape(bs, ts, -1), seln.reshape(bs, ts, LANES), cache_kt, cache_vt,
                       pad_keys(kt_s.transpose(2, 0, 1, 3)), pad_keys(vt_s.transpose(2, 0, 1, 3)),
                       layer, pages_per(32))
    x1, comb = mix(xs, c.reshape(ns, CONV_CH), o.reshape(ns, ATT_W).astype(BF16))
    y_sample = tail(x1, comb, p_sample[layer]).reshape(bs, ts, D_MODEL)
    new_k_s = kt_s.transpose(2, 3, 0, 1)[None]
    new_v_s = vt_s.transpose(2, 3, 0, 1)[None]
    new_ki_s = kit_s.transpose(1, 2, 0)[None]

    return (y_prompt, y_sample, new_k_p, new_v_p, new_ki_p, newconv_p[None],
            new_k_s, new_v_s, new_ki_s, newconv_s[None])
```

```python
import functools

import jax
import jax.numpy as jnp
from jax import lax
from jax.experimental import pallas as pl
from jax.experimental.pallas import tpu as pltpu

D_MODEL = 1024
CONV_CH = 512
CONV_WIDTH = 31
CONV_HIST = CONV_WIDTH - 1
N_HEADS = 8
HEAD_DIM = 64
ATT_W = N_HEADS * HEAD_DIM
N_IDX_HEADS = 8
IDX_DIM = 64
IDX_TOPK_MAX = 256
N_GROUPS = 4
EXPERTS_PER_GROUP = 8
N_EXPERTS = N_GROUPS * EXPERTS_PER_GROUP
D_EXPERT = 256
PLE_DIM = 256
LN_EPS = 1e-5
IDX_SCALE = (N_IDX_HEADS * IDX_DIM) ** -0.5
ATT_SCALE = HEAD_DIM ** -0.5
COL_GLU_A, COL_GLU_B, COL_Q, COL_K, COL_V, COL_QI, COL_KI, COL_WI, COL_GC = (
    0, 512, 1024, 1536, 2048, 2560, 3072, 3136, 3144)
COL_END = COL_GC + 2 * D_MODEL

LANES = 128
SUBLANES = 8
PREFIX_ROWS = 32
GROUP_LANE = LANES - 1
GROUP_ALIGN = 16
SEARCH_PARTS = 4
VMEM_LIMIT = 48 * 1024 * 1024
VMEM_LIMIT_FFN = 60 * 1024 * 1024

F32 = jnp.float32
BF16 = jnp.bfloat16
NEG_INF = float("-inf")
INT_MIN = -2 ** 31


def _cparams(sem):
    return pltpu.CompilerParams(dimension_semantics=sem, vmem_limit_bytes=VMEM_LIMIT)


def _layer_norm(x, g, b):
    mu = jnp.mean(x, axis=-1, keepdims=True)
    xc = x - mu
    var = jnp.mean(xc * xc, axis=-1, keepdims=True)
    return xc * lax.rsqrt(var + LN_EPS) * g + b


def _sigmoid(x):
    return 1.0 / (1.0 + jnp.exp(-x))


def _dot(a, b):
    return jnp.dot(a, b, preferred_element_type=F32)


def _dot_nt(a, b):
    return lax.dot_general(a, b, (((1,), (1,)), ((), ())), preferred_element_type=F32)


def _proj_kernel(x_ref, g0_ref, b0_ref, ws_ref, wt_ref,
                 glu_ref, q_ref, qi_ref, wi_ref, kt_ref, vt_ref, ktb_ref, vtb_ref, kit_ref, kitb_ref):
    h = _layer_norm(x_ref[...], g0_ref[...], b0_ref[...])
    hb = h.astype(BF16)

    def seg(col, width=512):
        return _dot(hb, ws_ref[:, col:col + width])

    glu_ref[...] = seg(0) * _sigmoid(seg(512))
    q = seg(1024) * ATT_SCALE
    qi = seg(1536)
    for hd in range(N_HEADS):
        sl = slice(hd * HEAD_DIM, (hd + 1) * HEAD_DIM)
        q_ref[hd] = q[:, sl].astype(BF16)
        qi_ref[hd] = qi[:, sl].astype(BF16)
    wi_ref[...] = seg(2048, LANES)[:, :N_IDX_HEADS]

    tm = hb.shape[0]
    kt = _dot_nt(wt_ref[0:ATT_W, :], hb).reshape(N_HEADS, HEAD_DIM, tm)
    kt_ref[...] = kt
    ktb_ref[...] = kt.astype(BF16)
    vt = _dot_nt(wt_ref[ATT_W:2 * ATT_W, :], hb).reshape(N_HEADS, HEAD_DIM, tm)
    vt_ref[...] = vt
    vtb_ref[...] = vt.astype(BF16)
    kit = _dot_nt(wt_ref[2 * ATT_W:2 * ATT_W + IDX_DIM, :], hb)
    kit_ref[...] = kit
    kitb_ref[...] = kit.astype(BF16)


def _proj(x, g0, b0, ws, wt, b, t, tm):
    n = b * t
    nt = t // tm
    row = lambda i: (i, 0)
    head = lambda i: (0, i, 0)
    full = lambda i: (0, 0)
    hm = jax.ShapeDtypeStruct((N_HEADS, n, HEAD_DIM), BF16)
    hm_spec = pl.BlockSpec((N_HEADS, tm, HEAD_DIM), head)
    kv_spec = pl.BlockSpec((None, N_HEADS, HEAD_DIM, tm), lambda i: (i // nt, 0, 0, i % nt))
    ki_spec = pl.BlockSpec((None, IDX_DIM, tm), lambda i: (i // nt, 0, i % nt))
    kv_shape = lambda dt: jax.ShapeDtypeStruct((b, N_HEADS, HEAD_DIM, t), dt)
    ki_shape = lambda dt: jax.ShapeDtypeStruct((b, IDX_DIM, t), dt)
    return pl.pallas_call(
        _proj_kernel,
        grid=(n // tm,),
        in_specs=[pl.BlockSpec((tm, D_MODEL), row),
                  pl.BlockSpec((1, D_MODEL), full), pl.BlockSpec((1, D_MODEL), full),
                  pl.BlockSpec(ws.shape, full), pl.BlockSpec(wt.shape, full)],
        out_specs=[pl.BlockSpec((tm, CONV_CH), row), hm_spec, hm_spec,
                   pl.BlockSpec((tm, N_IDX_HEADS), row),
                   kv_spec, kv_spec, kv_spec, kv_spec, ki_spec, ki_spec],
        out_shape=[jax.ShapeDtypeStruct((n, CONV_CH), F32), hm, hm,
                   jax.ShapeDtypeStruct((n, N_IDX_HEADS), F32),
                   kv_shape(F32), kv_shape(F32), kv_shape(BF16), kv_shape(BF16), ki_shape(F32), ki_shape(BF16)],
        compiler_params=_cparams(("parallel",)),
        name="proj",
    )(x, g0, b0, ws, wt)


def _conv_kernel(*refs, tc, nt, has_state):
    it = iter(refs)
    glu_ref = next(it)
    prev_ref = next(it) if nt > 1 else None
    state_ref = next(it) if has_state else None
    w_ref, cb_ref, g_ref, b_ref, c_ref, newconv_ref, buf_ref, shift_ref, acc_ref = it
    t = pl.program_id(1)

    @pl.when(t == 0)
    def _():
        buf_ref[0:PREFIX_ROWS, :] = jnp.zeros((PREFIX_ROWS, CONV_CH), F32)
        if has_state:
            buf_ref[2:PREFIX_ROWS, :] = state_ref[...]

    if nt > 1:
        @pl.when(t > 0)
        def _():
            buf_ref[0:PREFIX_ROWS, :] = prev_ref[...]

    buf_ref[PREFIX_ROWS:PREFIX_ROWS + tc, :] = glu_ref[...]

    shifted_rows = tc + PREFIX_ROWS - SUBLANES
    for s in range(1, SUBLANES):
        shift_ref[s - 1] = buf_ref[s:s + shifted_rows, :]

    rb = min(tc, 32)
    for c0 in range(0, CONV_CH, LANES):
        cs = slice(c0, c0 + LANES)
        bias = cb_ref[:, cs]
        for r0 in range(0, tc, rb):
            acc = jnp.zeros((rb, LANES), F32) + bias
            for j in range(CONV_WIDTH):
                a, s = divmod(2 + j, SUBLANES)
                lo = r0 + a * SUBLANES
                rows = buf_ref[lo:lo + rb, cs] if s == 0 else shift_ref[s - 1, lo:lo + rb, cs]
                acc = acc + w_ref[j:j + 1, cs] * rows
            acc_ref[r0:r0 + rb, cs] = acc

    y = _layer_norm(acc_ref[...], g_ref[...], b_ref[...])
    c_ref[...] = (y * _sigmoid(y)).astype(BF16)

    @pl.when(t == nt - 1)
    def _():
        newconv_ref[...] = buf_ref[tc + 2:tc + PREFIX_ROWS, :]


def _conv(glu, state, conv_w, conv_b, lnc_g, lnc_b, tc):
    b, t, _ = glu.shape
    nt = t // tc
    blocks_per_tile = tc // PREFIX_ROWS
    in_specs = [pl.BlockSpec((None, tc, CONV_CH), lambda i, j: (i, j, 0))]
    args = [glu]
    if nt > 1:
        in_specs.append(pl.BlockSpec((None, PREFIX_ROWS, CONV_CH),
                                     lambda i, j: (i, jnp.maximum(j * blocks_per_tile - 1, 0), 0)))
        args.append(glu)
    if state is not None:
        in_specs.append(pl.BlockSpec((None, CONV_HIST, CONV_CH), lambda i, j: (i, 0, 0)))
        args.append(state)
    full = lambda i, j: (0, 0)
    in_specs += [pl.BlockSpec((CONV_WIDTH, CONV_CH), full), pl.BlockSpec((1, CONV_CH), full),
                 pl.BlockSpec((1, CONV_CH), full), pl.BlockSpec((1, CONV_CH), full)]
    args += [conv_w, conv_b, lnc_g, lnc_b]
    return pl.pallas_call(
        functools.partial(_conv_kernel, tc=tc, nt=nt, has_state=state is not None),
        grid=(b, nt),
        in_specs=in_specs,
        out_specs=[pl.BlockSpec((None, tc, CONV_CH), lambda i, j: (i, j, 0)),
                   pl.BlockSpec((None, CONV_HIST, CONV_CH), lambda i, j: (i, 0, 0))],
        out_shape=[jax.ShapeDtypeStruct((b, t, CONV_CH), BF16),
                   jax.ShapeDtypeStruct((b, CONV_HIST, CONV_CH), F32)],
        scratch_shapes=[pltpu.VMEM((PREFIX_ROWS + tc, CONV_CH), F32),
                        pltpu.VMEM((SUBLANES - 1, tc + PREFIX_ROWS - SUBLANES, CONV_CH), F32),
                        pltpu.VMEM((tc, CONV_CH), F32)],
        compiler_params=_cparams(("parallel", "arbitrary")),
        name="conv",
    )(*args)


def _ordered_to_f32(c):
    return pltpu.bitcast(c ^ ((c >> 31) & jnp.int32(0x7FFFFFFF)), F32)


def _kth_largest(count_ge_parts, rows, topk):
    def body(i, taus):
        bit = lax.shift_left(jnp.int32(1), jnp.int32(31) - i)
        out = []
        for count_ge, tau in zip(count_ge_parts, taus):
            cand = tau ^ bit
            out.append(jnp.where(count_ge(_ordered_to_f32(cand)) >= topk, cand, tau))
        return tuple(out)
    init = tuple(jnp.full((rows, 1), INT_MIN, jnp.int32) for _ in count_ge_parts)
    return [_ordered_to_f32(t) for t in lax.fori_loop(0, 32, body, init, unroll=16)]


def _count(mask):
    return jnp.sum(mask.astype(jnp.int32), axis=1, keepdims=True)


def _tie_rank(eq_chunks, rows):
    r = lax.broadcasted_iota(jnp.int32, (LANES, LANES), 0)
    c = lax.broadcasted_iota(jnp.int32, (LANES, LANES), 1)
    upper = jnp.where(r < c, 1.0, 0.0).astype(BF16)
    prefix = jnp.zeros((rows, 1), F32)
    for idx, eq in eq_chunks:
        eqf = jnp.where(eq, 1.0, 0.0)
        yield idx, _dot(eqf.astype(BF16), upper) + prefix
        prefix = prefix + jnp.sum(eqf, axis=1, keepdims=True)


def _attn_prompt_kernel(qi_ref, wi_ref, q_ref, kitb_ref, ktb_ref, vtb_ref, o_ref, sc_ref, bias_ref,
                        *, tq, t_eff, topk, first_block):
    qb = first_block + pl.program_id(1)
    wi = wi_ref[...]
    kitb = kitb_ref[...]
    acc = jnp.zeros((tq, t_eff), F32)
    for hd in range(N_IDX_HEADS):
        acc = acc + jnp.maximum(_dot(qi_ref[hd], kitb), 0.0) * wi[:, hd:hd + 1]
    q_pos = qb * tq + lax.broadcasted_iota(jnp.int32, (tq, 1), 0)
    k_pos = lax.broadcasted_iota(jnp.int32, (1, t_eff), 1)
    adm = k_pos <= q_pos
    sc_ref[...] = jnp.where(adm, acc * IDX_SCALE, NEG_INF)

    take_all = (q_pos + 1) <= topk
    if t_eff <= topk:
        tau = sc_ref[:, 0:1]
    else:
        part = tq // SEARCH_PARTS
        tau = jnp.concatenate(
            _kth_largest([functools.partial(lambda lo, cand: _count(sc_ref[lo:lo + part, :] >= cand), i * part)
                          for i in range(SEARCH_PARTS)], part, topk), axis=0)
    sc = sc_ref[...]
    eq = (sc == tau) & adm
    need = topk - _count(sc > tau)
    excess = jnp.max(jnp.where(take_all, 0, _count(eq) - need)) > 0

    @pl.when(jnp.logical_not(excess))
    def _():
        bias_ref[...] = jnp.where(adm & ((sc >= tau) | take_all), 0.0, NEG_INF)

    @pl.when(excess)
    def _():
        needf = need.astype(F32)
        chunks = [(i, (sc_ref[:, i * LANES:(i + 1) * LANES] == tau) & adm[:, i * LANES:(i + 1) * LANES])
                  for i in range(t_eff // LANES)]
        for i, rank in _tie_rank(chunks, tq):
            cs = slice(i * LANES, (i + 1) * LANES)
            kc = sc_ref[:, cs]
            sel = ((kc > tau) | ((kc == tau) & (rank < needf)) | take_all) & adm[:, cs]
            bias_ref[:, cs] = jnp.where(sel, 0.0, NEG_INF)

    bias = bias_ref[...]
    for hd in range(N_HEADS):
        logit = _dot(q_ref[hd], ktb_ref[hd]) + bias
        m = jnp.max(logit, axis=1, keepdims=True)
        p = jnp.exp(logit - m)
        denom = jnp.sum(p, axis=1, keepdims=True)
        o = _dot_nt(p.astype(BF16), vtb_ref[hd]) / denom
        o_ref[:, hd * HEAD_DIM:(hd + 1) * HEAD_DIM] = o.astype(BF16)


def _attn_prompt(qi, wi, q, kitb, ktb, vtb, b, t, tq, first_block, n_blocks, t_eff):
    nq = t // tq
    topk = min(IDX_TOPK_MAX, t // 4)
    assert (first_block + n_blocks) * tq <= t_eff
    qrow = lambda i, j: i * nq + first_block + j
    qspec = pl.BlockSpec((N_HEADS, tq, HEAD_DIM), lambda i, j: (0, qrow(i, j), 0))
    kspec = pl.BlockSpec((None, N_HEADS, HEAD_DIM, t_eff), lambda i, j: (i, 0, 0, 0))
    return pl.pallas_call(
        functools.partial(_attn_prompt_kernel, tq=tq, t_eff=t_eff, topk=topk, first_block=first_block),
        grid=(b, n_blocks),
        in_specs=[qspec, pl.BlockSpec((tq, N_IDX_HEADS), lambda i, j: (qrow(i, j), 0)), qspec,
                  pl.BlockSpec((None, IDX_DIM, t_eff), lambda i, j: (i, 0, 0)), kspec, kspec],
        out_specs=pl.BlockSpec((None, tq, ATT_W), lambda i, j: (i, j, 0)),
        out_shape=jax.ShapeDtypeStruct((b, n_blocks * tq, ATT_W), BF16),
        scratch_shapes=[pltpu.VMEM((tq, t_eff), F32), pltpu.VMEM((tq, t_eff), F32)],
        compiler_params=_cparams(("parallel", "arbitrary")),
        name="attn_prompt",
    )(qi, wi, q, kitb, ktb, vtb)


def _indexer_rows(qi, keys_t, wi_col):
    s = jnp.maximum(_dot(qi, keys_t), 0.0) * wi_col
    n = s.shape[1]
    return jnp.sum(s.reshape(N_IDX_HEADS, SUBLANES, n), axis=0) * IDX_SCALE


def _sample_scores_kernel(pt_ref, qi_ref, wi_ref, kin_ref, *refs, pages_per_step, page):
    del pt_ref
    page_refs, out_ref, new_ref = refs[:pages_per_step], refs[pages_per_step], refs[pages_per_step + 1]
    qi = qi_ref[...]
    wi_col = wi_ref[...]
    chunk = max(d for d in range(1, SUBLANES + 1) if pages_per_step % d == 0)
    for i in range(0, pages_per_step, chunk):
        keys = jnp.concatenate([page_refs[i + j][...].astype(BF16) for j in range(chunk)], axis=1)
        out_ref[:, i * page:(i + chunk) * page] = _indexer_rows(qi, keys, wi_col)

    @pl.when(pl.program_id(1) == 0)
    def _():
        new_ref[...] = _indexer_rows(qi, kin_ref[...], wi_col)


def _sample_scores(page_table, qi_s, wi_col, kitb_new_pad, cache_kidx_t, layer, pages_per_step):
    bsz = qi_s.shape[0]
    n_pages = page_table.shape[0] // bsz
    page = cache_kidx_t.shape[-1]
    steps = n_pages // pages_per_step

    def page_spec(i):
        return pl.BlockSpec((None, None, IDX_DIM, page),
                            lambda b, g, pt: (layer, pt[b * n_pages + g * pages_per_step + i], 0, 0))

    grid_spec = pltpu.PrefetchScalarGridSpec(
        num_scalar_prefetch=1,
        grid=(bsz, steps),
        in_specs=[pl.BlockSpec((None, N_IDX_HEADS * SUBLANES, IDX_DIM), lambda b, g, pt: (b, 0, 0)),
                  pl.BlockSpec((None, N_IDX_HEADS * SUBLANES, 1), lambda b, g, pt: (b, 0, 0)),
                  pl.BlockSpec((None, IDX_DIM, LANES), lambda b, g, pt: (b, 0, 0))]
                 + [page_spec(i) for i in range(pages_per_step)],
        out_specs=[pl.BlockSpec((SUBLANES, pages_per_step * page), lambda b, g, pt: (b, g)),
                   pl.BlockSpec((SUBLANES, LANES), lambda b, g, pt: (b, 0))],
    )
    return pl.pallas_call(
        functools.partial(_sample_scores_kernel, pages_per_step=pages_per_step, page=page),
        grid_spec=grid_spec,
        out_shape=[jax.ShapeDtypeStruct((bsz * SUBLANES, n_pages * page), F32),
                   jax.ShapeDtypeStruct((bsz * SUBLANES, LANES), F32)],
        compiler_params=_cparams(("parallel", "arbitrary")),
        name="sample_scores",
    )(page_table, qi_s, wi_col, kitb_new_pad, *([cache_kidx_t] * pages_per_step))


def _sample_select_kernel(sp_ref, sn_ref, selp_ref, seln_ref, *, rows, t_new, past, topk):
    jn = lax.broadcasted_iota(jnp.int32, (rows, LANES), 1)
    tn = lax.broadcasted_iota(jnp.int32, (rows, LANES), 0) & (SUBLANES - 1)
    adm_n = (jn <= tn) & (jn < t_new)
    sn = jnp.where(adm_n, sn_ref[...], NEG_INF)

    half = rows // 2
    tau = jnp.concatenate(
        _kth_largest([lambda cand: _count(sp_ref[0:half, :] >= cand) + _count(sn[0:half] >= cand),
                      lambda cand: _count(sp_ref[half:rows, :] >= cand) + _count(sn[half:rows] >= cand)],
                     half, topk), axis=0)
    sp = sp_ref[...]
    need = topk - (_count(sp > tau) + _count(sn > tau))
    eqn = (sn == tau) & adm_n
    excess = jnp.max(_count(sp == tau) + _count(eqn) - need) > 0

    @pl.when(jnp.logical_not(excess))
    def _():
        selp_ref[...] = jnp.where(sp >= tau, 1.0, 0.0)
        seln_ref[...] = jnp.where((sn >= tau) & adm_n, 1.0, 0.0)

    @pl.when(excess)
    def _():
        needf = need.astype(F32)
        n_chunks = past // LANES
        chunks = [(i, sp_ref[:, i * LANES:(i + 1) * LANES] == tau) for i in range(n_chunks)]
        chunks.append((n_chunks, eqn))
        for i, rank in _tie_rank(chunks, rows):
            if i < n_chunks:
                kc = sp_ref[:, i * LANES:(i + 1) * LANES]
                sel = (kc > tau) | ((kc == tau) & (rank < needf))
                selp_ref[:, i * LANES:(i + 1) * LANES] = jnp.where(sel, 1.0, 0.0)
            else:
                sel = ((sn > tau) | ((sn == tau) & (rank < needf))) & adm_n
                seln_ref[...] = jnp.where(sel, 1.0, 0.0)


def _sample_select(scores_past, scores_new, t_new, rows):
    n, past = scores_past.shape
    assert past >= IDX_TOPK_MAX, "every query must see at least top-k admissible keys"
    assert t_new == SUBLANES and n % rows == 0
    topk = min(IDX_TOPK_MAX, (past + t_new) // 4)
    blk = lambda w: pl.BlockSpec((rows, w), lambda i: (i, 0))
    return pl.pallas_call(
        functools.partial(_sample_select_kernel, rows=rows, t_new=t_new, past=past, topk=topk),
        grid=(n // rows,),
        in_specs=[blk(past), blk(LANES)],
        out_specs=[blk(past), blk(LANES)],
        out_shape=[jax.ShapeDtypeStruct((n, past), F32), jax.ShapeDtypeStruct((n, LANES), F32)],
        compiler_params=_cparams(("parallel",)),
        name="sample_select",
    )(scores_past, scores_new)


def _sample_attend_kernel(pt_ref, q_ref, selp_ref, seln_ref, *refs, n_groups, pages_per_step, page):
    del pt_ref
    g = pages_per_step
    k_refs, v_refs = refs[:g], refs[g:2 * g]
    kn_ref, vn_ref, o_ref, m_ref, l_ref, acc_ref = refs[2 * g:]
    step = pl.program_id(1)
    rows = N_HEADS * SUBLANES

    @pl.when(step == 0)
    def _():
        m_ref[...] = jnp.full((rows, 1), NEG_INF, F32)
        l_ref[...] = jnp.zeros((rows, 1), F32)
        acc_ref[...] = jnp.zeros((rows, ATT_W), F32)

    qrow = lax.broadcasted_iota(jnp.int32, (rows, ATT_W), 0)
    qcol = lax.broadcasted_iota(jnp.int32, (rows, ATT_W), 1)
    q_bd = jnp.where((qcol >> 6) == (qrow >> 3), jnp.concatenate([q_ref[...]] * N_HEADS, axis=1), 0.0).astype(BF16)

    def process(k_pages, v_pages, sel):
        logit = jnp.concatenate([_dot(q_bd, kp.reshape(ATT_W, page).astype(BF16)) for kp in k_pages], axis=1)
        valid = jnp.concatenate([sel] * N_HEADS, axis=0) > 0.5
        logit = jnp.where(valid, logit, NEG_INF)
        m_old = m_ref[...]
        m_new = jnp.maximum(m_old, jnp.max(logit, axis=1, keepdims=True))
        m_safe = jnp.where(m_new == NEG_INF, 0.0, m_new)
        pr = jnp.exp(logit - m_safe)
        alpha = jnp.exp(m_old - m_safe)
        l_ref[...] = alpha * l_ref[...] + jnp.sum(pr, axis=1, keepdims=True)
        prb = pr.astype(BF16)
        pv = jnp.zeros((rows, ATT_W), F32)
        for i, vp in enumerate(v_pages):
            pv = pv + _dot_nt(prb[:, i * page:(i + 1) * page], vp.reshape(ATT_W, page).astype(BF16))
        acc_ref[...] = alpha * acc_ref[...] + pv
        m_ref[...] = m_new

    process([r[...] for r in k_refs], [r[...] for r in v_refs], selp_ref[...])

    @pl.when(step == n_groups - 1)
    def _():
        process([kn_ref[...]], [vn_ref[...]], seln_ref[...])
        out = acc_ref[...] / l_ref[...]
        for hd in range(N_HEADS):
            o_ref[:, hd * HEAD_DIM:(hd + 1) * HEAD_DIM] = out[hd * SUBLANES:(hd + 1) * SUBLANES,
                                                              hd * HEAD_DIM:(hd + 1) * HEAD_DIM]


def _sample_attend(page_table, q_s, selp, seln, cache_kt, cache_vt, kt_new_pad, vt_new_pad, layer, pages_per_step):
    bsz = q_s.shape[0]
    n_pages = page_table.shape[0] // bsz
    page = cache_kt.shape[-1]
    g = pages_per_step
    n_groups = n_pages // g

    def page_spec(i):
        return pl.BlockSpec(
            (None, None, N_HEADS, HEAD_DIM, page),
            lambda b, s, pt: (layer, pt[b * n_pages + s * g + i], 0, 0, 0))

    new_spec = pl.BlockSpec((None, N_HEADS, HEAD_DIM, page), lambda b, s, pt: (b, 0, 0, 0))
    rows = N_HEADS * SUBLANES
    grid_spec = pltpu.PrefetchScalarGridSpec(
        num_scalar_prefetch=1,
        grid=(bsz, n_groups),
        in_specs=[pl.BlockSpec((None, rows, HEAD_DIM), lambda b, s, pt: (b, 0, 0)),
                  pl.BlockSpec((None, SUBLANES, g * page), lambda b, s, pt: (b, 0, s)),
                  pl.BlockSpec((None, SUBLANES, LANES), lambda b, s, pt: (b, 0, 0))]
                 + [page_spec(i) for i in range(g)] + [page_spec(i) for i in range(g)]
                 + [new_spec, new_spec],
        out_specs=pl.BlockSpec((None, SUBLANES, ATT_W), lambda b, s, pt: (b, 0, 0)),
        scratch_shapes=[pltpu.VMEM((rows, 1), F32), pltpu.VMEM((rows, 1), F32), pltpu.VMEM((rows, ATT_W), F32)],
    )
    return pl.pallas_call(
        functools.partial(_sample_attend_kernel, n_groups=n_groups, pages_per_step=g, page=page),
        grid_spec=grid_spec,
        out_shape=jax.ShapeDtypeStruct((bsz, SUBLANES, ATT_W), F32),
        compiler_params=_cparams(("parallel", "arbitrary")),
        name="sample_attend",
    )(page_table, q_s, selp, seln, *([cache_kt] * g), *([cache_vt] * g), kt_new_pad, vt_new_pad)


def _mix_kernel(x_ref, c_ref, o_ref, g0_ref, b0_ref, wg_ref, bg_ref, wco_ref, wao_ref, wo_ref,
                g1_ref, b1_ref, wrh_ref, wrl_ref, br_ref, x1_ref, comb_ref, *, alpha):
    h = _layer_norm(x_ref[...], g0_ref[...], b0_ref[...])
    hb = h.astype(BF16)
    gc = _sigmoid(_dot(hb, wg_ref[:, :D_MODEL]) + bg_ref[:, :D_MODEL])
    ga = _sigmoid(_dot(hb, wg_ref[:, D_MODEL:]) + bg_ref[:, D_MODEL:])
    merged = gc * _dot(c_ref[...], wco_ref[...]) + ga * _dot(o_ref[...], wao_ref[...])
    mix = _dot(merged.astype(BF16), wo_ref[...])
    x1 = _layer_norm(alpha * h + mix, g1_ref[...], b1_ref[...])
    x1_ref[...] = x1

    xh = x1.astype(BF16)
    xl = (x1 - xh.astype(F32)).astype(BF16)
    logits = _dot(xh, wrh_ref[...]) + (_dot(xl, wrh_ref[...]) + _dot(xh, wrl_ref[...])) + br_ref[...]
    lane = lax.broadcasted_iota(jnp.int32, logits.shape, 1)
    is_grp = (lane >= N_EXPERTS) & (lane < N_EXPERTS + N_GROUPS)
    gl = jnp.where(is_grp, logits, NEG_INF)
    gmax = jnp.max(gl, axis=1, keepdims=True)
    grp = jnp.min(jnp.where(gl == gmax, lane, 4 * LANES), axis=1, keepdims=True) - N_EXPERTS
    p_grp = 1.0 / jnp.sum(jnp.exp(gl - gmax), axis=1, keepdims=True)
    in_grp = (lane < N_EXPERTS) & ((lane >> 3) == grp)
    el = jnp.where(in_grp, logits, NEG_INF)
    emax = jnp.max(el, axis=1, keepdims=True)
    ee = jnp.exp(el - emax)
    pe = ee / jnp.sum(ee, axis=1, keepdims=True)
    pe = jnp.where(in_grp, pe, -1.0)
    p1 = jnp.max(pe, axis=1, keepdims=True)
    i1 = jnp.min(jnp.where(pe == p1, lane, 4 * LANES), axis=1, keepdims=True)
    pe2 = jnp.where(lane == i1, -1.0, pe)
    p2 = jnp.max(pe2, axis=1, keepdims=True)
    i2 = jnp.min(jnp.where(pe2 == p2, lane, 4 * LANES), axis=1, keepdims=True)
    tot = p1 + p2
    comb_ref[...] = (jnp.where(lane == i1, p_grp * (p1 / tot), 0.0)
                     + jnp.where(lane == i2, p_grp * (p2 / tot), 0.0)
                     + jnp.where(lane == GROUP_LANE, grp.astype(F32), 0.0))


def _mix(x, c, o, g0, b0, wg, bg, wco, wao, wo, g1, b1, wrh, wrl, br, alpha, tm):
    n = x.shape[0]
    row = lambda i: (i, 0)
    full = lambda i: (0, 0)
    fs = lambda a: pl.BlockSpec(a.shape, full)
    return pl.pallas_call(
        functools.partial(_mix_kernel, alpha=alpha),
        grid=(n // tm,),
        in_specs=[pl.BlockSpec((tm, D_MODEL), row), pl.BlockSpec((tm, CONV_CH), row),
                  pl.BlockSpec((tm, ATT_W), row)] + [fs(a) for a in (g0, b0, wg, bg, wco, wao, wo, g1, b1, wrh, wrl, br)],
        out_specs=[pl.BlockSpec((tm, D_MODEL), row), pl.BlockSpec((tm, LANES), row)],
        out_shape=[jax.ShapeDtypeStruct((n, D_MODEL), F32), jax.ShapeDtypeStruct((n, LANES), F32)],
        compiler_params=_cparams(("parallel",)),
        name="mix",
    )(x, c, o, g0, b0, wg, bg, wco, wao, wo, g1, b1, wrh, wrl, br)


def _split3(a):
    hi = a.astype(BF16)
    r1 = a - hi.astype(F32)
    mid = r1.astype(BF16)
    lo = (r1 - mid.astype(F32)).astype(BF16)
    return hi, mid, lo


def _permute_rows(perm, a):
    hi, mid, lo = _split3(a)
    return _dot(perm, hi) + _dot(perm, mid) + _dot(perm, lo)


def _ffn_kernel(x1_ref, comb_ref, p_ref, wpg_ref, wpp_ref, weg_ref, weu_ref, wed_ref, g2_ref, b2_ref,
                y_ref, to_slot_ref, to_token_ref, xs_ref, cs_ref, acc_ref, meta_ref,
                *, alpha, tm, rb, eps, slots):
    s = pl.program_id(1)

    @pl.when(s == 0)
    def _():
        comb = comb_ref[...]
        lane = lax.broadcasted_iota(jnp.int32, (tm, LANES), 1)
        lane_row = lax.broadcasted_iota(jnp.int32, (1, LANES), 1)
        grp = comb[:, GROUP_LANE:GROUP_LANE + 1]
        onehot = jnp.where((lane.astype(F32) == grp) & (lane < N_GROUPS), 1.0, 0.0)
        r = lax.broadcasted_iota(jnp.int32, (tm, tm), 0)
        c = lax.broadcasted_iota(jnp.int32, (tm, tm), 1)
        earlier = jnp.where(c < r, 1.0, 0.0).astype(BF16)
        before = _dot(earlier, onehot.astype(BF16))
        cnt = jnp.sum(onehot, axis=0, keepdims=True)
        base = jnp.zeros((1, LANES), F32)
        run = jnp.zeros((1, 1), F32)
        for g in range(N_GROUPS):
            base = base + jnp.where(lane_row == g, run, 0.0)
            meta_ref[g] = run[0, 0].astype(jnp.int32)
            meta_ref[N_GROUPS + g] = cnt[0, g].astype(jnp.int32)
            run = jnp.floor((run + cnt[:, g:g + 1] + (GROUP_ALIGN - 1.0)) * (1.0 / GROUP_ALIGN)) * GROUP_ALIGN
        slot = jnp.sum(onehot * (before + base), axis=1, keepdims=True)
        slot_row = jnp.transpose(jnp.broadcast_to(slot, (tm, LANES)))[0:1, :]
        slot_c = lax.broadcasted_iota(jnp.int32, (tm, slots), 1).astype(F32)
        slot_r = lax.broadcasted_iota(jnp.int32, (slots, tm), 0).astype(F32)
        to_slot_ref[...] = jnp.where(slot == slot_c, 1.0, 0.0).astype(BF16)
        to_token = jnp.where(slot_row == slot_r, 1.0, 0.0).astype(BF16)
        to_token_ref[...] = to_token
        xs_ref[0:slots, :] = _dot(to_token, x1_ref[...].astype(BF16)).astype(BF16)
        cs_ref[0:slots, :] = _permute_rows(to_token, comb)
        xs_ref[slots:, :] = jnp.zeros((rb, D_MODEL), BF16)
        cs_ref[slots:, :] = jnp.zeros((rb, LANES), F32)
        acc_ref[...] = jnp.zeros((slots + rb, D_MODEL), F32)

    g = (s * eps) // EXPERTS_PER_GROUP
    first = meta_ref[g]
    count = meta_ref[N_GROUPS + g]
    for blk in range(-(-tm // rb)):
        @pl.when(blk * rb < count)
        def _():
            rows = pl.ds(pl.multiple_of(first + blk * rb, GROUP_ALIGN), rb)
            xb = xs_ref[rows, :]
            cs = cs_ref[rows, :]
            lane = lax.broadcasted_iota(jnp.int32, (rb, LANES), 1)
            out = jnp.zeros((rb, D_MODEL), F32)
            for j in range(eps):
                ce = jnp.sum(jnp.where(lane == s * eps + j, cs, 0.0), axis=1, keepdims=True)
                hg = _dot(xb, weg_ref[j])
                hu = _dot(xb, weu_ref[j])
                hidden = (hg * _sigmoid(hg)) * hu * ce
                out = out + _dot(hidden.astype(BF16), wed_ref[j])
            acc_ref[rows, :] += out

    @pl.when(s == N_EXPERTS // eps - 1)
    def _():
        x1 = x1_ref[...]
        ffn = _permute_rows(to_slot_ref[...], acc_ref[0:slots, :])
        ple = _sigmoid(_dot(x1.astype(BF16), wpg_ref[...])) * _dot(p_ref[...].astype(BF16), wpp_ref[...])
        y_ref[...] = _layer_norm(alpha * x1 + ffn + ple, g2_ref[...], b2_ref[...])


def _ffn(x1, comb, p, wpg, wpp, weg, weu, wed, g2, b2, alpha, tm, rb, eps):
    n = x1.shape[0]
    assert rb % GROUP_ALIGN == 0 and EXPERTS_PER_GROUP % eps == 0
    slots = -(-(tm + N_GROUPS * GROUP_ALIGN) // LANES) * LANES
    row = lambda i, e: (i, 0)
    full = lambda i, e: (0, 0)
    exp = lambda i, e: (e, 0, 0)
    return pl.pallas_call(
        functools.partial(_ffn_kernel, alpha=alpha, tm=tm, rb=rb, eps=eps, slots=slots),
        grid=(n // tm, N_EXPERTS // eps),
        in_specs=[pl.BlockSpec((tm, D_MODEL), row), pl.BlockSpec((tm, LANES), row),
                  pl.BlockSpec((tm, PLE_DIM), row),
                  pl.BlockSpec((D_MODEL, D_MODEL), full), pl.BlockSpec((PLE_DIM, D_MODEL), full),
                  pl.BlockSpec((eps, D_MODEL, D_EXPERT), exp), pl.BlockSpec((eps, D_MODEL, D_EXPERT), exp),
                  pl.BlockSpec((eps, D_EXPERT, D_MODEL), exp),
                  pl.BlockSpec((1, D_MODEL), full), pl.BlockSpec((1, D_MODEL), full)],
        out_specs=pl.BlockSpec((tm, D_MODEL), row),
        out_shape=jax.ShapeDtypeStruct((n, D_MODEL), F32),
        scratch_shapes=[pltpu.VMEM((tm, slots), BF16), pltpu.VMEM((slots, tm), BF16),
                        pltpu.VMEM((slots + rb, D_MODEL), BF16), pltpu.VMEM((slots + rb, LANES), F32),
                        pltpu.VMEM((slots + rb, D_MODEL), F32), pltpu.SMEM((2 * N_GROUPS,), jnp.int32)],
        compiler_params=pltpu.CompilerParams(dimension_semantics=("parallel", "arbitrary"),
                                             vmem_limit_bytes=VMEM_LIMIT_FFN),
        name="ffn",
    )(x1, comb, p, wpg, wpp, weg, weu, wed, g2, b2)


def _tile(n, target):
    t = min(n, target)
    assert n % t == 0, (n, t)
    return t


def _row(v):
    return v.reshape(1, -1).astype(F32)


def kernel(x_prompt, x_sample, cache_k, cache_v, cache_kidx, state_conv, page_table, p_prompt, p_sample,
           ln0_g, ln0_b, w_in, b_gate, conv_w, conv_b, lnc_g, lnc_b, w_conv_out, w_attn_out, w_o,
           ln1_g, ln1_b, w_rg, b_rg, w_re, b_re, w_eg, w_eu, w_ed, w_pg, w_pp, ln2_g, ln2_b):
    depth = w_in.shape[0]
    assert depth == 1, "single-layer step"
    assert w_in.shape[1:] == (D_MODEL, COL_END)
    layer = 0
    alpha = (2.0 * depth) ** 0.25
    bp, tp, _ = x_prompt.shape
    bs, ts, _ = x_sample.shape
    assert ts == SUBLANES, "sample step length must fill one sublane tile"
    page = cache_k.shape[2]
    n_pages = page_table.shape[1]
    assert page == LANES

    g0, b0 = _row(ln0_g), _row(ln0_b)
    w = w_in[layer]
    wi_cols = jnp.pad(w[:, COL_WI:COL_GC], ((0, 0), (0, LANES - N_IDX_HEADS)))
    ws = jnp.concatenate([w[:, COL_GLU_A:COL_K], w[:, COL_QI:COL_KI], wi_cols], axis=1).astype(BF16)
    wt = w[:, COL_K:COL_QI].T
    wt = jnp.concatenate([wt, w[:, COL_KI:COL_WI].T], axis=0).astype(BF16)
    wg = w[:, COL_GC:].astype(BF16)
    bg = _row(b_gate[layer])
    wr = jnp.pad(jnp.concatenate([w_re[layer], w_rg[layer]], axis=1), ((0, 0), (0, LANES - N_EXPERTS - N_GROUPS)))
    wrh = wr.astype(BF16)
    wrl = (wr - wrh.astype(F32)).astype(BF16)
    br = jnp.pad(jnp.concatenate([b_re[layer], b_rg[layer]]), (0, LANES - N_EXPERTS - N_GROUPS)).reshape(1, LANES)
    wco, wao, wo = w_conv_out[layer].astype(BF16), w_attn_out[layer].astype(BF16), w_o[layer].astype(BF16)
    wpg, wpp = w_pg[layer].astype(BF16), w_pp[layer].astype(BF16)
    weg, weu, wed = w_eg[layer].astype(BF16), w_eu[layer].astype(BF16), w_ed[layer].astype(BF16)
    cw, cb = conv_w[layer], _row(conv_b[layer])
    lcg, lcb = _row(lnc_g[layer]), _row(lnc_b[layer])
    g1, b1, g2, b2 = _row(ln1_g[layer]), _row(ln1_b[layer]), _row(ln2_g[layer]), _row(ln2_b[layer])
    cache_kt = jnp.transpose(cache_k, (0, 1, 3, 4, 2))
    cache_vt = jnp.transpose(cache_v, (0, 1, 3, 4, 2))
    cache_kidx_t = jnp.transpose(cache_kidx, (0, 1, 3, 2))

    def tail(x1, comb, p):
        n = x1.shape[0]
        tm = _tile(n, 1024)
        rb = min(tm, tm // N_GROUPS + 4 * GROUP_ALIGN)
        return _ffn(x1, comb, p.reshape(n, PLE_DIM), wpg, wpp, weg, weu, wed, g2, b2, alpha, tm, rb, 4)

    def mix(x, c, o):
        n = x.shape[0]
        return _mix(x, c, o, g0, b0, wg, bg, wco, wao, wo, g1, b1, wrh, wrl, br, alpha, _tile(n, 512))

    np_ = bp * tp
    xp = x_prompt.reshape(np_, D_MODEL)
    glu, q, qi, wi, kt, vt, ktb, vtb, kit, kitb = _proj(xp, g0, b0, ws, wt, bp, tp, _tile(tp, 512))
    c, newconv_p = _conv(glu.reshape(bp, tp, CONV_CH), None, cw, cb, lcg, lcb, _tile(tp, 256))
    tq = _tile(tp, 256)
    o = jnp.concatenate(
        [_attn_prompt(qi, wi, q, kitb, ktb, vtb, bp, tp, tq, i, 1, (i + 1) * tq) for i in range(tp // tq)],
        axis=1).reshape(np_, ATT_W)
    x1, comb = mix(xp, c.reshape(np_, CONV_CH), o)
    y_prompt = tail(x1, comb, p_prompt[layer]).reshape(bp, tp, D_MODEL)
    new_k_p = jnp.transpose(kt, (0, 3, 1, 2))[None]
    new_v_p = jnp.transpose(vt, (0, 3, 1, 2))[None]
    new_ki_p = jnp.transpose(kit, (0, 2, 1))[None]

    ns = bs * ts
    xs = x_sample.reshape(ns, D_MODEL)
    glu, q, qi, wi, kt, vt, _, _, kit, kitb = _proj(xs, g0, b0, ws, wt, 1, ns, ns)
    c, newconv_s = _conv(glu.reshape(bs, ts, CONV_CH), state_conv[layer], cw, cb, lcg, lcb, ts)
    to_rows = lambda a: a.reshape(N_HEADS, bs, ts, HEAD_DIM).transpose(1, 0, 2, 3).reshape(bs, N_HEADS * ts, HEAD_DIM)
    qi_s, q_s = to_rows(qi), to_rows(q)
    wi_col = wi.reshape(bs, ts, N_IDX_HEADS).transpose(0, 2, 1).reshape(bs, N_IDX_HEADS * ts, 1)
    kt_s = kt.reshape(N_HEADS, HEAD_DIM, bs, ts)
    vt_s = vt.reshape(N_HEADS, HEAD_DIM, bs, ts)
    kit_s = kit.reshape(IDX_DIM, bs, ts)
    pad_keys = lambda a: jnp.pad(a, ((0, 0),) * (a.ndim - 1) + ((0, page - ts),))
    pt_flat = page_table.reshape(-1).astype(jnp.int32)
    pages_per = lambda target: max(d for d in range(1, target + 1) if n_pages % d == 0)
    scores_past, scores_new = _sample_scores(
        pt_flat, qi_s, wi_col, pad_keys(kitb.reshape(IDX_DIM, bs, ts).transpose(1, 0, 2)),
        cache_kidx_t, layer, pages_per(64))
    selp, seln = _sample_select(scores_past, scores_new, ts, _tile(ns, 64))
    o = _sample_attend(pt_flat, q_s, selp.reshape(bs, ts, -1), seln.reshape(bs, ts, LANES), cache_kt, cache_vt,
                       pad_keys(kt_s.transpose(2, 0, 1, 3)), pad_keys(vt_s.transpose(2, 0, 1, 3)),
                       layer, pages_per(32))
    x1, comb = mix(xs, c.reshape(ns, CONV_CH), o.reshape(ns, ATT_W).astype(BF16))
    y_sample = tail(x1, comb, p_sample[layer]).reshape(bs, ts, D_MODEL)
    new_k_s = kt_s.transpose(2, 3, 0, 1)[None]
    new_v_s = vt_s.transpose(2, 3, 0, 1)[None]
    new_ki_s = kit_s.transpose(1, 2, 0)[None]

    return (y_prompt, y_sample, new_k_p, new_v_p, new_ki_p, newconv_p[None],
            new_k_s, new_v_s, new_ki_s, newconv_s[None])
```

```python
import functools

import jax
import jax.numpy as jnp
from jax import lax
from jax.experimental import pallas as pl
from jax.experimental.pallas import tpu as pltpu

D_MODEL = 1024
CONV_CH = 512
CONV_WIDTH = 31
CONV_HIST = CONV_WIDTH - 1
N_HEADS = 8
HEAD_DIM = 64
ATT_W = N_HEADS * HEAD_DIM
N_IDX_HEADS = 8
IDX_DIM = 64
IDX_TOPK_MAX = 256
N_GROUPS = 4
EXPERTS_PER_GROUP = 8
N_EXPERTS = N_GROUPS * EXPERTS_PER_GROUP
D_EXPERT = 256
PLE_DIM = 256
LN_EPS = 1e-5
IDX_SCALE = (N_IDX_HEADS * IDX_DIM) ** -0.5
ATT_SCALE = HEAD_DIM ** -0.5
COL_GLU_A, COL_GLU_B, COL_Q, COL_K, COL_V, COL_QI, COL_KI, COL_WI, COL_GC = (
    0, 512, 1024, 1536, 2048, 2560, 3072, 3136, 3144)
COL_END = COL_GC + 2 * D_MODEL

LANES = 128
SUBLANES = 8
PREFIX_ROWS = 32
GROUP_LANE = LANES - 1
GROUP_ALIGN = 16
SEARCH_PARTS = 4
VMEM_LIMIT = 48 * 1024 * 1024
VMEM_LIMIT_FFN = 60 * 1024 * 1024

F32 = jnp.float32
BF16 = jnp.bfloat16
NEG_INF = float("-inf")
INT_MIN = -2 ** 31


def _cparams(sem):
    return pltpu.CompilerParams(dimension_semantics=sem, vmem_limit_bytes=VMEM_LIMIT)


def _layer_norm(x, g, b):
    mu = jnp.mean(x, axis=-1, keepdims=True)
    xc = x - mu
    var = jnp.mean(xc * xc, axis=-1, keepdims=True)
    return xc * lax.rsqrt(var + LN_EPS) * g + b


def _sigmoid(x):
    return 1.0 / (1.0 + jnp.exp(-x))


def _dot(a, b):
    return jnp.dot(a, b, preferred_element_type=F32)


def _dot_nt(a, b):
    return lax.dot_general(a, b, (((1,), (1,)), ((), ())), preferred_element_type=F32)


def _proj_kernel(x_ref, g0_ref, b0_ref, ws_ref, wt_ref,
                 glu_ref, q_ref, qi_ref, wi_ref, kt_ref, vt_ref, ktb_ref, vtb_ref, kit_ref, kitb_ref):
    h = _layer_norm(x_ref[...], g0_ref[...], b0_ref[...])
    hb = h.astype(BF16)

    def seg(col, width=512):
        return _dot(hb, ws_ref[:, col:col + width])

    glu_ref[...] = seg(0) * _sigmoid(seg(512))
    q = seg(1024) * ATT_SCALE
    qi = seg(1536)
    for hd in range(N_HEADS):
        sl = slice(hd * HEAD_DIM, (hd + 1) * HEAD_DIM)
        q_ref[hd] = q[:, sl].astype(BF16)
        qi_ref[hd] = qi[:, sl].astype(BF16)
    wi_ref[...] = seg(2048, LANES)[:, :N_IDX_HEADS]

    tm = hb.shape[0]
    kt = _dot_nt(wt_ref[0:ATT_W, :], hb).reshape(N_HEADS, HEAD_DIM, tm)
    kt_ref[...] = kt
    ktb_ref[...] = kt.astype(BF16)
    vt = _dot_nt(wt_ref[ATT_W:2 * ATT_W, :], hb).reshape(N_HEADS, HEAD_DIM, tm)
    vt_ref[...] = vt
    vtb_ref[...] = vt.astype(BF16)
    kit = _dot_nt(wt_ref[2 * ATT_W:2 * ATT_W + IDX_DIM, :], hb)
    kit_ref[...] = kit
    kitb_ref[...] = kit.astype(BF16)


def _proj(x, g0, b0, ws, wt, b, t, tm):
    n = b * t
    nt = t // tm
    row = lambda i: (i, 0)
    head = lambda i: (0, i, 0)
    full = lambda i: (0, 0)
    hm = jax.ShapeDtypeStruct((N_HEADS, n, HEAD_DIM), BF16)
    hm_spec = pl.BlockSpec((N_HEADS, tm, HEAD_DIM), head)
    kv_spec = pl.BlockSpec((None, N_HEADS, HEAD_DIM, tm), lambda i: (i // nt, 0, 0, i % nt))
    ki_spec = pl.BlockSpec((None, IDX_DIM, tm), lambda i: (i // nt, 0, i % nt))
    kv_shape = lambda dt: jax.ShapeDtypeStruct((b, N_HEADS, HEAD_DIM, t), dt)
    ki_shape = lambda dt: jax.ShapeDtypeStruct((b, IDX_DIM, t), dt)
    return pl.pallas_call(
        _proj_kernel,
        grid=(n // tm,),
        in_specs=[pl.BlockSpec((tm, D_MODEL), row),
                  pl.BlockSpec((1, D_MODEL), full), pl.BlockSpec((1, D_MODEL), full),
                  pl.BlockSpec(ws.shape, full), pl.BlockSpec(wt.shape, full)],
        out_specs=[pl.BlockSpec((tm, CONV_CH), row), hm_spec, hm_spec,
                   pl.BlockSpec((tm, N_IDX_HEADS), row),
                   kv_spec, kv_spec, kv_spec, kv_spec, ki_spec, ki_spec],
        out_shape=[jax.ShapeDtypeStruct((n, CONV_CH), F32), hm, hm,
                   jax.ShapeDtypeStruct((n, N_IDX_HEADS), F32),
                   kv_shape(F32), kv_shape(F32), kv_shape(BF16), kv_shape(BF16), ki_shape(F32), ki_shape(BF16)],
        compiler_params=_cparams(("parallel",)),
        name="proj",
    )(x, g0, b0, ws, wt)


def _conv_kernel(*refs, tc, nt, has_state):
    it = iter(refs)
    glu_ref = next(it)
    prev_ref = next(it) if nt > 1 else None
    state_ref = next(it) if has_state else None
    w_ref, cb_ref, g_ref, b_ref, c_ref, newconv_ref, buf_ref, shift_ref, acc_ref = it
    t = pl.program_id(1)

    @pl.when(t == 0)
    def _():
        buf_ref[0:PREFIX_ROWS, :] = jnp.zeros((PREFIX_ROWS, CONV_CH), F32)
        if has_state:
            buf_ref[2:PREFIX_ROWS, :] = state_ref[...]

    if nt > 1:
        @pl.when(t > 0)
        def _():
            buf_ref[0:PREFIX_ROWS, :] = prev_ref[...]

    buf_ref[PREFIX_ROWS:PREFIX_ROWS + tc, :] = glu_ref[...]

    shifted_rows = tc + PREFIX_ROWS - SUBLANES
    for s in range(1, SUBLANES):
        shift_ref[s - 1] = buf_ref[s:s + shifted_rows, :]

    rb = min(tc, 32)
    for c0 in range(0, CONV_CH, LANES):
        cs = slice(c0, c0 + LANES)
        bias = cb_ref[:, cs]
        for r0 in range(0, tc, rb):
            acc = jnp.zeros((rb, LANES), F32) + bias
            for j in range(CONV_WIDTH):
                a, s = divmod(2 + j, SUBLANES)
                lo = r0 + a * SUBLANES
                rows = buf_ref[lo:lo + rb, cs] if s == 0 else shift_ref[s - 1, lo:lo + rb, cs]
                acc = acc + w_ref[j:j + 1, cs] * rows
            acc_ref[r0:r0 + rb, cs] = acc

    y = _layer_norm(acc_ref[...], g_ref[...], b_ref[...])
    c_ref[...] = (y * _sigmoid(y)).astype(BF16)

    @pl.when(t == nt - 1)
    def _():
        newconv_ref[...] = buf_ref[tc + 2:tc + PREFIX_ROWS, :]


def _conv(glu, state, conv_w, conv_b, lnc_g, lnc_b, tc):
    b, t, _ = glu.shape
    nt = t // tc
    blocks_per_tile = tc // PREFIX_ROWS
    in_specs = [pl.BlockSpec((None, tc, CONV_CH), lambda i, j: (i, j, 0))]
    args = [glu]
    if nt > 1:
        in_specs.append(pl.BlockSpec((None, PREFIX_ROWS, CONV_CH),
                                     lambda i, j: (i, jnp.maximum(j * blocks_per_tile - 1, 0), 0)))
        args.append(glu)
    if state is not None:
        in_specs.append(pl.BlockSpec((None, CONV_HIST, CONV_CH), lambda i, j: (i, 0, 0)))
        args.append(state)
    full = lambda i, j: (0, 0)
    in_specs += [pl.BlockSpec((CONV_WIDTH, CONV_CH), full), pl.BlockSpec((1, CONV_CH), full),
                 pl.BlockSpec((1, CONV_CH), full), pl.BlockSpec((1, CONV_CH), full)]
    args += [conv_w, conv_b, lnc_g, lnc_b]
    return pl.pallas_call(
        functools.partial(_conv_kernel, tc=tc, nt=nt, has_state=state is not None),
        grid=(b, nt),
        in_specs=in_specs,
        out_specs=[pl.BlockSpec((None, tc, CONV_CH), lambda i, j: (i, j, 0)),
                   pl.BlockSpec((None, CONV_HIST, CONV_CH), lambda i, j: (i, 0, 0))],
        out_shape=[jax.ShapeDtypeStruct((b, t, CONV_CH), BF16),
                   jax.ShapeDtypeStruct((b, CONV_HIST, CONV_CH), F32)],
        scratch_shapes=[pltpu.VMEM((PREFIX_ROWS + tc, CONV_CH), F32),
                        pltpu.VMEM((SUBLANES - 1, tc + PREFIX_ROWS - SUBLANES, CONV_CH), F32),
                        pltpu.VMEM((tc, CONV_CH), F32)],
        compiler_params=_cparams(("parallel", "arbitrary")),
        name="conv",
    )(*args)


def _ordered_to_f32(c):
    return pltpu.bitcast(c ^ ((c >> 31) & jnp.int32(0x7FFFFFFF)), F32)


def _kth_largest(count_ge_parts, rows, topk):
    def body(i, taus):
        bit = lax.shift_left(jnp.int32(1), jnp.int32(31) - i)
        out = []
        for count_ge, tau in zip(count_ge_parts, taus):
            cand = tau ^ bit
            out.append(jnp.where(count_ge(_ordered_to_f32(cand)) >= topk, cand, tau))
        return tuple(out)
    init = tuple(jnp.full((rows, 1), INT_MIN, jnp.int32) for _ in count_ge_parts)
    return [_ordered_to_f32(t) for t in lax.fori_loop(0, 32, body, init, unroll=16)]


def _count(mask):
    return jnp.sum(mask.astype(jnp.int32), axis=1, keepdims=True)


def _tie_rank(eq_chunks, rows):
    r = lax.broadcasted_iota(jnp.int32, (LANES, LANES), 0)
    c = lax.broadcasted_iota(jnp.int32, (LANES, LANES), 1)
    upper = jnp.where(r < c, 1.0, 0.0).astype(BF16)
    prefix = jnp.zeros((rows, 1), F32)
    for idx, eq in eq_chunks:
        eqf = jnp.where(eq, 1.0, 0.0)
        yield idx, _dot(eqf.astype(BF16), upper) + prefix
        prefix = prefix + jnp.sum(eqf, axis=1, keepdims=True)


def _attn_prompt_kernel(qi_ref, wi_ref, q_ref, kitb_ref, ktb_ref, vtb_ref, o_ref, sc_ref, bias_ref,
                        *, tq, t_eff, topk, first_block):
    qb = first_block + pl.program_id(1)
    wi = wi_ref[...]
    kitb = kitb_ref[...]
    acc = jnp.zeros((tq, t_eff), F32)
    for hd in range(N_IDX_HEADS):
        acc = acc + jnp.maximum(_dot(qi_ref[hd], kitb), 0.0) * wi[:, hd:hd + 1]
    q_pos = qb * tq + lax.broadcasted_iota(jnp.int32, (tq, 1), 0)
    k_pos = lax.broadcasted_iota(jnp.int32, (1, t_eff), 1)
    adm = k_pos <= q_pos
    sc_ref[...] = jnp.where(adm, acc * IDX_SCALE, NEG_INF)

    take_all = (q_pos + 1) <= topk
    if t_eff <= topk:
        tau = sc_ref[:, 0:1]
    else:
        part = tq // SEARCH_PARTS
        tau = jnp.concatenate(
            _kth_largest([functools.partial(lambda lo, cand: _count(sc_ref[lo:lo + part, :] >= cand), i * part)
                          for i in range(SEARCH_PARTS)], part, topk), axis=0)
    sc = sc_ref[...]
    eq = (sc == tau) & adm
    need = topk - _count(sc > tau)
    excess = jnp.max(jnp.where(take_all, 0, _count(eq) - need)) > 0

    @pl.when(jnp.logical_not(excess))
    def _():
        bias_ref[...] = jnp.where(adm & ((sc >= tau) | take_all), 0.0, NEG_INF)

    @pl.when(excess)
    def _():
        needf = need.astype(F32)
        chunks = [(i, (sc_ref[:, i * LANES:(i + 1) * LANES] == tau) & adm[:, i * LANES:(i + 1) * LANES])
                  for i in range(t_eff // LANES)]
        for i, rank in _tie_rank(chunks, tq):
            cs = slice(i * LANES, (i + 1) * LANES)
            kc = sc_ref[:, cs]
            sel = ((kc > tau) | ((kc == tau) & (rank < needf)) | take_all) & adm[:, cs]
            bias_ref[:, cs] = jnp.where(sel, 0.0, NEG_INF)

    bias = bias_ref[...]
    for hd in range(N_HEADS):
        logit = _dot(q_ref[hd], ktb_ref[hd]) + bias
        m = jnp.max(logit, axis=1, keepdims=True)
        p = jnp.exp(logit - m)
        denom = jnp.sum(p, axis=1, keepdims=True)
        o = _dot_nt(p.astype(BF16), vtb_ref[hd]) / denom
        o_ref[:, hd * HEAD_DIM:(hd + 1) * HEAD_DIM] = o.astype(BF16)


def _attn_prompt(qi, wi, q, kitb, ktb, vtb, b, t, tq, first_block, n_blocks, t_eff):
    nq = t // tq
    topk = min(IDX_TOPK_MAX, t // 4)
    assert (first_block + n_blocks) * tq <= t_eff
    qrow = lambda i, j: i * nq + first_block + j
    qspec = pl.BlockSpec((N_HEADS, tq, HEAD_DIM), lambda i, j: (0, qrow(i, j), 0))
    kspec = pl.BlockSpec((None, N_HEADS, HEAD_DIM, t_eff), lambda i, j: (i, 0, 0, 0))
    return pl.pallas_call(
        functools.partial(_attn_prompt_kernel, tq=tq, t_eff=t_eff, topk=topk, first_block=first_block),
        grid=(b, n_blocks),
        in_specs=[qspec, pl.BlockSpec((tq, N_IDX_HEADS), lambda i, j: (qrow(i, j), 0)), qspec,
                  pl.BlockSpec((None, IDX_DIM, t_eff), lambda i, j: (i, 0, 0)), kspec, kspec],
        out_specs=pl.BlockSpec((None, tq, ATT_W), lambda i, j: (i, j, 0)),
        out_shape=jax.ShapeDtypeStruct((b, n_blocks * tq, ATT_W), BF16),
        scratch_shapes=[pltpu.VMEM((tq, t_eff), F32), pltpu.VMEM((tq, t_eff), F32)],
        compiler_params=_cparams(("parallel", "arbitrary")),
        name="attn_prompt",
    )(qi, wi, q, kitb, ktb, vtb)


def _indexer_rows(qi, keys_t, wi_col):
    s = jnp.maximum(_dot(qi, keys_t), 0.0) * wi_col
    n = s.shape[1]
    return jnp.sum(s.reshape(N_IDX_HEADS, SUBLANES, n), axis=0) * IDX_SCALE


def _sample_scores_kernel(pt_ref, qi_ref, wi_ref, kin_ref, *refs, pages_per_step, page):
    del pt_ref
    page_refs, out_ref, new_ref = refs[:pages_per_step], refs[pages_per_step], refs[pages_per_step + 1]
    qi = qi_ref[...]
    wi_col = wi_ref[...]
    chunk = max(d for d in range(1, SUBLANES + 1) if pages_per_step % d == 0)
    for i in range(0, pages_per_step, chunk):
        keys = jnp.concatenate([page_refs[i + j][...].astype(BF16) for j in range(chunk)], axis=1)
        out_ref[:, i * page:(i + chunk) * page] = _indexer_rows(qi, keys, wi_col)

    @pl.when(pl.program_id(1) == 0)
    def _():
        new_ref[...] = _indexer_rows(qi, kin_ref[...], wi_col)


def _sample_scores(page_table, qi_s, wi_col, kitb_new_pad, cache_kidx_t, layer, pages_per_step):
    bsz = qi_s.shape[0]
    n_pages = page_table.shape[0] // bsz
    page = cache_kidx_t.shape[-1]
    steps = n_pages // pages_per_step

    def page_spec(i):
        return pl.BlockSpec((None, None, IDX_DIM, page),
                            lambda b, g, pt: (layer, pt[b * n_pages + g * pages_per_step + i], 0, 0))

    grid_spec = pltpu.PrefetchScalarGridSpec(
        num_scalar_prefetch=1,
        grid=(bsz, steps),
        in_specs=[pl.BlockSpec((None, N_IDX_HEADS * SUBLANES, IDX_DIM), lambda b, g, pt: (b, 0, 0)),
                  pl.BlockSpec((None, N_IDX_HEADS * SUBLANES, 1), lambda b, g, pt: (b, 0, 0)),
                  pl.BlockSpec((None, IDX_DIM, LANES), lambda b, g, pt: (b, 0, 0))]
                 + [page_spec(i) for i in range(pages_per_step)],
        out_specs=[pl.BlockSpec((SUBLANES, pages_per_step * page), lambda b, g, pt: (b, g)),
                   pl.BlockSpec((SUBLANES, LANES), lambda b, g, pt: (b, 0))],
    )
    return pl.pallas_call(
        functools.partial(_sample_scores_kernel, pages_per_step=pages_per_step, page=page),
        grid_spec=grid_spec,
        out_shape=[jax.ShapeDtypeStruct((bsz * SUBLANES, n_pages * page), F32),
                   jax.ShapeDtypeStruct((bsz * SUBLANES, LANES), F32)],
        compiler_params=_cparams(("parallel", "arbitrary")),
        name="sample_scores",
    )(page_table, qi_s, wi_col, kitb_new_pad, *([cache_kidx_t] * pages_per_step))


def _sample_select_kernel(sp_ref, sn_ref, selp_ref, seln_ref, *, rows, t_new, past, topk):
    jn = lax.broadcasted_iota(jnp.int32, (rows, LANES), 1)
    tn = lax.broadcasted_iota(jnp.int32, (rows, LANES), 0) & (SUBLANES - 1)
    adm_n = (jn <= tn) & (jn < t_new)
    sn = jnp.where(adm_n, sn_ref[...], NEG_INF)

    half = rows // 2
    tau = jnp.concatenate(
        _kth_largest([lambda cand: _count(sp_ref[0:half, :] >= cand) + _count(sn[0:half] >= cand),
                      lambda cand: _count(sp_ref[half:rows, :] >= cand) + _count(sn[half:rows] >= cand)],
                     half, topk), axis=0)
    sp = sp_ref[...]
    need = topk - (_count(sp > tau) + _count(sn > tau))
    eqn = (sn == tau) & adm_n
    excess = jnp.max(_count(sp == tau) + _count(eqn) - need) > 0

    @pl.when(jnp.logical_not(excess))
    def _():
        selp_ref[...] = jnp.where(sp >= tau, 1.0, 0.0)
        seln_ref[...] = jnp.where((sn >= tau) & adm_n, 1.0, 0.0)

    @pl.when(excess)
    def _():
        needf = need.astype(F32)
        n_chunks = past // LANES
        chunks = [(i, sp_ref[:, i * LANES:(i + 1) * LANES] == tau) for i in range(n_chunks)]
        chunks.append((n_chunks, eqn))
        for i, rank in _tie_rank(chunks, rows):
            if i < n_chunks:
                kc = sp_ref[:, i * LANES:(i + 1) * LANES]
                sel = (kc > tau) | ((kc == tau) & (rank < needf))
                selp_ref[:, i * LANES:(i + 1) * LANES] = jnp.where(sel, 1.0, 0.0)
            else:
                sel = ((sn > tau) | ((sn == tau) & (rank < needf))) & adm_n
                seln_ref[...] = jnp.where(sel, 1.0, 0.0)


def _sample_select(scores_past, scores_new, t_new, rows):
    n, past = scores_past.shape
    assert past >= IDX_TOPK_MAX, "every query must see at least top-k admissible keys"
    assert t_new == SUBLANES and n % rows == 0
    topk = min(IDX_TOPK_MAX, (past + t_new) // 4)
    blk = lambda w: pl.BlockSpec((rows, w), lambda i: (i, 0))
    return pl.pallas_call(
        functools.partial(_sample_select_kernel, rows=rows, t_new=t_new, past=past, topk=topk),
        grid=(n // rows,),
        in_specs=[blk(past), blk(LANES)],
        out_specs=[blk(past), blk(LANES)],
        out_shape=[jax.ShapeDtypeStruct((n, past), F32), jax.ShapeDtypeStruct((n, LANES), F32)],
        compiler_params=_cparams(("parallel",)),
        name="sample_select",
    )(scores_past, scores_new)


def _sample_attend_kernel(pt_ref, q_ref, selp_ref, seln_ref, *refs, n_groups, pages_per_step, page):
    del pt_ref
    g = pages_per_step
    k_refs, v_refs = refs[:g], refs[g:2 * g]
    kn_ref, vn_ref, o_ref, m_ref, l_ref, acc_ref = refs[2 * g:]
    step = pl.program_id(1)
    rows = N_HEADS * SUBLANES

    @pl.when(step == 0)
    def _():
        m_ref[...] = jnp.full((rows, 1), NEG_INF, F32)
        l_ref[...] = jnp.zeros((rows, 1), F32)
        acc_ref[...] = jnp.zeros((rows, ATT_W), F32)

    qrow = lax.broadcasted_iota(jnp.int32, (rows, ATT_W), 0)
    qcol = lax.broadcasted_iota(jnp.int32, (rows, ATT_W), 1)
    q_bd = jnp.where((qcol >> 6) == (qrow >> 3), jnp.concatenate([q_ref[...]] * N_HEADS, axis=1), 0.0).astype(BF16)

    def process(k_pages, v_pages, sel):
        logit = jnp.concatenate([_dot(q_bd, kp.reshape(ATT_W, page).astype(BF16)) for kp in k_pages], axis=1)
        valid = jnp.concatenate([sel] * N_HEADS, axis=0) > 0.5
        logit = jnp.where(valid, logit, NEG_INF)
        m_old = m_ref[...]
        m_new = jnp.maximum(m_old, jnp.max(logit, axis=1, keepdims=True))
        m_safe = jnp.where(m_new == NEG_INF, 0.0, m_new)
        pr = jnp.exp(logit - m_safe)
        alpha = jnp.exp(m_old - m_safe)
        l_ref[...] = alpha * l_ref[...] + jnp.sum(pr, axis=1, keepdims=True)
        prb = pr.astype(BF16)
        pv = jnp.zeros((rows, ATT_W), F32)
        for i, vp in enumerate(v_pages):
            pv = pv + _dot_nt(prb[:, i * page:(i + 1) * page], vp.reshape(ATT_W, page).astype(BF16))
        acc_ref[...] = alpha * acc_ref[...] + pv
        m_ref[...] = m_new

    process([r[...] for r in k_refs], [r[...] for r in v_refs], selp_ref[...])

    @pl.when(step == n_groups - 1)
    def _():
        process([kn_ref[...]], [vn_ref[...]], seln_ref[...])
        out = acc_ref[...] / l_ref[...]
        for hd in range(N_HEADS):
            o_ref[:, hd * HEAD_DIM:(hd + 1) * HEAD_DIM] = out[hd * SUBLANES:(hd + 1) * SUBLANES,
                                                              hd * HEAD_DIM:(hd + 1) * HEAD_DIM]


def _sample_attend(page_table, q_s, selp, seln, cache_kt, cache_vt, kt_new_pad, vt_new_pad, layer, pages_per_step):
    bsz = q_s.shape[0]
    n_pages = page_table.shape[0] // bsz
    page = cache_kt.shape[-1]
    g = pages_per_step
    n_groups = n_pages // g

    def page_spec(i):
        return pl.BlockSpec(
            (None, None, N_HEADS, HEAD_DIM, page),
            lambda b, s, pt: (layer, pt[b * n_pages + s * g + i], 0, 0, 0))

    new_spec = pl.BlockSpec((None, N_HEADS, HEAD_DIM, page), lambda b, s, pt: (b, 0, 0, 0))
    rows = N_HEADS * SUBLANES
    grid_spec = pltpu.PrefetchScalarGridSpec(
        num_scalar_prefetch=1,
        grid=(bsz, n_groups),
        in_specs=[pl.BlockSpec((None, rows, HEAD_DIM), lambda b, s, pt: (b, 0, 0)),
                  pl.BlockSpec((None, SUBLANES, g * page), lambda b, s, pt: (b, 0, s)),
                  pl.BlockSpec((None, SUBLANES, LANES), lambda b, s, pt: (b, 0, 0))]
                 + [page_spec(i) for i in range(g)] + [page_spec(i) for i in range(g)]
                 + [new_spec, new_spec],
        out_specs=pl.BlockSpec((None, SUBLANES, ATT_W), lambda b, s, pt: (b, 0, 0)),
        scratch_shapes=[pltpu.VMEM((rows, 1), F32), pltpu.VMEM((rows, 1), F32), pltpu.VMEM((rows, ATT_W), F32)],
    )
    return pl.pallas_call(
        functools.partial(_sample_attend_kernel, n_groups=n_groups, pages_per_step=g, page=page),
        grid_spec=grid_spec,
        out_shape=jax.ShapeDtypeStruct((bsz, SUBLANES, ATT_W), F32),
        compiler_params=_cparams(("parallel", "arbitrary")),
        name="sample_attend",
    )(page_table, q_s, selp, seln, *([cache_kt] * g), *([cache_vt] * g), kt_new_pad, vt_new_pad)


def _mix_kernel(x_ref, c_ref, o_ref, g0_ref, b0_ref, wg_ref, bg_ref, wco_ref, wao_ref, wo_ref,
                g1_ref, b1_ref, wrh_ref, wrl_ref, br_ref, x1_ref, comb_ref, *, alpha):
    h = _layer_norm(x_ref[...], g0_ref[...], b0_ref[...])
    hb = h.astype(BF16)
    gc = _sigmoid(_dot(hb, wg_ref[:, :D_MODEL]) + bg_ref[:, :D_MODEL])
    ga = _sigmoid(_dot(hb, wg_ref[:, D_MODEL:]) + bg_ref[:, D_MODEL:])
    merged = gc * _dot(c_ref[...], wco_ref[...]) + ga * _dot(o_ref[...], wao_ref[...])
    mix = _dot(merged.astype(BF16), wo_ref[...])
    x1 = _layer_norm(alpha * h + mix, g1_ref[...], b1_ref[...])
    x1_ref[...] = x1

    xh = x1.astype(BF16)
    xl = (x1 - xh.astype(F32)).astype(BF16)
    logits = _dot(xh, wrh_ref[...]) + (_dot(xl, wrh_ref[...]) + _dot(xh, wrl_ref[...])) + br_ref[...]
    lane = lax.broadcasted_iota(jnp.int32, logits.shape, 1)
    is_grp = (lane >= N_EXPERTS) & (lane < N_EXPERTS + N_GROUPS)
    gl = jnp.where(is_grp, logits, NEG_INF)
    gmax = jnp.max(gl, axis=1, keepdims=True)
    grp = jnp.min(jnp.where(gl == gmax, lane, 4 * LANES), axis=1, keepdims=True) - N_EXPERTS
    p_grp = 1.0 / jnp.sum(jnp.exp(gl - gmax), axis=1, keepdims=True)
    in_grp = (lane < N_EXPERTS) & ((lane >> 3) == grp)
    el = jnp.where(in_grp, logits, NEG_INF)
    emax = jnp.max(el, axis=1, keepdims=True)
    ee = jnp.exp(el - emax)
    pe = ee / jnp.sum(ee, axis=1, keepdims=True)
    pe = jnp.where(in_grp, pe, -1.0)
    p1 = jnp.max(pe, axis=1, keepdims=True)
    i1 = jnp.min(jnp.where(pe == p1, lane, 4 * LANES), axis=1, keepdims=True)
    pe2 = jnp.where(lane == i1, -1.0, pe)
    p2 = jnp.max(pe2, axis=1, keepdims=True)
    i2 = jnp.min(jnp.where(pe2 == p2, lane, 4 * LANES), axis=1, keepdims=True)
    tot = p1 + p2
    comb_ref[...] = (jnp.where(lane == i1, p_grp * (p1 / tot), 0.0)
                     + jnp.where(lane == i2, p_grp * (p2 / tot), 0.0)
                     + jnp.where(lane == GROUP_LANE, grp.astype(F32), 0.0))


def _mix(x, c, o, g0, b0, wg, bg, wco, wao, wo, g1, b1, wrh, wrl, br, alpha, tm):
    n = x.shape[0]
    row = lambda i: (i, 0)
    full = lambda i: (0, 0)
    fs = lambda a: pl.BlockSpec(a.shape, full)
    return pl.pallas_call(
        functools.partial(_mix_kernel, alpha=alpha),
        grid=(n // tm,),
        in_specs=[pl.BlockSpec((tm, D_MODEL), row), pl.BlockSpec((tm, CONV_CH), row),
                  pl.BlockSpec((tm, ATT_W), row)] + [fs(a) for a in (g0, b0, wg, bg, wco, wao, wo, g1, b1, wrh, wrl, br)],
        out_specs=[pl.BlockSpec((tm, D_MODEL), row), pl.BlockSpec((tm, LANES), row)],
        out_shape=[jax.ShapeDtypeStruct((n, D_MODEL), F32), jax.ShapeDtypeStruct((n, LANES), F32)],
        compiler_params=_cparams(("parallel",)),
        name="mix",
    )(x, c, o, g0, b0, wg, bg, wco, wao, wo, g1, b1, wrh, wrl, br)


def _split3(a):
    hi = a.astype(BF16)
    r1 = a - hi.astype(F32)
    mid = r1.astype(BF16)
    lo = (r1 - mid.astype(F32)).astype(BF16)
    return hi, mid, lo


def _permute_rows(perm, a):
    hi, mid, lo = _split3(a)
    return _dot(perm, hi) + _dot(perm, mid) + _dot(perm, lo)


def _ffn_kernel(x1_ref, comb_ref, p_ref, wpg_ref, wpp_ref, weg_ref, weu_ref, wed_ref, g2_ref, b2_ref,
                y_ref, to_slot_ref, to_token_ref, xs_ref, cs_ref, acc_ref, meta_ref,
                *, alpha, tm, rb, eps, slots):
    s = pl.program_id(1)

    @pl.when(s == 0)
    def _():
        comb = comb_ref[...]
        lane = lax.broadcasted_iota(jnp.int32, (tm, LANES), 1)
        lane_row = lax.broadcasted_iota(jnp.int32, (1, LANES), 1)
        grp = comb[:, GROUP_LANE:GROUP_LANE + 1]
        onehot = jnp.where((lane.astype(F32) == grp) & (lane < N_GROUPS), 1.0, 0.0)
        r = lax.broadcasted_iota(jnp.int32, (tm, tm), 0)
        c = lax.broadcasted_iota(jnp.int32, (tm, tm), 1)
        earlier = jnp.where(c < r, 1.0, 0.0).astype(BF16)
        before = _dot(earlier, onehot.astype(BF16))
        cnt = jnp.sum(onehot, axis=0, keepdims=True)
        base = jnp.zeros((1, LANES), F32)
        run = jnp.zeros((1, 1), F32)
        for g in range(N_GROUPS):
            base = base + jnp.where(lane_row == g, run, 0.0)
            meta_ref[g] = run[0, 0].astype(jnp.int32)
            meta_ref[N_GROUPS + g] = cnt[0, g].astype(jnp.int32)
            run = jnp.floor((run + cnt[:, g:g + 1] + (GROUP_ALIGN - 1.0)) * (1.0 / GROUP_ALIGN)) * GROUP_ALIGN
        slot = jnp.sum(onehot * (before + base), axis=1, keepdims=True)
        slot_row = jnp.transpose(jnp.broadcast_to(slot, (tm, LANES)))[0:1, :]
        slot_c = lax.broadcasted_iota(jnp.int32, (tm, slots), 1).astype(F32)
        slot_r = lax.broadcasted_iota(jnp.int32, (slots, tm), 0).astype(F32)
        to_slot_ref[...] = jnp.where(slot == slot_c, 1.0, 0.0).astype(BF16)
        to_token = jnp.where(slot_row == slot_r, 1.0, 0.0).astype(BF16)
        to_token_ref[...] = to_token
        xs_ref[0:slots, :] = _dot(to_token, x1_ref[...].astype(BF16)).astype(BF16)
        cs_ref[0:slots, :] = _permute_rows(to_token, comb)
        xs_ref[slots:, :] = jnp.zeros((rb, D_MODEL), BF16)
        cs_ref[slots:, :] = jnp.zeros((rb, LANES), F32)
        acc_ref[...] = jnp.zeros((slots + rb, D_MODEL), F32)

    g = (s * eps) // EXPERTS_PER_GROUP
    first = meta_ref[g]
    count = meta_ref[N_GROUPS + g]
    for blk in range(-(-tm // rb)):
        @pl.when(blk * rb < count)
        def _():
            rows = pl.ds(pl.multiple_of(first + blk * rb, GROUP_ALIGN), rb)
            xb = xs_ref[rows, :]
            cs = cs_ref[rows, :]
            lane = lax.broadcasted_iota(jnp.int32, (rb, LANES), 1)
            out = jnp.zeros((rb, D_MODEL), F32)
            for j in range(eps):
                ce = jnp.sum(jnp.where(lane == s * eps + j, cs, 0.0), axis=1, keepdims=True)
                hg = _dot(xb, weg_ref[j])
                hu = _dot(xb, weu_ref[j])
                hidden = (hg * _sigmoid(hg)) * hu * ce
                out = out + _dot(hidden.astype(BF16), wed_ref[j])
            acc_ref[rows, :] += out

    @pl.when(s == N_EXPERTS // eps - 1)
    def _():
        x1 = x1_ref[...]
        ffn = _permute_rows(to_slot_ref[...], acc_ref[0:slots, :])
        ple = _sigmoid(_dot(x1.astype(BF16), wpg_ref[...])) * _dot(p_ref[...].astype(BF16), wpp_ref[...])
        y_ref[...] = _layer_norm(alpha * x1 + ffn + ple, g2_ref[...], b2_ref[...])


def _ffn(x1, comb, p, wpg, wpp, weg, weu, wed, g2, b2, alpha, tm, rb, eps):
    n = x1.shape[0]
    assert rb % GROUP_ALIGN == 0 and EXPERTS_PER_GROUP % eps == 0
    slots = -(-(tm + N_GROUPS * GROUP_ALIGN) // LANES) * LANES
    row = lambda i, e: (i, 0)
    full = lambda i, e: (0, 0)
    exp = lambda i, e: (e, 0, 0)
    return pl.pallas_call(
        functools.partial(_ffn_kernel, alpha=alpha, tm=tm, rb=rb, eps=eps, slots=slots),
        grid=(n // tm, N_EXPERTS // eps),
        in_specs=[pl.BlockSpec((tm, D_MODEL), row), pl.BlockSpec((tm, LANES), row),
                  pl.BlockSpec((tm, PLE_DIM), row),
                  pl.BlockSpec((D_MODEL, D_MODEL), full), pl.BlockSpec((PLE_DIM, D_MODEL), full),
                  pl.BlockSpec((eps, D_MODEL, D_EXPERT), exp), pl.BlockSpec((eps, D_MODEL, D_EXPERT), exp),
                  pl.BlockSpec((eps, D_EXPERT, D_MODEL), exp),
                  pl.BlockSpec((1, D_MODEL), full), pl.BlockSpec((1, D_MODEL), full)],
        out_specs=pl.BlockSpec((tm, D_MODEL), row),
        out_shape=jax.ShapeDtypeStruct((n, D_MODEL), F32),
        scratch_shapes=[pltpu.VMEM((tm, slots), BF16), pltpu.VMEM((slots, tm), BF16),
                        pltpu.VMEM((slots + rb, D_MODEL), BF16), pltpu.VMEM((slots + rb, LANES), F32),
                        pltpu.VMEM((slots + rb, D_MODEL), F32), pltpu.SMEM((2 * N_GROUPS,), jnp.int32)],
        compiler_params=pltpu.CompilerParams(dimension_semantics=("parallel", "arbitrary"),
                                             vmem_limit_bytes=VMEM_LIMIT_FFN),
        name="ffn",
    )(x1, comb, p, wpg, wpp, weg, weu, wed, g2, b2)


def _tile(n, target):
    t = min(n, target)
    assert n % t == 0, (n, t)
    return t


def _row(v):
    return v.reshape(1, -1).astype(F32)


def kernel(x_prompt, x_sample, cache_k, cache_v, cache_kidx, state_conv, page_table, p_prompt, p_sample,
           ln0_g, ln0_b, w_in, b_gate, conv_w, conv_b, lnc_g, lnc_b, w_conv_out, w_attn_out, w_o,
           ln1_g, ln1_b, w_rg, b_rg, w_re, b_re, w_eg, w_eu, w_ed, w_pg, w_pp, ln2_g, ln2_b):
    depth = w_in.shape[0]
    assert depth == 1, "single-layer step"
    assert w_in.shape[1:] == (D_MODEL, COL_END)
    layer = 0
    alpha = (2.0 * depth) ** 0.25
    bp, tp, _ = x_prompt.shape
    bs, ts, _ = x_sample.shape
    assert ts == SUBLANES, "sample step length must fill one sublane tile"
    page = cache_k.shape[2]
    n_pages = page_table.shape[1]
    assert page == LANES

    g0, b0 = _row(ln0_g), _row(ln0_b)
    w = w_in[layer]
    wi_cols = jnp.pad(w[:, COL_WI:COL_GC], ((0, 0), (0, LANES - N_IDX_HEADS)))
    ws = jnp.concatenate([w[:, COL_GLU_A:COL_K], w[:, COL_QI:COL_KI], wi_cols], axis=1).astype(BF16)
    wt = w[:, COL_K:COL_QI].T
    wt = jnp.concatenate([wt, w[:, COL_KI:COL_WI].T], axis=0).astype(BF16)
    wg = w[:, COL_GC:].astype(BF16)
    bg = _row(b_gate[layer])
    wr = jnp.pad(jnp.concatenate([w_re[layer], w_rg[layer]], axis=1), ((0, 0), (0, LANES - N_EXPERTS - N_GROUPS)))
    wrh = wr.astype(BF16)
    wrl = (wr - wrh.astype(F32)).astype(BF16)
    br = jnp.pad(jnp.concatenate([b_re[layer], b_rg[layer]]), (0, LANES - N_EXPERTS - N_GROUPS)).reshape(1, LANES)
    wco, wao, wo = w_conv_out[layer].astype(BF16), w_attn_out[layer].astype(BF16), w_o[layer].astype(BF16)
    wpg, wpp = w_pg[layer].astype(BF16), w_pp[layer].astype(BF16)
    weg, weu, wed = w_eg[layer].astype(BF16), w_eu[layer].astype(BF16), w_ed[layer].astype(BF16)
    cw, cb = conv_w[layer], _row(conv_b[layer])
    lcg, lcb = _row(lnc_g[layer]), _row(lnc_b[layer])
    g1, b1, g2, b2 = _row(ln1_g[layer]), _row(ln1_b[layer]), _row(ln2_g[layer]), _row(ln2_b[layer])
    cache_kt = jnp.transpose(cache_k, (0, 1, 3, 4, 2))
    cache_vt = jnp.transpose(cache_v, (0, 1, 3, 4, 2))
    cache_kidx_t = jnp.transpose(cache_kidx, (0, 1, 3, 2))

    def tail(x1, comb, p):
        n = x1.shape[0]
        tm = _tile(n, 1024)
        rb = min(tm, tm // N_GROUPS + 4 * GROUP_ALIGN)
        return _ffn(x1, comb, p.reshape(n, PLE_DIM), wpg, wpp, weg, weu, wed, g2, b2, alpha, tm, rb, 4)

    def mix(x, c, o):
        n = x.shape[0]
        return _mix(x, c, o, g0, b0, wg, bg, wco, wao, wo, g1, b1, wrh, wrl, br, alpha, _tile(n, 1024))

    np_ = bp * tp
    xp = x_prompt.reshape(np_, D_MODEL)
    glu, q, qi, wi, kt, vt, ktb, vtb, kit, kitb = _proj(xp, g0, b0, ws, wt, bp, tp, _tile(tp, 1024))
    c, newconv_p = _conv(glu.reshape(bp, tp, CONV_CH), None, cw, cb, lcg, lcb, _tile(tp, 256))
    tq = _tile(tp, 256)
    o = jnp.concatenate(
        [_attn_prompt(qi, wi, q, kitb, ktb, vtb, bp, tp, tq, i, 1, (i + 1) * tq) for i in range(tp // tq)],
        axis=1).reshape(np_, ATT_W)
    x1, comb = mix(xp, c.reshape(np_, CONV_CH), o)
    y_prompt = tail(x1, comb, p_prompt[layer]).reshape(bp, tp, D_MODEL)
    new_k_p = jnp.transpose(kt, (0, 3, 1, 2))[None]
    new_v_p = jnp.transpose(vt, (0, 3, 1, 2))[None]
    new_ki_p = jnp.transpose(kit, (0, 2, 1))[None]

    ns = bs * ts
    xs = x_sample.reshape(ns, D_MODEL)
    glu, q, qi, wi, kt, vt, _, _, kit, kitb = _proj(xs, g0, b0, ws, wt, 1, ns, ns)
    c, newconv_s = _conv(glu.reshape(bs, ts, CONV_CH), state_conv[layer], cw, cb, lcg, lcb, ts)
    to_rows = lambda a: a.reshape(N_HEADS, bs, ts, HEAD_DIM).transpose(1, 0, 2, 3).reshape(bs, N_HEADS * ts, HEAD_DIM)
    qi_s, q_s = to_rows(qi), to_rows(q)
    wi_col = wi.reshape(bs, ts, N_IDX_HEADS).transpose(0, 2, 1).reshape(bs, N_IDX_HEADS * ts, 1)
    kt_s = kt.reshape(N_HEADS, HEAD_DIM, bs, ts)
    vt_s = vt.reshape(N_HEADS, HEAD_DIM, bs, ts)
    kit_s = kit.reshape(IDX_DIM, bs, ts)
    pad_keys = lambda a: jnp.pad(a, ((0, 0),) * (a.ndim - 1) + ((0, page - ts),))
    pt_flat = page_table.reshape(-1).astype(jnp.int32)
    pages_per = lambda target: max(d for d in range(1, target + 1) if n_pages % d == 0)
    scores_past, scores_new = _sample_scores(
        pt_flat, qi_s, wi_col, pad_keys(kitb.reshape(IDX_DIM, bs, ts).transpose(1, 0, 2)),
        cache_kidx_t, layer, pages_per(64))
    selp, seln = _sample_select(scores_past, scores_new, ts, _tile(ns, 64))
    o = _sample_attend(pt_flat, q_s, selp.reshape(bs, ts, -1), seln.reshape(bs, ts, LANES), cache_kt, cache_vt,
                       pad_keys(kt_s.transpose(2, 0, 1, 3)), pad_keys(vt_s.transpose(2, 0, 1, 3)),
                       layer, pages_per(32))
    x1, comb = mix(xs, c.reshape(ns, CONV_CH), o.reshape(ns, ATT_W).astype(BF16))
    y_sample = tail(x1, comb, p_sample[layer]).reshape(bs, ts, D_MODEL)
    new_k_s = kt_s.transpose(2, 3, 0, 1)[None]
    new_v_s = vt_s.transpose(2, 3, 0, 1)[None]
    new_ki_s = kit_s.transpose(1, 2, 0)[None]

    return (y_prompt, y_sample, new_k_p, new_v_p, new_ki_p, newconv_p[None],
            new_k_s, new_v_s, new_ki_s, newconv_s[None])
```

```python
import functools

import jax
import jax.numpy as jnp
from jax import lax
from jax.experimental import pallas as pl
from jax.experimental.pallas import tpu as pltpu

D_MODEL = 1024
CONV_CH = 512
CONV_WIDTH = 31
CONV_HIST = CONV_WIDTH - 1
N_HEADS = 8
HEAD_DIM = 64
ATT_W = N_HEADS * HEAD_DIM
N_IDX_HEADS = 8
IDX_DIM = 64
IDX_TOPK_MAX = 256
N_GROUPS = 4
EXPERTS_PER_GROUP = 8
N_EXPERTS = N_GROUPS * EXPERTS_PER_GROUP
D_EXPERT = 256
PLE_DIM = 256
LN_EPS = 1e-5
IDX_SCALE = (N_IDX_HEADS * IDX_DIM) ** -0.5
ATT_SCALE = HEAD_DIM ** -0.5
COL_GLU_A, COL_GLU_B, COL_Q, COL_K, COL_V, COL_QI, COL_KI, COL_WI, COL_GC = (
    0, 512, 1024, 1536, 2048, 2560, 3072, 3136, 3144)
COL_END = COL_GC + 2 * D_MODEL

LANES = 128
SUBLANES = 8
PREFIX_ROWS = 32
GROUP_LANE = LANES - 1
GROUP_ALIGN = 16
SEARCH_PARTS = 4
VMEM_LIMIT = 48 * 1024 * 1024
VMEM_LIMIT_FFN = 60 * 1024 * 1024

F32 = jnp.float32
BF16 = jnp.bfloat16
NEG_INF = float("-inf")
INT_MIN = -2 ** 31


def _cparams(sem):
    return pltpu.CompilerParams(dimension_semantics=sem, vmem_limit_bytes=VMEM_LIMIT)


def _layer_norm(x, g, b):
    mu = jnp.mean(x, axis=-1, keepdims=True)
    xc = x - mu
    var = jnp.mean(xc * xc, axis=-1, keepdims=True)
    return xc * lax.rsqrt(var + LN_EPS) * g + b


def _sigmoid(x):
    return 1.0 / (1.0 + jnp.exp(-x))


def _dot(a, b):
    return jnp.dot(a, b, preferred_element_type=F32)


def _dot_nt(a, b):
    return lax.dot_general(a, b, (((1,), (1,)), ((), ())), preferred_element_type=F32)


def _proj_kernel(x_ref, g0_ref, b0_ref, ws_ref, wt_ref,
                 glu_ref, q_ref, qi_ref, wi_ref, kt_ref, vt_ref, ktb_ref, vtb_ref, kit_ref, kitb_ref):
    h = _layer_norm(x_ref[...], g0_ref[...], b0_ref[...])
    hb = h.astype(BF16)

    def seg(col, width=512):
        return _dot(hb, ws_ref[:, col:col + width])

    glu_ref[...] = seg(0) * _sigmoid(seg(512))
    q = seg(1024) * ATT_SCALE
    qi = seg(1536)
    for hd in range(N_HEADS):
        sl = slice(hd * HEAD_DIM, (hd + 1) * HEAD_DIM)
        q_ref[hd] = q[:, sl].astype(BF16)
        qi_ref[hd] = qi[:, sl].astype(BF16)
    wi_ref[...] = seg(2048, LANES)[:, :N_IDX_HEADS]

    tm = hb.shape[0]
    kt = _dot_nt(wt_ref[0:ATT_W, :], hb).reshape(N_HEADS, HEAD_DIM, tm)
    kt_ref[...] = kt
    ktb_ref[...] = kt.astype(BF16)
    vt = _dot_nt(wt_ref[ATT_W:2 * ATT_W, :], hb).reshape(N_HEADS, HEAD_DIM, tm)
    vt_ref[...] = vt
    vtb_ref[...] = vt.astype(BF16)
    kit = _dot_nt(wt_ref[2 * ATT_W:2 * ATT_W + IDX_DIM, :], hb)
    kit_ref[...] = kit
    kitb_ref[...] = kit.astype(BF16)


def _proj(x, g0, b0, ws, wt, b, t, tm):
    n = b * t
    nt = t // tm
    row = lambda i: (i, 0)
    head = lambda i: (0, i, 0)
    full = lambda i: (0, 0)
    hm = jax.ShapeDtypeStruct((N_HEADS, n, HEAD_DIM), BF16)
    hm_spec = pl.BlockSpec((N_HEADS, tm, HEAD_DIM), head)
    kv_spec = pl.BlockSpec((None, N_HEADS, HEAD_DIM, tm), lambda i: (i // nt, 0, 0, i % nt))
    ki_spec = pl.BlockSpec((None, IDX_DIM, tm), lambda i: (i // nt, 0, i % nt))
    kv_shape = lambda dt: jax.ShapeDtypeStruct((b, N_HEADS, HEAD_DIM, t), dt)
    ki_shape = lambda dt: jax.ShapeDtypeStruct((b, IDX_DIM, t), dt)
    return pl.pallas_call(
        _proj_kernel,
        grid=(n // tm,),
        in_specs=[pl.BlockSpec((tm, D_MODEL), row),
                  pl.BlockSpec((1, D_MODEL), full), pl.BlockSpec((1, D_MODEL), full),
                  pl.BlockSpec(ws.shape, full), pl.BlockSpec(wt.shape, full)],
        out_specs=[pl.BlockSpec((tm, CONV_CH), row), hm_spec, hm_spec,
                   pl.BlockSpec((tm, N_IDX_HEADS), row),
                   kv_spec, kv_spec, kv_spec, kv_spec, ki_spec, ki_spec],
        out_shape=[jax.ShapeDtypeStruct((n, CONV_CH), F32), hm, hm,
                   jax.ShapeDtypeStruct((n, N_IDX_HEADS), F32),
                   kv_shape(F32), kv_shape(F32), kv_shape(BF16), kv_shape(BF16), ki_shape(F32), ki_shape(BF16)],
        compiler_params=_cparams(("parallel",)),
        name="proj",
    )(x, g0, b0, ws, wt)


def _conv_kernel(*refs, tc, nt, has_state):
    it = iter(refs)
    glu_ref = next(it)
    prev_ref = next(it) if nt > 1 else None
    state_ref = next(it) if has_state else None
    w_ref, cb_ref, g_ref, b_ref, c_ref, newconv_ref, buf_ref, shift_ref, acc_ref = it
    t = pl.program_id(1)

    @pl.when(t == 0)
    def _():
        buf_ref[0:PREFIX_ROWS, :] = jnp.zeros((PREFIX_ROWS, CONV_CH), F32)
        if has_state:
            buf_ref[2:PREFIX_ROWS, :] = state_ref[...]

    if nt > 1:
        @pl.when(t > 0)
        def _():
            buf_ref[0:PREFIX_ROWS, :] = prev_ref[...]

    buf_ref[PREFIX_ROWS:PREFIX_ROWS + tc, :] = glu_ref[...]

    shifted_rows = tc + PREFIX_ROWS - SUBLANES
    for s in range(1, SUBLANES):
        shift_ref[s - 1] = buf_ref[s:s + shifted_rows, :]

    rb = min(tc, 32)
    for c0 in range(0, CONV_CH, LANES):
        cs = slice(c0, c0 + LANES)
        bias = cb_ref[:, cs]
        for r0 in range(0, tc, rb):
            acc = jnp.zeros((rb, LANES), F32) + bias
            for j in range(CONV_WIDTH):
                a, s = divmod(2 + j, SUBLANES)
                lo = r0 + a * SUBLANES
                rows = buf_ref[lo:lo + rb, cs] if s == 0 else shift_ref[s - 1, lo:lo + rb, cs]
                acc = acc + w_ref[j:j + 1, cs] * rows
            acc_ref[r0:r0 + rb, cs] = acc

    y = _layer_norm(acc_ref[...], g_ref[...], b_ref[...])
    c_ref[...] = (y * _sigmoid(y)).astype(BF16)

    @pl.when(t == nt - 1)
    def _():
        newconv_ref[...] = buf_ref[tc + 2:tc + PREFIX_ROWS, :]


def _conv(glu, state, conv_w, conv_b, lnc_g, lnc_b, tc):
    b, t, _ = glu.shape
    nt = t // tc
    blocks_per_tile = tc // PREFIX_ROWS
    in_specs = [pl.BlockSpec((None, tc, CONV_CH), lambda i, j: (i, j, 0))]
    args = [glu]
    if nt > 1:
        in_specs.append(pl.BlockSpec((None, PREFIX_ROWS, CONV_CH),
                                     lambda i, j: (i, jnp.maximum(j * blocks_per_tile - 1, 0), 0)))
        args.append(glu)
    if state is not None:
        in_specs.append(pl.BlockSpec((None, CONV_HIST, CONV_CH), lambda i, j: (i, 0, 0)))
        args.append(state)
    full = lambda i, j: (0, 0)
    in_specs += [pl.BlockSpec((CONV_WIDTH, CONV_CH), full), pl.BlockSpec((1, CONV_CH), full),
                 pl.BlockSpec((1, CONV_CH), full), pl.BlockSpec((1, CONV_CH), full)]
    args += [conv_w, conv_b, lnc_g, lnc_b]
    return pl.pallas_call(
        functools.partial(_conv_kernel, tc=tc, nt=nt, has_state=state is not None),
        grid=(b, nt),
        in_specs=in_specs,
        out_specs=[pl.BlockSpec((None, tc, CONV_CH), lambda i, j: (i, j, 0)),
                   pl.BlockSpec((None, CONV_HIST, CONV_CH), lambda i, j: (i, 0, 0))],
        out_shape=[jax.ShapeDtypeStruct((b, t, CONV_CH), BF16),
                   jax.ShapeDtypeStruct((b, CONV_HIST, CONV_CH), F32)],
        scratch_shapes=[pltpu.VMEM((PREFIX_ROWS + tc, CONV_CH), F32),
                        pltpu.VMEM((SUBLANES - 1, tc + PREFIX_ROWS - SUBLANES, CONV_CH), F32),
                        pltpu.VMEM((tc, CONV_CH), F32)],
        compiler_params=_cparams(("parallel", "arbitrary")),
        name="conv",
    )(*args)


def _ordered_to_f32(c):
    return pltpu.bitcast(c ^ ((c >> 31) & jnp.int32(0x7FFFFFFF)), F32)


def _kth_largest(count_ge_parts, rows, topk):
    def body(i, taus):
        bit = lax.shift_left(jnp.int32(1), jnp.int32(31) - i)
        out = []
        for count_ge, tau in zip(count_ge_parts, taus):
            cand = tau ^ bit
            out.append(jnp.where(count_ge(_ordered_to_f32(cand)) >= topk, cand, tau))
        return tuple(out)
    init = tuple(jnp.full((rows, 1), INT_MIN, jnp.int32) for _ in count_ge_parts)
    return [_ordered_to_f32(t) for t in lax.fori_loop(0, 32, body, init, unroll=16)]


def _count(mask):
    ones = jnp.ones((mask.shape[1], LANES), BF16)
    return _dot(jnp.where(mask, 1.0, 0.0).astype(BF16), ones)[:, 0:1].astype(jnp.int32)


def _tie_rank(eq_chunks, rows):
    r = lax.broadcasted_iota(jnp.int32, (LANES, LANES), 0)
    c = lax.broadcasted_iota(jnp.int32, (LANES, LANES), 1)
    upper = jnp.where(r < c, 1.0, 0.0).astype(BF16)
    prefix = jnp.zeros((rows, 1), F32)
    for idx, eq in eq_chunks:
        eqf = jnp.where(eq, 1.0, 0.0)
        yield idx, _dot(eqf.astype(BF16), upper) + prefix
        prefix = prefix + jnp.sum(eqf, axis=1, keepdims=True)


def _attn_prompt_kernel(qi_ref, wi_ref, q_ref, kitb_ref, ktb_ref, vtb_ref, o_ref, sc_ref, bias_ref,
                        *, tq, t_eff, topk, first_block):
    qb = first_block + pl.program_id(1)
    wi = wi_ref[...]
    kitb = kitb_ref[...]
    acc = jnp.zeros((tq, t_eff), F32)
    for hd in range(N_IDX_HEADS):
        acc = acc + jnp.maximum(_dot(qi_ref[hd], kitb), 0.0) * wi[:, hd:hd + 1]
    q_pos = qb * tq + lax.broadcasted_iota(jnp.int32, (tq, 1), 0)
    k_pos = lax.broadcasted_iota(jnp.int32, (1, t_eff), 1)
    adm = k_pos <= q_pos
    sc_ref[...] = jnp.where(adm, acc * IDX_SCALE, NEG_INF)

    take_all = (q_pos + 1) <= topk
    if t_eff <= topk:
        tau = sc_ref[:, 0:1]
    else:
        part = tq // SEARCH_PARTS
        tau = jnp.concatenate(
            _kth_largest([functools.partial(lambda lo, cand: _count(sc_ref[lo:lo + part, :] >= cand), i * part)
                          for i in range(SEARCH_PARTS)], part, topk), axis=0)
    sc = sc_ref[...]
    eq = (sc == tau) & adm
    need = topk - _count(sc > tau)
    excess = jnp.max(jnp.where(take_all, 0, _count(eq) - need)) > 0

    @pl.when(jnp.logical_not(excess))
    def _():
        bias_ref[...] = jnp.where(adm & ((sc >= tau) | take_all), 0.0, NEG_INF)

    @pl.when(excess)
    def _():
        needf = need.astype(F32)
        chunks = [(i, (sc_ref[:, i * LANES:(i + 1) * LANES] == tau) & adm[:, i * LANES:(i + 1) * LANES])
                  for i in range(t_eff // LANES)]
        for i, rank in _tie_rank(chunks, tq):
            cs = slice(i * LANES, (i + 1) * LANES)
            kc = sc_ref[:, cs]
            sel = ((kc > tau) | ((kc == tau) & (rank < needf)) | take_all) & adm[:, cs]
            bias_ref[:, cs] = jnp.where(sel, 0.0, NEG_INF)

    bias = bias_ref[...]
    for hd in range(N_HEADS):
        logit = _dot(q_ref[hd], ktb_ref[hd]) + bias
        m = jnp.max(logit, axis=1, keepdims=True)
        p = jnp.exp(logit - m)
        denom = jnp.sum(p, axis=1, keepdims=True)
        o = _dot_nt(p.astype(BF16), vtb_ref[hd]) / denom
        o_ref[:, hd * HEAD_DIM:(hd + 1) * HEAD_DIM] = o.astype(BF16)


def _attn_prompt(qi, wi, q, kitb, ktb, vtb, b, t, tq, first_block, n_blocks, t_eff):
    nq = t // tq
    topk = min(IDX_TOPK_MAX, t // 4)
    assert (first_block + n_blocks) * tq <= t_eff
    qrow = lambda i, j: i * nq + first_block + j
    qspec = pl.BlockSpec((N_HEADS, tq, HEAD_DIM), lambda i, j: (0, qrow(i, j), 0))
    kspec = pl.BlockSpec((None, N_HEADS, HEAD_DIM, t_eff), lambda i, j: (i, 0, 0, 0))
    return pl.pallas_call(
        functools.partial(_attn_prompt_kernel, tq=tq, t_eff=t_eff, topk=topk, first_block=first_block),
        grid=(b, n_blocks),
        in_specs=[qspec, pl.BlockSpec((tq, N_IDX_HEADS), lambda i, j: (qrow(i, j), 0)), qspec,
                  pl.BlockSpec((None, IDX_DIM, t_eff), lambda i, j: (i, 0, 0)), kspec, kspec],
        out_specs=pl.BlockSpec((None, tq, ATT_W), lambda i, j: (i, j, 0)),
        out_shape=jax.ShapeDtypeStruct((b, n_blocks * tq, ATT_W), BF16),
        scratch_shapes=[pltpu.VMEM((tq, t_eff), F32), pltpu.VMEM((tq, t_eff), F32)],
        compiler_params=_cparams(("parallel", "arbitrary")),
        name="attn_prompt",
    )(qi, wi, q, kitb, ktb, vtb)


def _indexer_rows(qi, keys_t, wi_col):
    s = jnp.maximum(_dot(qi, keys_t), 0.0) * wi_col
    n = s.shape[1]
    return jnp.sum(s.reshape(N_IDX_HEADS, SUBLANES, n), axis=0) * IDX_SCALE


def _sample_scores_kernel(pt_ref, qi_ref, wi_ref, kin_ref, *refs, pages_per_step, page):
    del pt_ref
    page_refs, out_ref, new_ref = refs[:pages_per_step], refs[pages_per_step], refs[pages_per_step + 1]
    qi = qi_ref[...]
    wi_col = wi_ref[...]
    chunk = max(d for d in range(1, SUBLANES + 1) if pages_per_step % d == 0)
    for i in range(0, pages_per_step, chunk):
        keys = jnp.concatenate([page_refs[i + j][...].astype(BF16) for j in range(chunk)], axis=1)
        out_ref[:, i * page:(i + chunk) * page] = _indexer_rows(qi, keys, wi_col)

    @pl.when(pl.program_id(1) == 0)
    def _():
        new_ref[...] = _indexer_rows(qi, kin_ref[...], wi_col)


def _sample_scores(page_table, qi_s, wi_col, kitb_new_pad, cache_kidx_t, layer, pages_per_step):
    bsz = qi_s.shape[0]
    n_pages = page_table.shape[0] // bsz
    page = cache_kidx_t.shape[-1]
    steps = n_pages // pages_per_step

    def page_spec(i):
        return pl.BlockSpec((None, None, IDX_DIM, page),
                            lambda b, g, pt: (layer, pt[b * n_pages + g * pages_per_step + i], 0, 0))

    grid_spec = pltpu.PrefetchScalarGridSpec(
        num_scalar_prefetch=1,
        grid=(bsz, steps),
        in_specs=[pl.BlockSpec((None, N_IDX_HEADS * SUBLANES, IDX_DIM), lambda b, g, pt: (b, 0, 0)),
                  pl.BlockSpec((None, N_IDX_HEADS * SUBLANES, 1), lambda b, g, pt: (b, 0, 0)),
                  pl.BlockSpec((None, IDX_DIM, LANES), lambda b, g, pt: (b, 0, 0))]
                 + [page_spec(i) for i in range(pages_per_step)],
        out_specs=[pl.BlockSpec((SUBLANES, pages_per_step * page), lambda b, g, pt: (b, g)),
                   pl.BlockSpec((SUBLANES, LANES), lambda b, g, pt: (b, 0))],
    )
    return pl.pallas_call(
        functools.partial(_sample_scores_kernel, pages_per_step=pages_per_step, page=page),
        grid_spec=grid_spec,
        out_shape=[jax.ShapeDtypeStruct((bsz * SUBLANES, n_pages * page), F32),
                   jax.ShapeDtypeStruct((bsz * SUBLANES, LANES), F32)],
        compiler_params=_cparams(("parallel", "arbitrary")),
        name="sample_scores",
    )(page_table, qi_s, wi_col, kitb_new_pad, *([cache_kidx_t] * pages_per_step))


def _sample_select_kernel(sp_ref, sn_ref, selp_ref, seln_ref, *, rows, t_new, past, topk):
    jn = lax.broadcasted_iota(jnp.int32, (rows, LANES), 1)
    tn = lax.broadcasted_iota(jnp.int32, (rows, LANES), 0) & (SUBLANES - 1)
    adm_n = (jn <= tn) & (jn < t_new)
    sn = jnp.where(adm_n, sn_ref[...], NEG_INF)

    half = rows // 2
    tau = jnp.concatenate(
        _kth_largest([lambda cand: _count(sp_ref[0:half, :] >= cand) + _count(sn[0:half] >= cand),
                      lambda cand: _count(sp_ref[half:rows, :] >= cand) + _count(sn[half:rows] >= cand)],
                     half, topk), axis=0)
    sp = sp_ref[...]
    need = topk - (_count(sp > tau) + _count(sn > tau))
    eqn = (sn == tau) & adm_n
    excess = jnp.max(_count(sp == tau) + _count(eqn) - need) > 0

    @pl.when(jnp.logical_not(excess))
    def _():
        selp_ref[...] = jnp.where(sp >= tau, 1.0, 0.0)
        seln_ref[...] = jnp.where((sn >= tau) & adm_n, 1.0, 0.0)

    @pl.when(excess)
    def _():
        needf = need.astype(F32)
        n_chunks = past // LANES
        chunks = [(i, sp_ref[:, i * LANES:(i + 1) * LANES] == tau) for i in range(n_chunks)]
        chunks.append((n_chunks, eqn))
        for i, rank in _tie_rank(chunks, rows):
            if i < n_chunks:
                kc = sp_ref[:, i * LANES:(i + 1) * LANES]
                sel = (kc > tau) | ((kc == tau) & (rank < needf))
                selp_ref[:, i * LANES:(i + 1) * LANES] = jnp.where(sel, 1.0, 0.0)
            else:
                sel = ((sn > tau) | ((sn == tau) & (rank < needf))) & adm_n
                seln_ref[...] = jnp.where(sel, 1.0, 0.0)


def _sample_select(scores_past, scores_new, t_new, rows):
    n, past = scores_past.shape
    assert past >= IDX_TOPK_MAX, "every query must see at least top-k admissible keys"
    assert t_new == SUBLANES and n % rows == 0
    topk = min(IDX_TOPK_MAX, (past + t_new) // 4)
    blk = lambda w: pl.BlockSpec((rows, w), lambda i: (i, 0))
    return pl.pallas_call(
        functools.partial(_sample_select_kernel, rows=rows, t_new=t_new, past=past, topk=topk),
        grid=(n // rows,),
        in_specs=[blk(past), blk(LANES)],
        out_specs=[blk(past), blk(LANES)],
        out_shape=[jax.ShapeDtypeStruct((n, past), F32), jax.ShapeDtypeStruct((n, LANES), F32)],
        compiler_params=_cparams(("parallel",)),
        name="sample_select",
    )(scores_past, scores_new)


def _sample_attend_kernel(pt_ref, q_ref, selp_ref, seln_ref, *refs, n_groups, pages_per_step, page):
    del pt_ref
    g = pages_per_step
    k_refs, v_refs = refs[:g], refs[g:2 * g]
    kn_ref, vn_ref, o_ref, m_ref, l_ref, acc_ref = refs[2 * g:]
    step = pl.program_id(1)
    rows = N_HEADS * SUBLANES

    @pl.when(step == 0)
    def _():
        m_ref[...] = jnp.full((rows, 1), NEG_INF, F32)
        l_ref[...] = jnp.zeros((rows, 1), F32)
        acc_ref[...] = jnp.zeros((rows, ATT_W), F32)

    qrow = lax.broadcasted_iota(jnp.int32, (rows, ATT_W), 0)
    qcol = lax.broadcasted_iota(jnp.int32, (rows, ATT_W), 1)
    q_bd = jnp.where((qcol >> 6) == (qrow >> 3), jnp.concatenate([q_ref[...]] * N_HEADS, axis=1), 0.0).astype(BF16)

    def process(k_pages, v_pages, sel):
        logit = jnp.concatenate([_dot(q_bd, kp.reshape(ATT_W, page).astype(BF16)) for kp in k_pages], axis=1)
        valid = jnp.concatenate([sel] * N_HEADS, axis=0) > 0.5
        logit = jnp.where(valid, logit, NEG_INF)
        m_old = m_ref[...]
        m_new = jnp.maximum(m_old, jnp.max(logit, axis=1, keepdims=True))
        m_safe = jnp.where(m_new == NEG_INF, 0.0, m_new)
        pr = jnp.exp(logit - m_safe)
        alpha = jnp.exp(m_old - m_safe)
        l_ref[...] = alpha * l_ref[...] + jnp.sum(pr, axis=1, keepdims=True)
        prb = pr.astype(BF16)
        pv = jnp.zeros((rows, ATT_W), F32)
        for i, vp in enumerate(v_pages):
            pv = pv + _dot_nt(prb[:, i * page:(i + 1) * page], vp.reshape(ATT_W, page).astype(BF16))
        acc_ref[...] = alpha * acc_ref[...] + pv
        m_ref[...] = m_new

    process([r[...] for r in k_refs], [r[...] for r in v_refs], selp_ref[...])

    @pl.when(step == n_groups - 1)
    def _():
        process([kn_ref[...]], [vn_ref[...]], seln_ref[...])
        out = acc_ref[...] / l_ref[...]
        for hd in range(N_HEADS):
            o_ref[:, hd * HEAD_DIM:(hd + 1) * HEAD_DIM] = out[hd * SUBLANES:(hd + 1) * SUBLANES,
                                                              hd * HEAD_DIM:(hd + 1) * HEAD_DIM]


def _sample_attend(page_table, q_s, selp, seln, cache_kt, cache_vt, kt_new_pad, vt_new_pad, layer, pages_per_step):
    bsz = q_s.shape[0]
    n_pages = page_table.shape[0] // bsz
    page = cache_kt.shape[-1]
    g = pages_per_step
    n_groups = n_pages // g

    def page_spec(i):
        return pl.BlockSpec(
            (None, None, N_HEADS, HEAD_DIM, page),
            lambda b, s, pt: (layer, pt[b * n_pages + s * g + i], 0, 0, 0))

    new_spec = pl.BlockSpec((None, N_HEADS, HEAD_DIM, page), lambda b, s, pt: (b, 0, 0, 0))
    rows = N_HEADS * SUBLANES
    grid_spec = pltpu.PrefetchScalarGridSpec(
        num_scalar_prefetch=1,
        grid=(bsz, n_groups),
        in_specs=[pl.BlockSpec((None, rows, HEAD_DIM), lambda b, s, pt: (b, 0, 0)),
                  pl.BlockSpec((None, SUBLANES, g * page), lambda b, s, pt: (b, 0, s)),
                  pl.BlockSpec((None, SUBLANES, LANES), lambda b, s, pt: (b, 0, 0))]
                 + [page_spec(i) for i in range(g)] + [page_spec(i) for i in range(g)]
                 + [new_spec, new_spec],
        out_specs=pl.BlockSpec((None, SUBLANES, ATT_W), lambda b, s, pt: (b, 0, 0)),
        scratch_shapes=[pltpu.VMEM((rows, 1), F32), pltpu.VMEM((rows, 1), F32), pltpu.VMEM((rows, ATT_W), F32)],
    )
    return pl.pallas_call(
        functools.partial(_sample_attend_kernel, n_groups=n_groups, pages_per_step=g, page=page),
        grid_spec=grid_spec,
        out_shape=jax.ShapeDtypeStruct((bsz, SUBLANES, ATT_W), F32),
        compiler_params=_cparams(("parallel", "arbitrary")),
        name="sample_attend",
    )(page_table, q_s, selp, seln, *([cache_kt] * g), *([cache_vt] * g), kt_new_pad, vt_new_pad)


def _mix_kernel(x_ref, c_ref, o_ref, g0_ref, b0_ref, wg_ref, bg_ref, wco_ref, wao_ref, wo_ref,
                g1_ref, b1_ref, wrh_ref, wrl_ref, br_ref, x1_ref, comb_ref, *, alpha):
    h = _layer_norm(x_ref[...], g0_ref[...], b0_ref[...])
    hb = h.astype(BF16)
    gc = _sigmoid(_dot(hb, wg_ref[:, :D_MODEL]) + bg_ref[:, :D_MODEL])
    ga = _sigmoid(_dot(hb, wg_ref[:, D_MODEL:]) + bg_ref[:, D_MODEL:])
    merged = gc * _dot(c_ref[...], wco_ref[...]) + ga * _dot(o_ref[...], wao_ref[...])
    mix = _dot(merged.astype(BF16), wo_ref[...])
    x1 = _layer_norm(alpha * h + mix, g1_ref[...], b1_ref[...])
    x1_ref[...] = x1

    xh = x1.astype(BF16)
    xl = (x1 - xh.astype(F32)).astype(BF16)
    logits = _dot(xh, wrh_ref[...]) + (_dot(xl, wrh_ref[...]) + _dot(xh, wrl_ref[...])) + br_ref[...]
    lane = lax.broadcasted_iota(jnp.int32, logits.shape, 1)
    is_grp = (lane >= N_EXPERTS) & (lane < N_EXPERTS + N_GROUPS)
    gl = jnp.where(is_grp, logits, NEG_INF)
    gmax = jnp.max(gl, axis=1, keepdims=True)
    grp = jnp.min(jnp.where(gl == gmax, lane, 4 * LANES), axis=1, keepdims=True) - N_EXPERTS
    p_grp = 1.0 / jnp.sum(jnp.exp(gl - gmax), axis=1, keepdims=True)
    in_grp = (lane < N_EXPERTS) & ((lane >> 3) == grp)
    el = jnp.where(in_grp, logits, NEG_INF)
    emax = jnp.max(el, axis=1, keepdims=True)
    ee = jnp.exp(el - emax)
    pe = ee / jnp.sum(ee, axis=1, keepdims=True)
    pe = jnp.where(in_grp, pe, -1.0)
    p1 = jnp.max(pe, axis=1, keepdims=True)
    i1 = jnp.min(jnp.where(pe == p1, lane, 4 * LANES), axis=1, keepdims=True)
    pe2 = jnp.where(lane == i1, -1.0, pe)
    p2 = jnp.max(pe2, axis=1, keepdims=True)
    i2 = jnp.min(jnp.where(pe2 == p2, lane, 4 * LANES), axis=1, keepdims=True)
    tot = p1 + p2
    comb_ref[...] = (jnp.where(lane == i1, p_grp * (p1 / tot), 0.0)
                     + jnp.where(lane == i2, p_grp * (p2 / tot), 0.0)
                     + jnp.where(lane == GROUP_LANE, grp.astype(F32), 0.0))


def _mix(x, c, o, g0, b0, wg, bg, wco, wao, wo, g1, b1, wrh, wrl, br, alpha, tm):
    n = x.shape[0]
    row = lambda i: (i, 0)
    full = lambda i: (0, 0)
    fs = lambda a: pl.BlockSpec(a.shape, full)
    return pl.pallas_call(
        functools.partial(_mix_kernel, alpha=alpha),
        grid=(n // tm,),
        in_specs=[pl.BlockSpec((tm, D_MODEL), row), pl.BlockSpec((tm, CONV_CH), row),
                  pl.BlockSpec((tm, ATT_W), row)] + [fs(a) for a in (g0, b0, wg, bg, wco, wao, wo, g1, b1, wrh, wrl, br)],
        out_specs=[pl.BlockSpec((tm, D_MODEL), row), pl.BlockSpec((tm, LANES), row)],
        out_shape=[jax.ShapeDtypeStruct((n, D_MODEL), F32), jax.ShapeDtypeStruct((n, LANES), F32)],
        compiler_params=_cparams(("parallel",)),
        name="mix",
    )(x, c, o, g0, b0, wg, bg, wco, wao, wo, g1, b1, wrh, wrl, br)


def _split3(a):
    hi = a.astype(BF16)
    r1 = a - hi.astype(F32)
    mid = r1.astype(BF16)
    lo = (r1 - mid.astype(F32)).astype(BF16)
    return hi, mid, lo


def _permute_rows(perm, a):
    hi, mid, lo = _split3(a)
    return _dot(perm, hi) + _dot(perm, mid) + _dot(perm, lo)


def _ffn_kernel(x1_ref, comb_ref, p_ref, wpg_ref, wpp_ref, weg_ref, weu_ref, wed_ref, g2_ref, b2_ref,
                y_ref, to_slot_ref, to_token_ref, xs_ref, cs_ref, acc_ref, meta_ref,
                *, alpha, tm, rb, eps, slots):
    s = pl.program_id(1)

    @pl.when(s == 0)
    def _():
        comb = comb_ref[...]
        lane = lax.broadcasted_iota(jnp.int32, (tm, LANES), 1)
        lane_row = lax.broadcasted_iota(jnp.int32, (1, LANES), 1)
        grp = comb[:, GROUP_LANE:GROUP_LANE + 1]
        onehot = jnp.where((lane.astype(F32) == grp) & (lane < N_GROUPS), 1.0, 0.0)
        r = lax.broadcasted_iota(jnp.int32, (tm, tm), 0)
        c = lax.broadcasted_iota(jnp.int32, (tm, tm), 1)
        earlier = jnp.where(c < r, 1.0, 0.0).astype(BF16)
        before = _dot(earlier, onehot.astype(BF16))
        cnt = jnp.sum(onehot, axis=0, keepdims=True)
        base = jnp.zeros((1, LANES), F32)
        run = jnp.zeros((1, 1), F32)
        for g in range(N_GROUPS):
            base = base + jnp.where(lane_row == g, run, 0.0)
            meta_ref[g] = run[0, 0].astype(jnp.int32)
            meta_ref[N_GROUPS + g] = cnt[0, g].astype(jnp.int32)
            run = jnp.floor((run + cnt[:, g:g + 1] + (GROUP_ALIGN - 1.0)) * (1.0 / GROUP_ALIGN)) * GROUP_ALIGN
        slot = jnp.sum(onehot * (before + base), axis=1, keepdims=True)
        slot_row = jnp.transpose(jnp.broadcast_to(slot, (tm, LANES)))[0:1, :]
        slot_c = lax.broadcasted_iota(jnp.int32, (tm, slots), 1).astype(F32)
        slot_r = lax.broadcasted_iota(jnp.int32, (slots, tm), 0).astype(F32)
        to_slot_ref[...] = jnp.where(slot == slot_c, 1.0, 0.0).astype(BF16)
        to_token = jnp.where(slot_row == slot_r, 1.0, 0.0).astype(BF16)
        to_token_ref[...] = to_token
        xs_ref[0:slots, :] = _dot(to_token, x1_ref[...].astype(BF16)).astype(BF16)
        cs_ref[0:slots, :] = _permute_rows(to_token, comb)
        xs_ref[slots:, :] = jnp.zeros((rb, D_MODEL), BF16)
        cs_ref[slots:, :] = jnp.zeros((rb, LANES), F32)
        acc_ref[...] = jnp.zeros((slots + rb, D_MODEL), F32)

    g = (s * eps) // EXPERTS_PER_GROUP
    first = meta_ref[g]
    count = meta_ref[N_GROUPS + g]
    for blk in range(-(-tm // rb)):
        @pl.when(blk * rb < count)
        def _():
            rows = pl.ds(pl.multiple_of(first + blk * rb, GROUP_ALIGN), rb)
            xb = xs_ref[rows, :]
            cs = cs_ref[rows, :]
            lane = lax.broadcasted_iota(jnp.int32, (rb, LANES), 1)
            out = jnp.zeros((rb, D_MODEL), F32)
            for j in range(eps):
                ce = jnp.sum(jnp.where(lane == s * eps + j, cs, 0.0), axis=1, keepdims=True)
                hg = _dot(xb, weg_ref[j])
                hu = _dot(xb, weu_ref[j])
                hidden = (hg * _sigmoid(hg)) * hu * ce
                out = out + _dot(hidden.astype(BF16), wed_ref[j])
            acc_ref[rows, :] += out

    @pl.when(s == N_EXPERTS // eps - 1)
    def _():
        x1 = x1_ref[...]
        ffn = _permute_rows(to_slot_ref[...], acc_ref[0:slots, :])
        ple = _sigmoid(_dot(x1.astype(BF16), wpg_ref[...])) * _dot(p_ref[...].astype(BF16), wpp_ref[...])
        y_ref[...] = _layer_norm(alpha * x1 + ffn + ple, g2_ref[...], b2_ref[...])


def _ffn(x1, comb, p, wpg, wpp, weg, weu, wed, g2, b2, alpha, tm, rb, eps):
    n = x1.shape[0]
    assert rb % GROUP_ALIGN == 0 and EXPERTS_PER_GROUP % eps == 0
    slots = -(-(tm + N_GROUPS * GROUP_ALIGN) // LANES) * LANES
    row = lambda i, e: (i, 0)
    full = lambda i, e: (0, 0)
    exp = lambda i, e: (e, 0, 0)
    return pl.pallas_call(
        functools.partial(_ffn_kernel, alpha=alpha, tm=tm, rb=rb, eps=eps, slots=slots),
        grid=(n // tm, N_EXPERTS // eps),
        in_specs=[pl.BlockSpec((tm, D_MODEL), row), pl.BlockSpec((tm, LANES), row),
                  pl.BlockSpec((tm, PLE_DIM), row),
                  pl.BlockSpec((D_MODEL, D_MODEL), full), pl.BlockSpec((PLE_DIM, D_MODEL), full),
                  pl.BlockSpec((eps, D_MODEL, D_EXPERT), exp), pl.BlockSpec((eps, D_MODEL, D_EXPERT), exp),
                  pl.BlockSpec((eps, D_EXPERT, D_MODEL), exp),
                  pl.BlockSpec((1, D_MODEL), full), pl.BlockSpec((1, D_MODEL), full)],
        out_specs=pl.BlockSpec((tm, D_MODEL), row),
        out_shape=jax.ShapeDtypeStruct((n, D_MODEL), F32),
        scratch_shapes=[pltpu.VMEM((tm, slots), BF16), pltpu.VMEM((slots, tm), BF16),
                        pltpu.VMEM((slots + rb, D_MODEL), BF16), pltpu.VMEM((slots + rb, LANES), F32),
                        pltpu.VMEM((slots + rb, D_MODEL), F32), pltpu.SMEM((2 * N_GROUPS,), jnp.int32)],
        compiler_params=pltpu.CompilerParams(dimension_semantics=("parallel", "arbitrary"),
                                             vmem_limit_bytes=VMEM_LIMIT_FFN),
        name="ffn",
    )(x1, comb, p, wpg, wpp, weg, weu, wed, g2, b2)


def _tile(n, target):
    t = min(n, target)
    assert n % t == 0, (n, t)
    return t


def _row(v):
    return v.reshape(1, -1).astype(F32)


def kernel(x_prompt, x_sample, cache_k, cache_v, cache_kidx, state_conv, page_table, p_prompt, p_sample,
           ln0_g, ln0_b, w_in, b_gate, conv_w, conv_b, lnc_g, lnc_b, w_conv_out, w_attn_out, w_o,
           ln1_g, ln1_b, w_rg, b_rg, w_re, b_re, w_eg, w_eu, w_ed, w_pg, w_pp, ln2_g, ln2_b):
    depth = w_in.shape[0]
    assert depth == 1, "single-layer step"
    assert w_in.shape[1:] == (D_MODEL, COL_END)
    layer = 0
    alpha = (2.0 * depth) ** 0.25
    bp, tp, _ = x_prompt.shape
    bs, ts, _ = x_sample.shape
    assert ts == SUBLANES, "sample step length must fill one sublane tile"
    page = cache_k.shape[2]
    n_pages = page_table.shape[1]
    assert page == LANES

    g0, b0 = _row(ln0_g), _row(ln0_b)
    w = w_in[layer]
    wi_cols = jnp.pad(w[:, COL_WI:COL_GC], ((0, 0), (0, LANES - N_IDX_HEADS)))
    ws = jnp.concatenate([w[:, COL_GLU_A:COL_K], w[:, COL_QI:COL_KI], wi_cols], axis=1).astype(BF16)
    wt = w[:, COL_K:COL_QI].T
    wt = jnp.concatenate([wt, w[:, COL_KI:COL_WI].T], axis=0).astype(BF16)
    wg = w[:, COL_GC:].astype(BF16)
    bg = _row(b_gate[layer])
    wr = jnp.pad(jnp.concatenate([w_re[layer], w_rg[layer]], axis=1), ((0, 0), (0, LANES - N_EXPERTS - N_GROUPS)))
    wrh = wr.astype(BF16)
    wrl = (wr - wrh.astype(F32)).astype(BF16)
    br = jnp.pad(jnp.concatenate([b_re[layer], b_rg[layer]]), (0, LANES - N_EXPERTS - N_GROUPS)).reshape(1, LANES)
    wco, wao, wo = w_conv_out[layer].astype(BF16), w_attn_out[layer].astype(BF16), w_o[layer].astype(BF16)
    wpg, wpp = w_pg[layer].astype(BF16), w_pp[layer].astype(BF16)
    weg, weu, wed = w_eg[layer].astype(BF16), w_eu[layer].astype(BF16), w_ed[layer].astype(BF16)
    cw, cb = conv_w[layer], _row(conv_b[layer])
    lcg, lcb = _row(lnc_g[layer]), _row(lnc_b[layer])
    g1, b1, g2, b2 = _row(ln1_g[layer]), _row(ln1_b[layer]), _row(ln2_g[layer]), _row(ln2_b[layer])
    cache_kt = jnp.transpose(cache_k, (0, 1, 3, 4, 2))
    cache_vt = jnp.transpose(cache_v, (0, 1, 3, 4, 2))
    cache_kidx_t = jnp.transpose(cache_kidx, (0, 1, 3, 2))

    def tail(x1, comb, p):
        n = x1.shape[0]
        tm = _tile(n, 1024)
        rb = min(tm, tm // N_GROUPS + 4 * GROUP_ALIGN)
        return _ffn(x1, comb, p.reshape(n, PLE_DIM), wpg, wpp, weg, weu, wed, g2, b2, alpha, tm, rb, 4)

    def mix(x, c, o):
        n = x.shape[0]
        return _mix(x, c, o, g0, b0, wg, bg, wco, wao, wo, g1, b1, wrh, wrl, br, alpha, _tile(n, 1024))

    np_ = bp * tp
    xp = x_prompt.reshape(np_, D_MODEL)
    glu, q, qi, wi, kt, vt, ktb, vtb, kit, kitb = _proj(xp, g0, b0, ws, wt, bp, tp, _tile(tp, 1024))
    c, newconv_p = _conv(glu.reshape(bp, tp, CONV_CH), None, cw, cb, lcg, lcb, _tile(tp, 256))
    tq = _tile(tp, 256)
    o = jnp.concatenate(
        [_attn_prompt(qi, wi, q, kitb, ktb, vtb, bp, tp, tq, i, 1, (i + 1) * tq) for i in range(tp // tq)],
        axis=1).reshape(np_, ATT_W)
    x1, comb = mix(xp, c.reshape(np_, CONV_CH), o)
    y_prompt = tail(x1, comb, p_prompt[layer]).reshape(bp, tp, D_MODEL)
    new_k_p = jnp.transpose(kt, (0, 3, 1, 2))[None]
    new_v_p = jnp.transpose(vt, (0, 3, 1, 2))[None]
    new_ki_p = jnp.transpose(kit, (0, 2, 1))[None]

    ns = bs * ts
    xs = x_sample.reshape(ns, D_MODEL)
    glu, q, qi, wi, kt, vt, _, _, kit, kitb = _proj(xs, g0, b0, ws, wt, 1, ns, ns)
    c, newconv_s = _conv(glu.reshape(bs, ts, CONV_CH), state_conv[layer], cw, cb, lcg, lcb, ts)
    to_rows = lambda a: a.reshape(N_HEADS, bs, ts, HEAD_DIM).transpose(1, 0, 2, 3).reshape(bs, N_HEADS * ts, HEAD_DIM)
    qi_s, q_s = to_rows(qi), to_rows(q)
    wi_col = wi.reshape(bs, ts, N_IDX_HEADS).transpose(0, 2, 1).reshape(bs, N_IDX_HEADS * ts, 1)
    kt_s = kt.reshape(N_HEADS, HEAD_DIM, bs, ts)
    vt_s = vt.reshape(N_HEADS, HEAD_DIM, bs, ts)
    kit_s = kit.reshape(IDX_DIM, bs, ts)
    pad_keys = lambda a: jnp.pad(a, ((0, 0),) * (a.ndim - 1) + ((0, page - ts),))
    pt_flat = page_table.reshape(-1).astype(jnp.int32)
    pages_per = lambda target: max(d for d in range(1, target + 1) if n_pages % d == 0)
    scores_past, scores_new = _sample_scores(
        pt_flat, qi_s, wi_col, pad_keys(kitb.reshape(IDX_DIM, bs, ts).transpose(1, 0, 2)),
        cache_kidx_t, layer, pages_per(64))
    selp, seln = _sample_select(scores_past, scores_new, ts, _tile(ns, 64))
    o = _sample_attend(pt_flat, q_s, selp.reshape(bs, ts, -1), seln.reshape(bs, ts, LANES), cache_kt, cache_vt,
                       pad_keys(kt_s.transpose(2, 0, 1, 3)), pad_keys(vt_s.transpose(2, 0, 1, 3)),
                       layer, pages_per(32))
    x1, comb = mix(xs, c.reshape(ns, CONV_CH), o.reshape(ns, ATT_W).astype(BF16))
    y_sample = tail(x1, comb, p_sample[layer]).reshape(bs, ts, D_MODEL)
    new_k_s = kt_s.transpose(2, 3, 0, 1)[None]
    new_v_s = vt_s.transpose(2, 3, 0, 1)[None]
    new_ki_s = kit_s.transpose(1, 2, 0)[None]

    return (y_prompt, y_sample, new_k_p, new_v_p, new_ki_p, newconv_p[None],
            new_k_s, new_v_s, new_ki_s, newconv_s[None])
```
